```python
import math
import jax, jax.numpy as jnp
from jax import lax
import numpy as np

D_MODEL = 2048
BATCH = 2
SEQ = 16384
DEPTH = 2
DEC_BATCH = 32
DEC_SEQ = 32
PAST_LEN = 4096

CHUNK = 64
GDN_HEADS = 8
GDN_DK = 128
GDN_DV = 128
GDN_W = GDN_HEADS * GDN_DK
GDN_QKV = 3 * GDN_W
CONV_W = 4
RWKV_HEADS = 16
RWKV_HD = 64
RWKV_W = RWKV_HEADS * RWKV_HD
W_LORA = 96
A_LORA = 96
RWKV_SHIFT = 3 * RWKV_W + W_LORA + A_LORA
N_BRANCH = 2
EPS = 1e-6
GN_EPS = 64e-5
W_DECAY_OFFSET = 0.5

kernel_name = "gdn_rwkv7_parallel_adaln_stream"


def _in_widths():
    return [GDN_QKV, GDN_W, GDN_HEADS, GDN_HEADS, RWKV_SHIFT, RWKV_W, N_BRANCH * D_MODEL]


def _split_points():
    pts, acc = [], 0
    for w in _in_widths()[:-1]:
        acc += w
        pts.append(acc)
    return pts


def _rms_norm(x, g):
    xf = x.astype(jnp.float32)
    return xf * lax.rsqrt(jnp.mean(xf * xf, axis=-1, keepdims=True) + EPS) * g.astype(jnp.float32)


def _l2norm(x):
    return x * lax.rsqrt(jnp.sum(x * x, axis=-1, keepdims=True) + EPS)


def _group_norm(y, w, b):
    mu = jnp.mean(y, axis=-1, keepdims=True)
    var = jnp.mean(jnp.square(y - mu), axis=-1, keepdims=True)
    yn = (y - mu) * lax.rsqrt(var + GN_EPS)
    H, D = y.shape[-2], y.shape[-1]
    return yn * w.astype(jnp.float32).reshape(H, D) + b.astype(jnp.float32).reshape(H, D)


def _causal_dwconv(u, prev, w):
    T = u.shape[1]
    full = jnp.concatenate([prev, u], axis=1)
    y = full[:, 0:T] * w[0]
    for i in range(1, CONV_W):
        y = y + full[:, i:i + T] * w[i]
    return y, full[:, T:]


def _token_shift(p, prev, mu):
    full = jnp.concatenate([prev, p], axis=1)
    p_prev = full[:, :-1]
    return p + (p_prev - p) * mu, full[:, -1:]


def _gdn_chunked(q, k, v, beta, g, S0):
    B, T, H, DK = q.shape
    DV = v.shape[-1]
    L = min(CHUNK, T)
    pad = (-T) % L
    if pad:
        padf = lambda a: jnp.pad(a, [(0, 0), (0, pad)] + [(0, 0)] * (a.ndim - 2))
        q, k, v, beta, g = padf(q), padf(k), padf(v), padf(beta), padf(g)
    N = (T + pad) // L

    def blk(a):
        return a.reshape((B, N, L, H) + a.shape[3:]).swapaxes(2, 3)

    q, k, v, beta, g = blk(q), blk(k), blk(v), blk(beta), blk(g)
    gc = jnp.cumsum(g, axis=-1)
    causal = jnp.tril(jnp.ones((L, L), dtype=bool))
    strict = jnp.tril(jnp.ones((L, L), dtype=bool), -1)
    decay = jnp.exp(jnp.where(causal, gc[..., :, None] - gc[..., None, :], -jnp.inf))
    kk = jnp.einsum('bnhld,bnhmd->bnhlm', k, k)
    A = jnp.where(strict, beta[..., None] * kk * decay, 0.0)
    IA = A + jnp.eye(L, dtype=A.dtype)
    rhs = jnp.concatenate([beta[..., None] * v, (beta * jnp.exp(gc))[..., None] * k], axis=-1)
    sol = lax.linalg.triangular_solve(IA, rhs, left_side=True, lower=True, unit_diagonal=True)
    Uv, Wk = sol[..., :DV], sol[..., DV:]
    qk = jnp.einsum('bnhld,bnhmd->bnhlm', q, k) * decay
    q_dec = q * jnp.exp(gc)[..., None]
    k_dec = k * jnp.exp(gc[..., -1:] - gc)[..., None]
    g_last = jnp.exp(gc[..., -1])

    def step(S, xs):
        uv, wk, qkn, qd, kd, gl = xs
        U = uv - jnp.einsum('bhlk,bhkv->bhlv', wk, S)
        o = jnp.einsum('bhlk,bhkv->bhlv', qd, S) + jnp.einsum('bhlm,bhmv->bhlv', qkn, U)
        S = S * gl[..., None, None] + jnp.einsum('bhlk,bhlv->bhkv', kd, U)
        return S, o

    xs = tuple(a.swapaxes(0, 1) for a in (Uv, Wk, qk, q_dec, k_dec, g_last))
    S, o = lax.scan(step, S0, xs)
    o = o.transpose(1, 0, 3, 2, 4).reshape(B, N * L, H, DV)[:, :T]
    return o, S


def _rwkv7_scan(r, w, k, v, kk, a, S0):
    def step(S, xs):
        r_t, w_t, k_t, v_t, kk_t, a_t = xs
        sk = jnp.einsum('bhk,bhkv->bhv', kk_t, S)
        S = w_t[..., None] * S - (a_t * kk_t)[..., None] * sk[..., None, :] + k_t[..., None] * v_t[..., None, :]
        y = jnp.einsum('bhk,bhkv->bhv', r_t, S)
        return S, y

    xs = tuple(t.swapaxes(0, 1) for t in (r, w, k, v, kk, a))
    S, y = lax.scan(step, S0, xs)
    return y.swapaxes(0, 1), S


def _layer(x, c, conv_prev, s_gdn, shift_prev, s_rwkv,
           ada_w, ada_b, norm_g, w_in, conv_w, a_log, dt_bias, onorm_g,
           mu, w0, w_lora, a0, a_lora, k_k, k_a, r_k, ln_w, ln_b,
           w_o_gdn, w_o_rwkv, w_out):
    f32 = jnp.float32
    dt = x.dtype
    B, T, _ = x.shape
    mod = (c @ ada_w + ada_b).astype(f32)[:, None, :]
    shift, scale, gate = jnp.split(mod, 3, axis=-1)
    h = (_rms_norm(x, norm_g) * (1.0 + scale) + shift).astype(dt)
    proj = h @ w_in
    p_qkv, z_g, b_g, a_g, p_rw, z_r, br = jnp.split(proj, _split_points(), axis=-1)

    qkv, conv_new = _causal_dwconv(p_qkv, conv_prev.astype(dt), conv_w.astype(dt))
    qkv = jax.nn.silu(qkv.astype(f32)).reshape(B, T, 3, GDN_HEADS, GDN_DK)
    q = _l2norm(qkv[:, :, 0]) * (GDN_DK ** -0.5)
    k = _l2norm(qkv[:, :, 1])
    v = qkv[:, :, 2]
    beta = jax.nn.sigmoid(b_g.astype(f32))
    g = -jnp.exp(a_log.astype(f32)) * jax.nn.softplus(a_g.astype(f32) + dt_bias.astype(f32))
    o_g, s_gdn_new = _gdn_chunked(q, k, v, beta, g, s_gdn.astype(f32))
    o_g = _rms_norm(o_g, onorm_g) * jax.nn.silu(z_g.astype(f32)).reshape(B, T, GDN_HEADS, GDN_DV)
    o_g = o_g.reshape(B, T, GDN_W).astype(dt)

    rw, shift_new = _token_shift(p_rw, shift_prev.astype(dt), mu.astype(dt))
    rw = rw.astype(f32)
    r, kr, vr, xw, xa = jnp.split(rw, [RWKV_W, 2 * RWKV_W, 3 * RWKV_W, 3 * RWKV_W + W_LORA], axis=-1)
    wl = w0.astype(f32) + jnp.tanh(xw) @ w_lora.astype(f32)
    w_log = -jax.nn.softplus(-wl) - W_DECAY_OFFSET
    decay = jnp.exp(-jnp.exp(w_log))
    a = jax.nn.sigmoid(a0.astype(f32) + xa @ a_lora.astype(f32))
    hd = lambda t: t.reshape(B, T, RWKV_HEADS, RWKV_HD)
    kk = _l2norm(hd(kr * k_k.astype(f32)))
    kt = kr * (1.0 + (a - 1.0) * k_a.astype(f32))
    r, kt, vr, decay, a = hd(r), hd(kt), hd(vr), hd(decay), hd(a)
    y, s_rwkv_new = _rwkv7_scan(r, decay, kt, vr, kk, a, s_rwkv.astype(f32))
    y = _group_norm(y, ln_w, ln_b) + jnp.sum(r * kt * r_k.astype(f32), axis=-1, keepdims=True) * vr
    o_r = (y.reshape(B, T, RWKV_W) * jax.nn.silu(z_r.astype(f32))).astype(dt)

    gates = jax.nn.sigmoid(br.astype(f32)).astype(dt)
    g_gdn, g_rwkv = jnp.split(gates, 2, axis=-1)
    m = g_gdn * (o_g @ w_o_gdn) + g_rwkv * (o_r @ w_o_rwkv)
    out = m @ w_out
    x = (x.astype(f32) + gate * out.astype(f32)).astype(dt)
    return x, conv_new, s_gdn_new, shift_new, s_rwkv_new


def setup_inputs(seed: int = 0) -> dict:
    key = jax.random.key(seed)
    ks = iter(jax.random.split(key, 40))
    f32 = jnp.float32
    nrm = lambda shape, s: jax.random.normal(next(ks), shape, f32) * s
    d_in = sum(_in_widths())
    dt = jnp.exp(jax.random.uniform(next(ks), (DEPTH, GDN_HEADS), f32, math.log(1e-3), math.log(1e-1)))
    return {
        "x_prompt": nrm((BATCH, SEQ, D_MODEL), 1.0),
        "x_sample": nrm((DEC_BATCH, DEC_SEQ, D_MODEL), 1.0),
        "c_prompt": nrm((BATCH, D_MODEL), 1.0),
        "c_sample": nrm((DEC_BATCH, D_MODEL), 1.0),
        "cache_gdn_conv": nrm((DEPTH, DEC_BATCH, CONV_W - 1, GDN_QKV), 1.0),
        "state_gdn": nrm((DEPTH, DEC_BATCH, GDN_HEADS, GDN_DK, GDN_DV), 0.3),
        "cache_rwkv_shift": nrm((DEPTH, DEC_BATCH, 1, RWKV_SHIFT), 1.0),
        "state_rwkv": nrm((DEPTH, DEC_BATCH, RWKV_HEADS, RWKV_HD, RWKV_HD), 0.3),
        "ada_w": nrm((DEPTH, D_MODEL, 3 * D_MODEL), 0.5 * D_MODEL ** -0.5),
        "ada_b": nrm((DEPTH, 3 * D_MODEL), 0.01),
        "norm_g": 1.0 + nrm((DEPTH, D_MODEL), 0.01),
        "w_in": nrm((DEPTH, D_MODEL, d_in), D_MODEL ** -0.5),
        "gdn_conv_w": nrm((DEPTH, CONV_W, GDN_QKV), CONV_W ** -0.5),
        "gdn_a_log": jnp.log(jax.random.uniform(next(ks), (DEPTH, GDN_HEADS), f32, 1.0, 16.0)),
        "gdn_dt_bias": dt + jnp.log(-jnp.expm1(-dt)),
        "gdn_out_norm_g": 1.0 + nrm((DEPTH, GDN_DV), 0.01),
        "rwkv_mu": jax.random.uniform(next(ks), (DEPTH, RWKV_SHIFT), f32, 0.0, 1.0),
        "rwkv_w0": jax.random.uniform(next(ks), (DEPTH, RWKV_W), f32, -6.0, -1.0),
        "rwkv_w_lora": nrm((DEPTH, W_LORA, RWKV_W), 0.5 * W_LORA ** -0.5),
        "rwkv_a0": nrm((DEPTH, RWKV_W), 0.1),
        "rwkv_a_lora": nrm((DEPTH, A_LORA, RWKV_W), A_LORA ** -0.5),
        "rwkv_k_k": 0.85 + nrm((DEPTH, RWKV_W), 0.05),
        "rwkv_k_a": 1.0 + nrm((DEPTH, RWKV_W), 0.01),
        "rwkv_r_k": nrm((DEPTH, RWKV_HEADS, RWKV_HD), 0.1),
        "rwkv_ln_w": 1.0 + nrm((DEPTH, RWKV_W), 0.01),
        "rwkv_ln_b": nrm((DEPTH, RWKV_W), 0.01),
        "w_o_gdn": nrm((DEPTH, GDN_W, D_MODEL), GDN_W ** -0.5),
        "w_o_rwkv": nrm((DEPTH, RWKV_W, D_MODEL), RWKV_W ** -0.5),
        "w_out": nrm((DEPTH, D_MODEL, D_MODEL), D_MODEL ** -0.5),
        "final_norm_g": 1.0 + nrm((D_MODEL,), 0.01),
    }


def reference(x_prompt, x_sample, c_prompt, c_sample, cache_gdn_conv, state_gdn, cache_rwkv_shift, state_rwkv,
              ada_w, ada_b, norm_g, w_in, gdn_conv_w, gdn_a_log, gdn_dt_bias, gdn_out_norm_g,
              rwkv_mu, rwkv_w0, rwkv_w_lora, rwkv_a0, rwkv_a_lora, rwkv_k_k, rwkv_k_a, rwkv_r_k,
              rwkv_ln_w, rwkv_ln_b, w_o_gdn, w_o_rwkv, w_out, final_norm_g):
    Bp = x_prompt.shape[0]
    xp, xs = x_prompt, x_sample
    p_conv, p_gdn, p_shift, p_rwkv = [], [], [], []
    s_conv, s_gdn, s_shift, s_rwkv = [], [], [], []
    for l in range(DEPTH):
        prm = (ada_w[l], ada_b[l], norm_g[l], w_in[l], gdn_conv_w[l], gdn_a_log[l], gdn_dt_bias[l],
               gdn_out_norm_g[l], rwkv_mu[l], rwkv_w0[l], rwkv_w_lora[l], rwkv_a0[l], rwkv_a_lora[l],
               rwkv_k_k[l], rwkv_k_a[l], rwkv_r_k[l], rwkv_ln_w[l], rwkv_ln_b[l],
               w_o_gdn[l], w_o_rwkv[l], w_out[l])
        xp, c1, g1, sh1, r1 = _layer(
            xp, c_prompt,
            jnp.zeros((Bp, CONV_W - 1, GDN_QKV), x_prompt.dtype),
            jnp.zeros((Bp, GDN_HEADS, GDN_DK, GDN_DV), jnp.float32),
            jnp.zeros((Bp, 1, RWKV_SHIFT), x_prompt.dtype),
            jnp.zeros((Bp, RWKV_HEADS, RWKV_HD, RWKV_HD), jnp.float32),
            *prm)
        xs, c2, g2, sh2, r2 = _layer(
            xs, c_sample, cache_gdn_conv[l], state_gdn[l], cache_rwkv_shift[l], state_rwkv[l], *prm)
        p_conv.append(c1.astype(cache_gdn_conv.dtype)); p_gdn.append(g1.astype(state_gdn.dtype))
        p_shift.append(sh1.astype(cache_rwkv_shift.dtype)); p_rwkv.append(r1.astype(state_rwkv.dtype))
        s_conv.append(c2.astype(cache_gdn_conv.dtype)); s_gdn.append(g2.astype(state_gdn.dtype))
        s_shift.append(sh2.astype(cache_rwkv_shift.dtype)); s_rwkv.append(r2.astype(state_rwkv.dtype))
    y_prompt = _rms_norm(xp, final_norm_g).astype(x_prompt.dtype)
    y_sample = _rms_norm(xs, final_norm_g).astype(x_sample.dtype)
    return (y_prompt, y_sample,
            jnp.stack(p_conv), jnp.stack(p_gdn), jnp.stack(p_shift), jnp.stack(p_rwkv),
            jnp.stack(s_conv), jnp.stack(s_gdn), jnp.stack(s_shift), jnp.stack(s_rwkv))
```

```python
import functools
import math

import jax
import jax.numpy as jnp
from jax import lax
from jax.experimental import pallas as pl
from jax.experimental.pallas import tpu as pltpu

F32 = jnp.float32
BF16 = jnp.bfloat16

D_MODEL = 2048
GDN_HEADS = 8
GDN_DK = 128
GDN_DV = 128
GDN_W = GDN_HEADS * GDN_DK
GDN_CHUNK = 64
CONV_W = 4
RWKV_HEADS = 16
RWKV_HD = 64
RWKV_W = RWKV_HEADS * RWKV_HD
W_LORA = 96
A_LORA = 96
EPS = 1e-6
GN_EPS = 64e-5
W_DECAY_OFFSET = 0.5

LANE = 128
SUBLANE = 8
SEG = 1024
COL_Q, COL_K, COL_V, COL_ZG, COL_R, COL_RK, COL_RV, COL_ZR = 0, 1, 2, 3, 4, 5, 6, 7
COL_GATES = 8 * SEG
COL_BA = 12 * SEG
COL_XW = COL_BA + LANE
COL_XA = COL_XW + LANE
N_PROJ = 13 * SEG
GROUP = 256
VMEM_LIMIT = 56 * 1024 * 1024

HIGHEST = lax.Precision.HIGHEST


def _mm(a, b, precision=None):
    return jnp.dot(a, b, preferred_element_type=F32, precision=precision)


def _mm_nt(a, b, precision=None):
    return lax.dot_general(a, b, (((1,), (1,)), ((), ())), preferred_element_type=F32, precision=precision)


def _mm_tn(a, b, precision=None):
    return lax.dot_general(a, b, (((0,), (0,)), ((), ())), preferred_element_type=F32, precision=precision)


def _sigmoid(x):
    return 1.0 / (1.0 + jnp.exp(-x))


def _silu(x):
    return x * _sigmoid(x)


def _softplus(x):
    return jnp.maximum(x, 0.0) + jnp.log1p(jnp.exp(-jnp.abs(x)))


def _mod_kernel(c_ref, w_ref, b_ref, o_ref):
    o_ref[...] = _mm(c_ref[...], w_ref[...]) + b_ref[...]


def _ada_mod(c_all, ada_w, ada_b):
    depth, d, n3 = ada_w.shape
    rows = c_all.shape[0]
    tn = 512
    return pl.pallas_call(
        _mod_kernel,
        grid=(depth, n3 // tn),
        in_specs=[
            pl.BlockSpec((rows, d), lambda l, j: (0, 0)),
            pl.BlockSpec((None, d, tn), lambda l, j: (l, 0, j)),
            pl.BlockSpec((None, 1, tn), lambda l, j: (l, 0, j)),
        ],
        out_specs=pl.BlockSpec((None, rows, tn), lambda l, j: (l, 0, j)),
        out_shape=jax.ShapeDtypeStruct((depth, rows, n3), F32),
        compiler_params=pltpu.CompilerParams(
            dimension_semantics=("arbitrary", "arbitrary"), vmem_limit_bytes=VMEM_LIMIT),
        name="ada_mod",
    )(c_all, ada_w, ada_b.reshape(depth, 1, n3))


def _in_proj_kernel(x_ref, sc_ref, sh_ref, g_ref, w_ref, o_ref, h_ref):
    @pl.when(pl.program_id(1) == 0)
    def _():
        x = x_ref[...]
        ms = jnp.mean(x * x, axis=-1, keepdims=True)
        h = x * lax.rsqrt(ms + EPS) * g_ref[...] * (1.0 + sc_ref[...]) + sh_ref[...]
        h_ref[...] = h.astype(BF16)

    o_ref[...] = _mm(h_ref[...], w_ref[...])


def _in_proj(x2d, scale, shift, norm_g, w_pad, rows_per_mod):
    m, d = x2d.shape
    tm = min(m, 512) if rows_per_mod == 1 else min(rows_per_mod, 1024)
    tn = SEG
    if rows_per_mod == 1:
        mod_spec = pl.BlockSpec((tm, d), lambda i, j: (i, 0))
    else:
        assert rows_per_mod % tm == 0
        per = rows_per_mod // tm
        scale = scale.reshape(-1, 1, d)
        shift = shift.reshape(-1, 1, d)
        mod_spec = pl.BlockSpec((None, 1, d), lambda i, j: (i // per, 0, 0))
    return pl.pallas_call(
        _in_proj_kernel,
        grid=(m // tm, N_PROJ // tn),
        in_specs=[
            pl.BlockSpec((tm, d), lambda i, j: (i, 0)),
            mod_spec, mod_spec,
            pl.BlockSpec((1, d), lambda i, j: (0, 0)),
            pl.BlockSpec((d, tn), lambda i, j: (0, j)),
        ],
        out_specs=pl.BlockSpec((tm, tn), lambda i, j: (i, j)),
        out_shape=jax.ShapeDtypeStruct((m, N_PROJ), F32),
        scratch_shapes=[pltpu.VMEM((tm, d), BF16)],
        compiler_params=pltpu.CompilerParams(
            dimension_semantics=("arbitrary", "arbitrary"), vmem_limit_bytes=VMEM_LIMIT),
        name="in_proj",
    )(x2d, scale, shift, norm_g.reshape(1, d), w_pad)


def _gdn_kernel(q_ref, k_ref, v_ref, z_ref, ba_ref, cw_ref, prev_ref, s0_ref, alog_ref, dtb_ref, on_ref,
                o_ref, sfin_ref, ubuf, s_scr, *, chunk, nsteps):
    L = chunk
    i = pl.program_id(1)

    @pl.when(i == 0)
    def _():
        ubuf[0:SUBLANE, :] = prev_ref[...]
        s_scr[...] = s0_ref[...]

    ubuf[SUBLANE:SUBLANE + L, 0:GDN_W] = q_ref[...]
    ubuf[SUBLANE:SUBLANE + L, GDN_W:2 * GDN_W] = k_ref[...]
    ubuf[SUBLANE:SUBLANE + L, 2 * GDN_W:3 * GDN_W] = v_ref[...]
    base = SUBLANE - (CONV_W - 1)
    y = ubuf[base:base + L, :] * cw_ref[0:1, :]
    for j in range(1, CONV_W):
        y = y + ubuf[base + j:base + j + L, :] * cw_ref[j:j + 1, :]
    tail = ubuf[L:L + SUBLANE, :]
    ubuf[0:SUBLANE, :] = tail
    act = _silu(y)

    ba = ba_ref[...]
    beta_all = _sigmoid(ba)
    g_all = -jnp.exp(alog_ref[...]) * _softplus(ba + dtb_ref[...])
    ri = lax.broadcasted_iota(jnp.int32, (L, L), 0)
    ci = lax.broadcasted_iota(jnp.int32, (L, L), 1)
    causal = ri >= ci
    strict = ri > ci
    eye = (ri == ci).astype(F32)
    gc_cols = _mm(causal.astype(F32), g_all, HIGHEST)
    gc_rows = _mm_tn(g_all, (ri <= ci).astype(F32), HIGHEST)

    n_sq = int(math.log2(L)) - 1
    for h in range(GDN_HEADS):
        sl = slice(h * GDN_DK, (h + 1) * GDN_DK)
        qh = act[:, sl]
        kh = act[:, GDN_W + h * GDN_DK:GDN_W + (h + 1) * GDN_DK]
        vh = act[:, 2 * GDN_W + h * GDN_DV:2 * GDN_W + (h + 1) * GDN_DV]
        qh = qh * lax.rsqrt(jnp.sum(qh * qh, axis=-1, keepdims=True) + EPS) * (GDN_DK ** -0.5)
        kh = kh * lax.rsqrt(jnp.sum(kh * kh, axis=-1, keepdims=True) + EPS)
        beta = beta_all[:, h:h + 1]
        gcol = gc_cols[:, GDN_HEADS + h:GDN_HEADS + h + 1]
        grow = gc_rows[GDN_HEADS + h:GDN_HEADS + h + 1, :]
        glast = gc_cols[L - 1:L, GDN_HEADS + h:GDN_HEADS + h + 1]
        diff = gcol - grow
        dm = jnp.where(causal, jnp.exp(jnp.where(causal, diff, 0.0)), 0.0)
        kk = _mm_nt(kh, kh)
        qk = _mm_nt(qh, kh)
        a_mat = jnp.where(strict, beta * kk * dm, 0.0)
        t_inv = eye - a_mat
        pw = a_mat
        for _ in range(n_sq):
            pw = _mm(pw, pw, HIGHEST)
            t_inv = t_inv + _mm(t_inv, pw, HIGHEST)
        eg = jnp.exp(gcol)
        rhs = jnp.concatenate([beta * vh, (beta * eg) * kh], axis=-1)
        sol = _mm(t_inv, rhs)
        uv = sol[:, :GDN_DV]
        wk = sol[:, GDN_DV:]
        s_h = s_scr[h]
        u = uv - _mm(wk, s_h)
        o = _mm(qh * eg, s_h) + _mm(qk * dm, u)
        kd = kh * jnp.exp(glast - gcol)
        s_scr[h] = s_h * jnp.exp(glast) + _mm_tn(kd, u)
        on = o * lax.rsqrt(jnp.mean(o * o, axis=-1, keepdims=True) + EPS) * on_ref[...]
        o_ref[:, sl] = (on * _silu(z_ref[:, sl])).astype(o_ref.dtype)

    @pl.when(i == nsteps - 1)
    def _():
        sfin_ref[...] = s_scr[...]


def _gdn(proj, conv_prev, s0, conv_w, a_log, dt_bias, onorm_g, batch, seq):
    L = min(GDN_CHUNK, seq)
    assert seq % L == 0 and L % SUBLANE == 0
    nc = seq // L
    prev_pad = jnp.pad(conv_prev, ((0, 0), (SUBLANE - (CONV_W - 1), 0), (0, 0)))
    lane_pad = (GDN_HEADS, LANE - 2 * GDN_HEADS)
    alog_row = jnp.pad(a_log, lane_pad).reshape(1, LANE)
    dtb_row = jnp.pad(dt_bias, lane_pad).reshape(1, LANE)
    row = lambda c: (lambda b, i: (b * nc + i, c))
    kern = functools.partial(_gdn_kernel, chunk=L, nsteps=nc)
    return pl.pallas_call(
        kern,
        grid=(batch, nc),
        in_specs=[
            pl.BlockSpec((L, SEG), row(COL_Q)),
            pl.BlockSpec((L, SEG), row(COL_K)),
            pl.BlockSpec((L, SEG), row(COL_V)),
            pl.BlockSpec((L, SEG), row(COL_ZG)),
            pl.BlockSpec((L, LANE), row(COL_BA // LANE)),
            pl.BlockSpec((CONV_W, 3 * GDN_W), lambda b, i: (0, 0)),
            pl.BlockSpec((None, SUBLANE, 3 * GDN_W), lambda b, i: (b, 0, 0)),
            pl.BlockSpec((None, GDN_HEADS, GDN_DK, GDN_DV), lambda b, i: (b, 0, 0, 0)),
            pl.BlockSpec((1, LANE), lambda b, i: (0, 0)),
            pl.BlockSpec((1, LANE), lambda b, i: (0, 0)),
            pl.BlockSpec((1, GDN_DV), lambda b, i: (0, 0)),
        ],
        out_specs=[
            pl.BlockSpec((L, GDN_W), lambda b, i: (b * nc + i, 0)),
            pl.BlockSpec((None, GDN_HEADS, GDN_DK, GDN_DV), lambda b, i: (b, 0, 0, 0)),
        ],
        out_shape=[
            jax.ShapeDtypeStruct((batch * seq, GDN_W), BF16),
            jax.ShapeDtypeStruct((batch, GDN_HEADS, GDN_DK, GDN_DV), F32),
        ],
        scratch_shapes=[
            pltpu.VMEM((L + SUBLANE, 3 * GDN_W), F32),
            pltpu.VMEM((GDN_HEADS, GDN_DK, GDN_DV), F32),
        ],
        compiler_params=pltpu.CompilerParams(
            dimension_semantics=("arbitrary", "arbitrary"), vmem_limit_bytes=VMEM_LIMIT),
        name="gdn_scan",
    )(proj, proj, proj, proj, proj, conv_w, prev_pad, s0, alog_row, dtb_row, onorm_g.reshape(1, GDN_DV))


def _group_rows(x):
    return jnp.concatenate([x[:, g * GROUP:(g + 1) * GROUP] for g in range(RWKV_W // GROUP)], axis=0)


def _ungroup_rows(x, off=0):
    return jnp.concatenate(
        [x[off + g * RWKV_HD:off + (g + 1) * RWKV_HD, :] for g in range(RWKV_W // GROUP)], axis=1)


def _head_sum(x, gmat, precision=None):
    return jnp.concatenate(
        [_mm(x[:, g * GROUP:(g + 1) * GROUP], gmat, precision) for g in range(RWKV_W // GROUP)], axis=1)


def _rwkv_kernel(r_ref, k_ref, v_ref, xw_ref, xa_ref, z_ref, pr_ref, pk_ref, pv_ref, pxw_ref, pxa_ref, s0_ref,
                 mu_ref, muw_ref, mua_ref, w0_ref, wl_ref, a0_ref, al_ref, kk_ref, ka_ref, rk_ref,
                 lnw_ref, lnb_ref,
                 o_ref, sfin_ref,
                 sbuf, lbuf, st, gmat_s, imask_s, kk_s, w_s, ah_s, kt_s, r_s, v_s, y_s, *, nb, tt, nsteps):
    i = pl.program_id(1)
    T = tt
    ri = lax.broadcasted_iota(jnp.int32, (GROUP, GROUP), 0)
    ci = lax.broadcasted_iota(jnp.int32, (GROUP, GROUP), 1)
    hd_bits = RWKV_HD.bit_length() - 1
    gmat_s[...] = ((ri >> hd_bits) == (ci >> hd_bits)).astype(F32)
    rv = lax.broadcasted_iota(jnp.int32, (RWKV_HD, RWKV_W), 0)
    lk = lax.broadcasted_iota(jnp.int32, (RWKV_HD, RWKV_W), 1)
    imask_s[...] = ((lk & (RWKV_HD - 1)) == rv).astype(F32)

    @pl.when(i == 0)
    def _():
        st[...] = s0_ref[...]
        for n in range(nb):
            sbuf[n, SUBLANE - 1:SUBLANE, 0:SEG] = pr_ref[n]
            sbuf[n, SUBLANE - 1:SUBLANE, SEG:2 * SEG] = pk_ref[n]
            sbuf[n, SUBLANE - 1:SUBLANE, 2 * SEG:3 * SEG] = pv_ref[n]
            lbuf[n, SUBLANE - 1:SUBLANE, 0:LANE] = pxw_ref[n]
            lbuf[n, SUBLANE - 1:SUBLANE, LANE:2 * LANE] = pxa_ref[n]

    gmat = gmat_s[...]
    inv_hd = 1.0 / RWKV_HD
    for n in range(nb):
        sbuf[n, SUBLANE:SUBLANE + T, 0:SEG] = r_ref[n]
        sbuf[n, SUBLANE:SUBLANE + T, SEG:2 * SEG] = k_ref[n]
        sbuf[n, SUBLANE:SUBLANE + T, 2 * SEG:3 * SEG] = v_ref[n]
        lbuf[n, SUBLANE:SUBLANE + T, 0:LANE] = xw_ref[n]
        lbuf[n, SUBLANE:SUBLANE + T, LANE:2 * LANE] = xa_ref[n]
        cur = sbuf[n, SUBLANE:SUBLANE + T, :]
        prv = sbuf[n, SUBLANE - 1:SUBLANE - 1 + T, :]
        rkv = cur + (prv - cur) * mu_ref[...]
        curl = lbuf[n, SUBLANE:SUBLANE + T, :]
        prvl = lbuf[n, SUBLANE - 1:SUBLANE - 1 + T, :]
        xw = curl[:, 0:LANE] + (prvl[:, 0:LANE] - curl[:, 0:LANE]) * muw_ref[...]
        xa = curl[:, LANE:] + (prvl[:, LANE:] - curl[:, LANE:]) * mua_ref[...]
        sbuf[n, 0:SUBLANE, :] = sbuf[n, T:T + SUBLANE, :]
        lbuf[n, 0:SUBLANE, :] = lbuf[n, T:T + SUBLANE, :]
        r = rkv[:, 0:SEG]
        kr = rkv[:, SEG:2 * SEG]
        vr = rkv[:, 2 * SEG:3 * SEG]
        wl = w0_ref[...] + _mm(jnp.tanh(xw), wl_ref[...])
        decay = jnp.exp(-math.exp(-W_DECAY_OFFSET) * _sigmoid(wl))
        a = _sigmoid(a0_ref[...] + _mm(xa, al_ref[...]))
        kkr = kr * kk_ref[...]
        kk = kkr * lax.rsqrt(_head_sum(kkr * kkr, gmat, HIGHEST) + EPS)
        kt = kr * (1.0 + (a - 1.0) * ka_ref[...])
        kk_s[n] = kk
        w_s[n] = decay
        ah_s[n] = a * kk
        kt_s[n] = kt
        r_s[n] = r
        v_s[n] = vr

    def tok(t, carry):
        imask = imask_s[...]
        g = gmat_s[...]
        for n in range(nb):
            s = st[n]
            kk_row = kk_s[n, pl.ds(t, 1), :]
            v_row = v_s[n, pl.ds(t, 1), :]
            lhs = jnp.concatenate([_group_rows(s * kk_row), _group_rows(v_row * imask)], axis=0)
            res = _mm(lhs, g)
            sk_b = _ungroup_rows(res, 0)
            v_b = _ungroup_rows(res, 4 * RWKV_HD)
            s_new = s * w_s[n, pl.ds(t, 1), :] - sk_b * ah_s[n, pl.ds(t, 1), :] + v_b * kt_s[n, pl.ds(t, 1), :]
            st[n] = s_new
            y_b = _ungroup_rows(_mm(_group_rows(s_new * r_s[n, pl.ds(t, 1), :]), g), 0)
            y_s[n, pl.ds(t, 1), :] = jnp.sum(y_b * imask, axis=0, keepdims=True)
        return carry

    lax.fori_loop(0, T, tok, 0)

    for n in range(nb):
        y = y_s[n]
        mean = _head_sum(y, gmat, HIGHEST) * inv_hd
        dlt = y - mean
        var = _head_sum(dlt * dlt, gmat, HIGHEST) * inv_hd
        yn = dlt * lax.rsqrt(var + GN_EPS) * lnw_ref[...] + lnb_ref[...]
        bonus = _head_sum(r_s[n] * kt_s[n] * rk_ref[...], gmat, HIGHEST)
        o_ref[n] = ((yn + bonus * v_s[n]) * _silu(z_ref[n])).astype(o_ref.dtype)

    @pl.when(i == nsteps - 1)
    def _():
        sfin_ref[...] = st[...]


def _rwkv(proj3, shift_prev, s0, mu, w0, w_lora, a0, a_lora, k_k, k_a, r_k, ln_w, ln_b, nb, tt):
    b, t, _ = proj3.shape
    assert b % nb == 0 and t % tt == 0 and tt % SUBLANE == 0
    nsteps = t // tt
    w3 = 3 * RWKV_W
    padl = lambda x, n: jnp.pad(x, [(0, 0)] * (x.ndim - 1) + [(0, LANE - n)])
    p_r, p_k, p_v = shift_prev[..., 0:RWKV_W], shift_prev[..., RWKV_W:2 * RWKV_W], shift_prev[..., 2 * RWKV_W:w3]
    p_xw = padl(shift_prev[..., w3:w3 + W_LORA], W_LORA)
    p_xa = padl(shift_prev[..., w3 + W_LORA:], A_LORA)
    mu_rkv = mu[0:w3].reshape(1, w3)
    mu_w = padl(mu[w3:w3 + W_LORA], W_LORA).reshape(1, LANE)
    mu_a = padl(mu[w3 + W_LORA:], A_LORA).reshape(1, LANE)
    wl_pad = jnp.pad(w_lora, ((0, LANE - W_LORA), (0, 0)))
    al_pad = jnp.pad(a_lora, ((0, LANE - A_LORA), (0, 0)))
    s0t = s0.transpose(0, 3, 1, 2).reshape(b, RWKV_HD, RWKV_W)
    row1 = lambda x: x.reshape(1, RWKV_W)
    blk = lambda c, w: pl.BlockSpec((nb, tt, w), lambda g, i, c=c: (g, i, c))
    prevspec = lambda w: pl.BlockSpec((nb, 1, w), lambda g, i: (g, 0, 0))
    const = lambda shape: pl.BlockSpec(shape, lambda g, i: (0,) * len(shape))
    kern = functools.partial(_rwkv_kernel, nb=nb, tt=tt, nsteps=nsteps)
    rows = lambda: pltpu.VMEM((nb, tt, RWKV_W), F32)
    o, sfin = pl.pallas_call(
        kern,
        grid=(b // nb, nsteps),
        in_specs=[
            blk(COL_R, SEG), blk(COL_RK, SEG), blk(COL_RV, SEG),
            blk(COL_XW // LANE, LANE), blk(COL_XA // LANE, LANE), blk(COL_ZR, SEG),
            prevspec(SEG), prevspec(SEG), prevspec(SEG), prevspec(LANE), prevspec(LANE),
            pl.BlockSpec((nb, RWKV_HD, RWKV_W), lambda g, i: (g, 0, 0)),
            const((1, w3)), const((1, LANE)), const((1, LANE)),
            const((1, RWKV_W)), const((LANE, RWKV_W)), const((1, RWKV_W)), const((LANE, RWKV_W)),
            const((1, RWKV_W)), const((1, RWKV_W)), const((1, RWKV_W)), const((1, RWKV_W)), const((1, RWKV_W)),
        ],
        out_specs=[
            pl.BlockSpec((nb, tt, RWKV_W), lambda g, i: (g, i, 0)),
            pl.BlockSpec((nb, RWKV_HD, RWKV_W), lambda g, i: (g, 0, 0)),
        ],
        out_shape=[
            jax.ShapeDtypeStruct((b, t, RWKV_W), BF16),
            jax.ShapeDtypeStruct((b, RWKV_HD, RWKV_W), F32),
        ],
        scratch_shapes=[
            pltpu.VMEM((nb, tt + SUBLANE, w3), F32),
            pltpu.VMEM((nb, tt + SUBLANE, 2 * LANE), F32),
            pltpu.VMEM((nb, RWKV_HD, RWKV_W), F32),
            pltpu.VMEM((GROUP, GROUP), F32),
            pltpu.VMEM((RWKV_HD, RWKV_W), F32),
            rows(), rows(), rows(), rows(), rows(), rows(), rows(),
        ],
        compiler_params=pltpu.CompilerParams(
            dimension_semantics=("arbitrary", "arbitrary"), vmem_limit_bytes=VMEM_LIMIT),
        name="rwkv_scan",
    )(proj3, proj3, proj3, proj3, proj3, proj3, p_r, p_k, p_v, p_xw, p_xa, s0t,
      mu_rkv, mu_w, mu_a, row1(w0), wl_pad, row1(a0), al_pad, row1(k_k), row1(k_a), row1(r_k),
      row1(ln_w), row1(ln_b))
    s_new = sfin.reshape(b, RWKV_HD, RWKV_HEADS, RWKV_HD).transpose(0, 2, 3, 1)
    return o.reshape(b * t, RWKV_W), s_new


def _merge_kernel(og_ref, or_ref, gg_ref, gr_ref, x_ref, gate_ref, wog_ref, wor_ref, wout_ref, fg_ref, o_ref,
                  *, final_norm):
    m = _sigmoid(gg_ref[...]) * _mm(og_ref[...], wog_ref[...]) \
        + _sigmoid(gr_ref[...]) * _mm(or_ref[...], wor_ref[...])
    out = _mm(m.astype(BF16), wout_ref[...])
    xn = x_ref[...] + gate_ref[...] * out
    if final_norm:
        xn = xn * lax.rsqrt(jnp.mean(xn * xn, axis=-1, keepdims=True) + EPS) * fg_ref[...]
    o_ref[...] = xn


def _merge(o_g, o_r, proj, x2d, gate, w_og, w_or, w_out, final_g, rows_per_mod, final_norm):
    m, d = x2d.shape
    tm = min(m, 256) if rows_per_mod == 1 else min(rows_per_mod, 256)
    if rows_per_mod == 1:
        gate_spec = pl.BlockSpec((tm, d), lambda i: (i, 0))
    else:
        assert rows_per_mod % tm == 0
        per = rows_per_mod // tm
        gate = gate.reshape(-1, 1, d)
        gate_spec = pl.BlockSpec((None, 1, d), lambda i: (i // per, 0, 0))
    whole = lambda shape: pl.BlockSpec(shape, lambda i: (0, 0))
    gcol = COL_GATES // d
    return pl.pallas_call(
        functools.partial(_merge_kernel, final_norm=final_norm),
        grid=(m // tm,),
        in_specs=[
            pl.BlockSpec((tm, GDN_W), lambda i: (i, 0)),
            pl.BlockSpec((tm, RWKV_W), lambda i: (i, 0)),
            pl.BlockSpec((tm, d), lambda i: (i, gcol)),
            pl.BlockSpec((tm, d), lambda i: (i, gcol + 1)),
            pl.BlockSpec((tm, d), lambda i: (i, 0)),
            gate_spec,
            whole((GDN_W, d)), whole((RWKV_W, d)), whole((d, d)), whole((1, d)),
        ],
        out_specs=pl.BlockSpec((tm, d), lambda i: (i, 0)),
        out_shape=jax.ShapeDtypeStruct((m, d), F32),
        compiler_params=pltpu.CompilerParams(
            dimension_semantics=("arbitrary",), vmem_limit_bytes=VMEM_LIMIT),
        name="merge_out",
    )(o_g, o_r, proj, proj, x2d, gate, w_og, w_or, w_out, final_g.reshape(1, d))


def _pad_in_weight(w_in):
    d = w_in.shape[0]
    o_zg = 3 * GDN_W
    o_b = o_zg + GDN_W
    o_rw = o_b + 2 * GDN_HEADS
    o_xw = o_rw + 3 * RWKV_W
    o_xa = o_xw + W_LORA
    o_zr = o_xa + A_LORA
    o_br = o_zr + RWKV_W
    z = lambda n: jnp.zeros((d, n), w_in.dtype)
    cols = [
        w_in[:, 0:o_zg], w_in[:, o_zg:o_b],
        w_in[:, o_rw:o_xw], w_in[:, o_zr:o_br],
        w_in[:, o_br:o_br + 2 * D_MODEL],
        w_in[:, o_b:o_rw], z(LANE - 2 * GDN_HEADS),
        w_in[:, o_xw:o_xa], z(LANE - W_LORA),
        w_in[:, o_xa:o_zr], z(LANE - A_LORA),
    ]
    used = COL_XA + LANE
    cols.append(z(N_PROJ - used))
    return jnp.concatenate(cols, axis=1).astype(BF16)


def _layer(x2d, batch, seq, scale, shift, gate, rows_per_mod, conv_prev, s_gdn, shift_prev, s_rwkv, p, final_g,
           final_norm, rwkv_nb, rwkv_tt):
    proj = _in_proj(x2d, scale, shift, p["norm_g"], p["w_in_pad"], rows_per_mod)
    o_g, s_gdn_new = _gdn(proj, conv_prev, s_gdn, p["conv_w"], p["a_log"], p["dt_bias"], p["onorm_g"], batch, seq)
    proj3 = proj.reshape(batch, seq, N_PROJ)
    o_r, s_rwkv_new = _rwkv(proj3, shift_prev, s_rwkv, p["mu"], p["w0"], p["w_lora"], p["a0"], p["a_lora"],
                            p["k_k"], p["k_a"], p["r_k"], p["ln_w"], p["ln_b"], rwkv_nb, rwkv_tt)
    x_new = _merge(o_g, o_r, proj, x2d, gate, p["w_o_gdn"], p["w_o_rwkv"], p["w_out"], final_g, rows_per_mod,
                   final_norm)
    conv_new = proj3[:, seq - (CONV_W - 1):, 0:3 * GDN_W]
    last = proj3[:, seq - 1:, :]
    shift_new = jnp.concatenate(
        [last[..., COL_R * SEG:COL_R * SEG + 3 * RWKV_W], last[..., COL_XW:COL_XW + W_LORA],
         last[..., COL_XA:COL_XA + A_LORA]], axis=-1)
    return x_new, conv_new, s_gdn_new, shift_new, s_rwkv_new


def _forward(x_prompt, x_sample, c_prompt, c_sample, cache_gdn_conv, state_gdn, cache_rwkv_shift, state_rwkv,
             ada_w, ada_b, norm_g, w_in, gdn_conv_w, gdn_a_log, gdn_dt_bias, gdn_out_norm_g,
             rwkv_mu, rwkv_w0, rwkv_w_lora, rwkv_a0, rwkv_a_lora, rwkv_k_k, rwkv_k_a, rwkv_r_k,
             rwkv_ln_w, rwkv_ln_b, w_o_gdn, w_o_rwkv, w_out, final_norm_g):
    depth = ada_w.shape[0]
    bp, tp, d = x_prompt.shape
    bs, ts, _ = x_sample.shape
    assert CONV_W - 1 <= min(tp, ts)
    c_all = jnp.concatenate([c_prompt, c_sample], axis=0)
    rows = -(-(bp + bs) // SUBLANE) * SUBLANE
    c_all = jnp.pad(c_all, ((0, rows - (bp + bs)), (0, 0)))
    mod = _ada_mod(c_all, ada_w, ada_b)
    xp = x_prompt.reshape(bp * tp, d)
    xs = x_sample.reshape(bs * ts, d)
    outs = [[] for _ in range(8)]
    for l in range(depth):
        p = dict(norm_g=norm_g[l], w_in_pad=_pad_in_weight(w_in[l]), conv_w=gdn_conv_w[l], a_log=gdn_a_log[l],
                 dt_bias=gdn_dt_bias[l], onorm_g=gdn_out_norm_g[l], mu=rwkv_mu[l], w0=rwkv_w0[l],
                 w_lora=rwkv_w_lora[l], a0=rwkv_a0[l], a_lora=rwkv_a_lora[l], k_k=rwkv_k_k[l], k_a=rwkv_k_a[l],
                 r_k=rwkv_r_k[l].reshape(-1), ln_w=rwkv_ln_w[l], ln_b=rwkv_ln_b[l],
                 w_o_gdn=w_o_gdn[l].astype(BF16), w_o_rwkv=w_o_rwkv[l].astype(BF16), w_out=w_out[l].astype(BF16))
        last = l == depth - 1
        m_p = mod[l, 0:bp]
        m_s = jnp.repeat(mod[l, bp:bp + bs], ts, axis=0)
        sh_p, sc_p, gt_p = m_p[:, 0:d], m_p[:, d:2 * d], m_p[:, 2 * d:]
        sh_s, sc_s, gt_s = m_s[:, 0:d], m_s[:, d:2 * d], m_s[:, 2 * d:]
        zeros = lambda *s: jnp.zeros(s, F32)
        xp, c1, g1, h1, r1 = _layer(
            xp, bp, tp, sc_p, sh_p, gt_p, tp,
            zeros(bp, CONV_W - 1, 3 * GDN_W), zeros(bp, GDN_HEADS, GDN_DK, GDN_DV),
            zeros(bp, 1, 3 * RWKV_W + W_LORA + A_LORA), zeros(bp, RWKV_HEADS, RWKV_HD, RWKV_HD),
            p, final_norm_g, last, rwkv_nb=bp, rwkv_tt=min(tp, 256))
        xs, c2, g2, h2, r2 = _layer(
            xs, bs, ts, sc_s, sh_s, gt_s, 1,
            cache_gdn_conv[l], state_gdn[l], cache_rwkv_shift[l], state_rwkv[l],
            p, final_norm_g, last, rwkv_nb=min(bs, 4), rwkv_tt=ts)
        for lst, val in zip(outs, (c1, g1, h1, r1, c2, g2, h2, r2)):
            lst.append(val)
    stk = [jnp.stack(v) for v in outs]
    return (xp.reshape(bp, tp, d), xs.reshape(bs, ts, d), *stk)


def kernel(x_prompt, x_sample, c_prompt, c_sample, cache_gdn_conv, state_gdn, cache_rwkv_shift, state_rwkv, ada_w, ada_b, norm_g, w_in, gdn_conv_w, gdn_a_log, gdn_dt_bias, gdn_out_norm_g, rwkv_mu, rwkv_w0, rwkv_w_lora, rwkv_a0, rwkv_a_lora, rwkv_k_k, rwkv_k_a, rwkv_r_k, rwkv_ln_w, rwkv_ln_b, w_o_gdn, w_o_rwkv, w_out, final_norm_g):
    return _forward(x_prompt, x_sample, c_prompt, c_sample, cache_gdn_conv, state_gdn, cache_rwkv_shift,
                    state_rwkv, ada_w, ada_b, norm_g, w_in, gdn_conv_w, gdn_a_log, gdn_dt_bias, gdn_out_norm_g,
                    rwkv_mu, rwkv_w0, rwkv_w_lora, rwkv_a0, rwkv_a_lora, rwkv_k_k, rwkv_k_a, rwkv_r_k,
                    rwkv_ln_w, rwkv_ln_b, w_o_gdn, w_o_rwkv, w_out, final_norm_g)
```

```python
import functools
import math

import jax
import jax.numpy as jnp
from jax import lax
from jax.experimental import pallas as pl
from jax.experimental.pallas import tpu as pltpu

F32 = jnp.float32
BF16 = jnp.bfloat16

D_MODEL = 2048
GDN_HEADS = 8
GDN_DK = 128
GDN_DV = 128
GDN_W = GDN_HEADS * GDN_DK
GDN_CHUNK = 64
CONV_W = 4
RWKV_HEADS = 16
RWKV_HD = 64
RWKV_W = RWKV_HEADS * RWKV_HD
RWKV_CHUNK = 64
W_LORA = 96
A_LORA = 96
EPS = 1e-6
GN_EPS = 64e-5
W_DECAY_OFFSET = 0.5

LANE = 128
SUBLANE = 8
SEG = 1024
COL_Q, COL_K, COL_V, COL_ZG, COL_R, COL_RK, COL_RV, COL_ZR = 0, 1, 2, 3, 4, 5, 6, 7
COL_GATES = 8 * SEG
COL_BA = 12 * SEG
COL_XW = COL_BA + LANE
COL_XA = COL_XW + LANE
N_PROJ = 13 * SEG
GROUP = 256
GROUP_HEADS = GROUP // RWKV_HD
N_GROUPS = RWKV_W // GROUP
VMEM_LIMIT = 56 * 1024 * 1024

HIGHEST = lax.Precision.HIGHEST


def _mm(a, b, precision=None):
    return jnp.dot(a, b, preferred_element_type=F32, precision=precision)


def _mm_nt(a, b, precision=None):
    return lax.dot_general(a, b, (((1,), (1,)), ((), ())), preferred_element_type=F32, precision=precision)


def _mm_tn(a, b, precision=None):
    return lax.dot_general(a, b, (((0,), (0,)), ((), ())), preferred_element_type=F32, precision=precision)


def _split_bf16(x):
    hi = x.astype(BF16)
    return hi, (x - hi.astype(F32)).astype(BF16)


def _sigmoid(x):
    return 1.0 / (1.0 + jnp.exp(-x))


def _silu(x):
    return x * _sigmoid(x)


def _softplus(x):
    return jnp.maximum(x, 0.0) + jnp.log1p(jnp.exp(-jnp.abs(x)))


def _mod_kernel(c_ref, w_ref, b_ref, o_ref):
    o_ref[...] = _mm(c_ref[...], w_ref[...]) + b_ref[...]


def _ada_mod(c_all, ada_w, ada_b):
    depth, d, n3 = ada_w.shape
    rows = c_all.shape[0]
    tn = 512
    return pl.pallas_call(
        _mod_kernel,
        grid=(depth, n3 // tn),
        in_specs=[
            pl.BlockSpec((rows, d), lambda l, j: (0, 0)),
            pl.BlockSpec((None, d, tn), lambda l, j: (l, 0, j)),
            pl.BlockSpec((None, 1, tn), lambda l, j: (l, 0, j)),
        ],
        out_specs=pl.BlockSpec((None, rows, tn), lambda l, j: (l, 0, j)),
        out_shape=jax.ShapeDtypeStruct((depth, rows, n3), F32),
        compiler_params=pltpu.CompilerParams(
            dimension_semantics=("arbitrary", "arbitrary"), vmem_limit_bytes=VMEM_LIMIT),
        name="ada_mod",
    )(c_all, ada_w, ada_b.reshape(depth, 1, n3))


def _in_proj_kernel(x_ref, sc_ref, sh_ref, g_ref, w_ref, o_ref, h_ref):
    @pl.when(pl.program_id(1) == 0)
    def _():
        x = x_ref[...]
        ms = jnp.mean(x * x, axis=-1, keepdims=True)
        h = x * lax.rsqrt(ms + EPS) * g_ref[...] * (1.0 + sc_ref[...]) + sh_ref[...]
        h_ref[...] = h.astype(BF16)

    o_ref[...] = _mm(h_ref[...], w_ref[...])


def _in_proj(x2d, scale, shift, norm_g, w_pad, rows_per_mod):
    m, d = x2d.shape
    tm = min(m, 512) if rows_per_mod == 1 else min(rows_per_mod, 1024)
    tn = SEG
    if rows_per_mod == 1:
        mod_spec = pl.BlockSpec((tm, d), lambda i, j: (i, 0))
    else:
        assert rows_per_mod % tm == 0
        per = rows_per_mod // tm
        scale = scale.reshape(-1, 1, d)
        shift = shift.reshape(-1, 1, d)
        mod_spec = pl.BlockSpec((None, 1, d), lambda i, j: (i // per, 0, 0))
    return pl.pallas_call(
        _in_proj_kernel,
        grid=(m // tm, N_PROJ // tn),
        in_specs=[
            pl.BlockSpec((tm, d), lambda i, j: (i, 0)),
            mod_spec, mod_spec,
            pl.BlockSpec((1, d), lambda i, j: (0, 0)),
            pl.BlockSpec((d, tn), lambda i, j: (0, j)),
        ],
        out_specs=pl.BlockSpec((tm, tn), lambda i, j: (i, j)),
        out_shape=jax.ShapeDtypeStruct((m, N_PROJ), F32),
        scratch_shapes=[pltpu.VMEM((tm, d), BF16)],
        compiler_params=pltpu.CompilerParams(
            dimension_semantics=("arbitrary", "arbitrary"), vmem_limit_bytes=VMEM_LIMIT),
        name="in_proj",
    )(x2d, scale, shift, norm_g.reshape(1, d), w_pad)


def _gdn_kernel(q_ref, k_ref, v_ref, z_ref, ba_ref, cw_ref, prev_ref, s0_ref, alog_ref, dtb_ref, on_ref,
                o_ref, sfin_ref, ubuf, s_scr, *, chunk, nsteps):
    L = chunk
    i = pl.program_id(1)

    @pl.when(i == 0)
    def _():
        ubuf[0:SUBLANE, :] = prev_ref[...]
        s_scr[...] = s0_ref[...]

    ubuf[SUBLANE:SUBLANE + L, 0:GDN_W] = q_ref[...]
    ubuf[SUBLANE:SUBLANE + L, GDN_W:2 * GDN_W] = k_ref[...]
    ubuf[SUBLANE:SUBLANE + L, 2 * GDN_W:3 * GDN_W] = v_ref[...]
    base = SUBLANE - (CONV_W - 1)
    y = ubuf[base:base + L, :] * cw_ref[0:1, :]
    for j in range(1, CONV_W):
        y = y + ubuf[base + j:base + j + L, :] * cw_ref[j:j + 1, :]
    tail = ubuf[L:L + SUBLANE, :]
    ubuf[0:SUBLANE, :] = tail
    act = _silu(y)

    ba = ba_ref[...]
    beta_all = _sigmoid(ba)
    g_all = -jnp.exp(alog_ref[...]) * _softplus(ba + dtb_ref[...])
    ri = lax.broadcasted_iota(jnp.int32, (L, L), 0)
    ci = lax.broadcasted_iota(jnp.int32, (L, L), 1)
    causal = ri >= ci
    strict = ri > ci
    eye = (ri == ci).astype(F32)
    gc_cols = _mm(causal.astype(F32), g_all, HIGHEST)
    gc_rows = _mm_tn(g_all, (ri <= ci).astype(F32), HIGHEST)

    n_sq = int(math.log2(L)) - 1
    for h in range(GDN_HEADS):
        sl = slice(h * GDN_DK, (h + 1) * GDN_DK)
        qh = act[:, sl]
        kh = act[:, GDN_W + h * GDN_DK:GDN_W + (h + 1) * GDN_DK]
        vh = act[:, 2 * GDN_W + h * GDN_DV:2 * GDN_W + (h + 1) * GDN_DV]
        qh = qh * lax.rsqrt(jnp.sum(qh * qh, axis=-1, keepdims=True) + EPS) * (GDN_DK ** -0.5)
        kh = kh * lax.rsqrt(jnp.sum(kh * kh, axis=-1, keepdims=True) + EPS)
        beta = beta_all[:, h:h + 1]
        gcol = gc_cols[:, GDN_HEADS + h:GDN_HEADS + h + 1]
        grow = gc_rows[GDN_HEADS + h:GDN_HEADS + h + 1, :]
        glast = gc_cols[L - 1:L, GDN_HEADS + h:GDN_HEADS + h + 1]
        diff = gcol - grow
        dm = jnp.where(causal, jnp.exp(jnp.where(causal, diff, 0.0)), 0.0)
        kk = _mm_nt(kh, kh)
        qk = _mm_nt(qh, kh)
        a_mat = jnp.where(strict, beta * kk * dm, 0.0)
        t_inv = eye - a_mat
        pw = a_mat
        for _ in range(n_sq):
            pw = _mm(pw, pw, HIGHEST)
            t_inv = t_inv + _mm(t_inv, pw, HIGHEST)
        eg = jnp.exp(gcol)
        rhs = jnp.concatenate([beta * vh, (beta * eg) * kh], axis=-1)
        sol = _mm(t_inv, rhs)
        uv = sol[:, :GDN_DV]
        wk = sol[:, GDN_DV:]
        s_h = s_scr[h]
        u = uv - _mm(wk, s_h)
        o = _mm(qh * eg, s_h) + _mm(qk * dm, u)
        kd = kh * jnp.exp(glast - gcol)
        s_scr[h] = s_h * jnp.exp(glast) + _mm_tn(kd, u)
        on = o * lax.rsqrt(jnp.mean(o * o, axis=-1, keepdims=True) + EPS) * on_ref[...]
        o_ref[:, sl] = (on * _silu(z_ref[:, sl])).astype(o_ref.dtype)

    @pl.when(i == nsteps - 1)
    def _():
        sfin_ref[...] = s_scr[...]


def _gdn(proj, conv_prev, s0, conv_w, a_log, dt_bias, onorm_g, batch, seq):
    L = min(GDN_CHUNK, seq)
    assert seq % L == 0 and L % SUBLANE == 0
    nc = seq // L
    prev_pad = jnp.pad(conv_prev, ((0, 0), (SUBLANE - (CONV_W - 1), 0), (0, 0)))
    lane_pad = (GDN_HEADS, LANE - 2 * GDN_HEADS)
    alog_row = jnp.pad(a_log, lane_pad).reshape(1, LANE)
    dtb_row = jnp.pad(dt_bias, lane_pad).reshape(1, LANE)
    row = lambda c: (lambda b, i: (b * nc + i, c))
    kern = functools.partial(_gdn_kernel, chunk=L, nsteps=nc)
    return pl.pallas_call(
        kern,
        grid=(batch, nc),
        in_specs=[
            pl.BlockSpec((L, SEG), row(COL_Q)),
            pl.BlockSpec((L, SEG), row(COL_K)),
            pl.BlockSpec((L, SEG), row(COL_V)),
            pl.BlockSpec((L, SEG), row(COL_ZG)),
            pl.BlockSpec((L, LANE), row(COL_BA // LANE)),
            pl.BlockSpec((CONV_W, 3 * GDN_W), lambda b, i: (0, 0)),
            pl.BlockSpec((None, SUBLANE, 3 * GDN_W), lambda b, i: (b, 0, 0)),
            pl.BlockSpec((None, GDN_HEADS, GDN_DK, GDN_DV), lambda b, i: (b, 0, 0, 0)),
            pl.BlockSpec((1, LANE), lambda b, i: (0, 0)),
            pl.BlockSpec((1, LANE), lambda b, i: (0, 0)),
            pl.BlockSpec((1, GDN_DV), lambda b, i: (0, 0)),
        ],
        out_specs=[
            pl.BlockSpec((L, GDN_W), lambda b, i: (b * nc + i, 0)),
            pl.BlockSpec((None, GDN_HEADS, GDN_DK, GDN_DV), lambda b, i: (b, 0, 0, 0)),
        ],
        out_shape=[
            jax.ShapeDtypeStruct((batch * seq, GDN_W), BF16),
            jax.ShapeDtypeStruct((batch, GDN_HEADS, GDN_DK, GDN_DV), F32),
        ],
        scratch_shapes=[
            pltpu.VMEM((L + SUBLANE, 3 * GDN_W), F32),
            pltpu.VMEM((GDN_HEADS, GDN_DK, GDN_DV), F32),
        ],
        compiler_params=pltpu.CompilerParams(
            dimension_semantics=("arbitrary", "arbitrary"), vmem_limit_bytes=VMEM_LIMIT),
        name="gdn_scan",
    )(proj, proj, proj, proj, proj, conv_w, prev_pad, s0, alog_row, dtb_row, onorm_g.reshape(1, GDN_DV))


def _head_sums(xs, gmat_b):
    t = xs[0].shape[0]
    parts = []
    for x in xs:
        for piece in _split_bf16(x):
            parts += [piece[:, g * GROUP:(g + 1) * GROUP] for g in range(N_GROUPS)]
    res = _mm(jnp.concatenate(parts, axis=0), gmat_b)
    outs = []
    for i in range(len(xs)):
        blk = lambda p, g: res[((2 * i + p) * N_GROUPS + g) * t:((2 * i + p) * N_GROUPS + g + 1) * t]
        outs.append(jnp.concatenate([blk(0, g) + blk(1, g) for g in range(N_GROUPS)], axis=1))
    return outs


def _tile_rows(x, reps):
    return jnp.concatenate([x] * reps, axis=0)


def _rwkv_kernel(r_ref, k_ref, v_ref, xw_ref, xa_ref, z_ref, pr_ref, pk_ref, pv_ref, pxw_ref, pxa_ref, s0_ref,
                 mu_ref, muw_ref, mua_ref, w0_ref, wl_ref, a0_ref, al_ref, kk_ref, ka_ref, rk_ref,
                 lnw_ref, lnb_ref,
                 o_ref, sfin_ref,
                 sbuf, lbuf, sbt, *, nb, chunk, nsteps):
    i = pl.program_id(1)
    L = chunk
    pw_w = GROUP_HEADS * L
    l_bits = L.bit_length() - 1
    hd_bits = RWKV_HD.bit_length() - 1

    def iota(shape, dim):
        return lax.broadcasted_iota(jnp.int32, shape, dim)

    m_pk_b = ((iota((pw_w, GROUP), 0) >> l_bits) == (iota((pw_w, GROUP), 1) >> hd_bits)).astype(BF16)
    m_pp_b = ((iota((pw_w, pw_w), 0) >> l_bits) == (iota((pw_w, pw_w), 1) >> l_bits)).astype(BF16)
    m_kk = (iota((GROUP, GROUP), 0) >> hd_bits) == (iota((GROUP, GROUP), 1) >> hd_bits)
    m_kk_b = m_kk.astype(BF16)
    t_row = iota((L, pw_w), 0)
    j_lane = iota((L, pw_w), 1) & (L - 1)
    strict = j_lane < t_row
    incl = j_lane <= t_row
    eye = (j_lane == t_row).astype(F32)
    tri_b = (iota((L, L), 0) >= iota((L, L), 1)).astype(BF16)

    def bdiag(x):
        return _tile_rows(x.astype(BF16), GROUP_HEADS) * m_pk_b

    @pl.when(i == 0)
    def _():
        sbt[...] = s0_ref[...]
        for n in range(nb):
            sbuf[n, SUBLANE - 1:SUBLANE, 0:SEG] = pr_ref[n]
            sbuf[n, SUBLANE - 1:SUBLANE, SEG:2 * SEG] = pk_ref[n]
            sbuf[n, SUBLANE - 1:SUBLANE, 2 * SEG:3 * SEG] = pv_ref[n]
            lbuf[n, SUBLANE - 1:SUBLANE, 0:LANE] = pxw_ref[n]
            lbuf[n, SUBLANE - 1:SUBLANE, LANE:2 * LANE] = pxa_ref[n]

    inv_hd = 1.0 / RWKV_HD
    for n in range(nb):
        sbuf[n, SUBLANE:SUBLANE + L, 0:SEG] = r_ref[n]
        sbuf[n, SUBLANE:SUBLANE + L, SEG:2 * SEG] = k_ref[n]
        sbuf[n, SUBLANE:SUBLANE + L, 2 * SEG:3 * SEG] = v_ref[n]
        lbuf[n, SUBLANE:SUBLANE + L, 0:LANE] = xw_ref[n]
        lbuf[n, SUBLANE:SUBLANE + L, LANE:2 * LANE] = xa_ref[n]
        cur = sbuf[n, SUBLANE:SUBLANE + L, :]
        prv = sbuf[n, SUBLANE - 1:SUBLANE - 1 + L, :]
        rkv = cur + (prv - cur) * mu_ref[...]
        curl = lbuf[n, SUBLANE:SUBLANE + L, :]
        prvl = lbuf[n, SUBLANE - 1:SUBLANE - 1 + L, :]
        xw = curl[:, 0:LANE] + (prvl[:, 0:LANE] - curl[:, 0:LANE]) * muw_ref[...]
        xa = curl[:, LANE:] + (prvl[:, LANE:] - curl[:, LANE:]) * mua_ref[...]
        sbuf[n, 0:SUBLANE, :] = sbuf[n, L:L + SUBLANE, :]
        lbuf[n, 0:SUBLANE, :] = lbuf[n, L:L + SUBLANE, :]
        r = rkv[:, 0:SEG]
        kr = rkv[:, SEG:2 * SEG]
        vr = rkv[:, 2 * SEG:3 * SEG]
        wl = w0_ref[...] + _mm(jnp.tanh(xw).astype(BF16), wl_ref[...])
        logw = -math.exp(-W_DECAY_OFFSET) * _sigmoid(wl)
        a = _sigmoid(a0_ref[...] + _mm(xa.astype(BF16), al_ref[...]))
        kkr = kr * kk_ref[...]
        kt = kr * (1.0 + (a - 1.0) * ka_ref[...])
        ss, bonus = _head_sums([kkr * kkr, r * kt * rk_ref[...]], m_kk_b)
        kk = kkr * lax.rsqrt(ss + EPS)
        ah = a * kk
        lw_hi, lw_lo = _split_bf16(logw)
        lw_mid, lw_lo = _split_bf16(logw - lw_hi.astype(F32))
        cum3 = _mm(tri_b, jnp.concatenate([lw_hi, lw_mid, lw_lo], axis=1))
        cum = cum3[:, 0:SEG] + (cum3[:, SEG:2 * SEG] + cum3[:, 2 * SEG:])
        c_last = cum[L - 1:L, :]
        e_neg = jnp.exp(-cum)
        e_last = jnp.exp(c_last - cum)
        kx = kk * jnp.exp(cum - logw)
        rx = r * jnp.exp(cum)
        kb = kt * e_neg
        ab = ah * e_neg
        kh = kt * e_last
        ahh = ah * e_last
        g_last = jnp.exp(c_last)
        ys = []
        for g in range(N_GROUPS):
            sl = slice(g * GROUP, (g + 1) * GROUP)
            lhs = jnp.concatenate([kx[:, sl], rx[:, sl]], axis=0).astype(BF16)
            rhs_nt = jnp.concatenate([bdiag(kb[:, sl]), bdiag(ab[:, sl])], axis=0)
            prod = _mm_nt(lhs, rhs_nt)
            c_mat = jnp.where(strict, prod[0:L, 0:pw_w], 0.0)
            e_mat = jnp.where(strict, prod[0:L, pw_w:], 0.0)
            rk_mat = jnp.where(incl, prod[L:, 0:pw_w], 0.0)
            ra_mat = jnp.where(incl, prod[L:, pw_w:], 0.0)
            pw = -e_mat
            t_inv = eye + pw
            for s in range(l_bits):
                first, last = s == 0, s == l_bits - 1
                if first and last:
                    break
                parts = ([] if first else [t_inv]) + ([] if last else [pw])
                bd_hi, bd_lo = [], []
                for x in parts:
                    x_hi, x_lo = _split_bf16(x)
                    bd_hi.append(_tile_rows(x_hi, GROUP_HEADS) * m_pp_b)
                    bd_lo.append(_tile_rows(x_lo, GROUP_HEADS) * m_pp_b)
                bd_hi = jnp.concatenate(bd_hi, axis=1)
                bd_lo = jnp.concatenate(bd_lo, axis=1)
                p_hi, p_lo = _split_bf16(pw)
                res = _mm(p_hi, bd_hi) + (_mm(p_hi, bd_lo) + _mm(p_lo, bd_hi))
                if not first:
                    t_inv = t_inv + res[:, 0:pw_w]
                if not last:
                    pw = res[:, res.shape[1] - pw_w:]
            s_g = sbt[n, g]
            ks_rs = _mm_nt(lhs, s_g.astype(BF16))
            cv = _mm(jnp.concatenate([c_mat, rk_mat], axis=0).astype(BF16), bdiag(vr[:, sl]))
            w_mat = ks_rs[0:L] + cv[0:L]
            u = _mm(t_inv.astype(BF16), bdiag(w_mat))
            y_g = ks_rs[L:] + cv[L:] - _mm(ra_mat.astype(BF16), bdiag(u))
            upd = _mm_tn(jnp.concatenate([vr[:, sl], -u], axis=0).astype(BF16),
                         jnp.concatenate([kh[:, sl], ahh[:, sl]], axis=0).astype(BF16))
            sbt[n, g] = s_g * g_last[:, sl] + jnp.where(m_kk, upd, 0.0)
            ys.append(y_g)
        y = jnp.concatenate(ys, axis=1)
        dlt = y - _head_sums([y], m_kk_b)[0] * inv_hd
        var = _head_sums([dlt * dlt], m_kk_b)[0] * inv_hd
        yn = dlt * lax.rsqrt(var + GN_EPS) * lnw_ref[...] + lnb_ref[...]
        o_ref[n] = ((yn + bonus * vr) * _silu(z_ref[n])).astype(o_ref.dtype)

    @pl.when(i == nsteps - 1)
    def _():
        sfin_ref[...] = sbt[...]


def _rwkv(proj3, shift_prev, s0, mu, w0, w_lora, a0, a_lora, k_k, k_a, r_k, ln_w, ln_b, nb):
    b, t, _ = proj3.shape
    L = min(RWKV_CHUNK, t)
    assert b % nb == 0 and t % L == 0 and L % SUBLANE == 0 and L & (L - 1) == 0
    nsteps = t // L
    w3 = 3 * RWKV_W
    padl = lambda x, n: jnp.pad(x, [(0, 0)] * (x.ndim - 1) + [(0, LANE - n)])
    p_r, p_k, p_v = shift_prev[..., 0:RWKV_W], shift_prev[..., RWKV_W:2 * RWKV_W], shift_prev[..., 2 * RWKV_W:w3]
    p_xw = padl(shift_prev[..., w3:w3 + W_LORA], W_LORA)
    p_xa = padl(shift_prev[..., w3 + W_LORA:], A_LORA)
    mu_rkv = mu[0:w3].reshape(1, w3)
    mu_w = padl(mu[w3:w3 + W_LORA], W_LORA).reshape(1, LANE)
    mu_a = padl(mu[w3 + W_LORA:], A_LORA).reshape(1, LANE)
    wl_pad = jnp.pad(w_lora, ((0, LANE - W_LORA), (0, 0))).astype(BF16)
    al_pad = jnp.pad(a_lora, ((0, LANE - A_LORA), (0, 0))).astype(BF16)
    s5 = s0.reshape(b, N_GROUPS, GROUP_HEADS, RWKV_HD, RWKV_HD).transpose(0, 1, 2, 4, 3)
    s0_bd = jnp.einsum("bghvk,hj->bghvjk", s5, jnp.eye(GROUP_HEADS, dtype=s0.dtype))
    s0_bd = s0_bd.reshape(b, N_GROUPS, GROUP, GROUP)
    row1 = lambda x: x.reshape(1, RWKV_W)
    blk = lambda c, w: pl.BlockSpec((nb, L, w), lambda g, i, c=c: (g, i, c))
    prevspec = lambda w: pl.BlockSpec((nb, 1, w), lambda g, i: (g, 0, 0))
    const = lambda shape: pl.BlockSpec(shape, lambda g, i: (0,) * len(shape))
    state_spec = pl.BlockSpec((nb, N_GROUPS, GROUP, GROUP), lambda g, i: (g, 0, 0, 0))
    kern = functools.partial(_rwkv_kernel, nb=nb, chunk=L, nsteps=nsteps)
    o, sfin = pl.pallas_call(
        kern,
        grid=(b // nb, nsteps),
        in_specs=[
            blk(COL_R, SEG), blk(COL_RK, SEG), blk(COL_RV, SEG),
            blk(COL_XW // LANE, LANE), blk(COL_XA // LANE, LANE), blk(COL_ZR, SEG),
            prevspec(SEG), prevspec(SEG), prevspec(SEG), prevspec(LANE), prevspec(LANE),
            state_spec,
            const((1, w3)), const((1, LANE)), const((1, LANE)),
            const((1, RWKV_W)), const((LANE, RWKV_W)), const((1, RWKV_W)), const((LANE, RWKV_W)),
            const((1, RWKV_W)), const((1, RWKV_W)), const((1, RWKV_W)), const((1, RWKV_W)), const((1, RWKV_W)),
        ],
        out_specs=[
            pl.BlockSpec((nb, L, RWKV_W), lambda g, i: (g, i, 0)),
            state_spec,
        ],
        out_shape=[
            jax.ShapeDtypeStruct((b, t, RWKV_W), BF16),
            jax.ShapeDtypeStruct((b, N_GROUPS, GROUP, GROUP), F32),
        ],
        scratch_shapes=[
            pltpu.VMEM((nb, L + SUBLANE, w3), F32),
            pltpu.VMEM((nb, L + SUBLANE, 2 * LANE), F32),
            pltpu.VMEM((nb, N_GROUPS, GROUP, GROUP), F32),
        ],
        compiler_params=pltpu.CompilerParams(
            dimension_semantics=("arbitrary", "arbitrary"), vmem_limit_bytes=VMEM_LIMIT),
        name="rwkv_scan",
    )(proj3, proj3, proj3, proj3, proj3, proj3, p_r, p_k, p_v, p_xw, p_xa, s0_bd,
      mu_rkv, mu_w, mu_a, row1(w0), wl_pad, row1(a0), al_pad, row1(k_k), row1(k_a), row1(r_k),
      row1(ln_w), row1(ln_b))
    s6 = sfin.reshape(b, N_GROUPS, GROUP_HEADS, RWKV_HD, GROUP_HEADS, RWKV_HD)
    s_new = jnp.einsum("bghvjk,hj->bghkv", s6, jnp.eye(GROUP_HEADS, dtype=sfin.dtype))
    return o.reshape(b * t, RWKV_W), s_new.reshape(b, RWKV_HEADS, RWKV_HD, RWKV_HD)


def _merge_kernel(og_ref, or_ref, gg_ref, gr_ref, x_ref, gate_ref, wog_ref, wor_ref, wout_ref, fg_ref, o_ref,
                  *, final_norm):
    m = _sigmoid(gg_ref[...]) * _mm(og_ref[...], wog_ref[...]) \
        + _sigmoid(gr_ref[...]) * _mm(or_ref[...], wor_ref[...])
    out = _mm(m.astype(BF16), wout_ref[...])
    xn = x_ref[...] + gate_ref[...] * out
    if final_norm:
        xn = xn * lax.rsqrt(jnp.mean(xn * xn, axis=-1, keepdims=True) + EPS) * fg_ref[...]
    o_ref[...] = xn


def _merge(o_g, o_r, proj, x2d, gate, w_og, w_or, w_out, final_g, rows_per_mod, final_norm):
    m, d = x2d.shape
    tm = min(m, 256) if rows_per_mod == 1 else min(rows_per_mod, 256)
    if rows_per_mod == 1:
        gate_spec = pl.BlockSpec((tm, d), lambda i: (i, 0))
    else:
        assert rows_per_mod % tm == 0
        per = rows_per_mod // tm
        gate = gate.reshape(-1, 1, d)
        gate_spec = pl.BlockSpec((None, 1, d), lambda i: (i // per, 0, 0))
    whole = lambda shape: pl.BlockSpec(shape, lambda i: (0, 0))
    gcol = COL_GATES // d
    return pl.pallas_call(
        functools.partial(_merge_kernel, final_norm=final_norm),
        grid=(m // tm,),
        in_specs=[
            pl.BlockSpec((tm, GDN_W), lambda i: (i, 0)),
            pl.BlockSpec((tm, RWKV_W), lambda i: (i, 0)),
            pl.BlockSpec((tm, d), lambda i: (i, gcol)),
            pl.BlockSpec((tm, d), lambda i: (i, gcol + 1)),
            pl.BlockSpec((tm, d), lambda i: (i, 0)),
            gate_spec,
            whole((GDN_W, d)), whole((RWKV_W, d)), whole((d, d)), whole((1, d)),
        ],
        out_specs=pl.BlockSpec((tm, d), lambda i: (i, 0)),
        out_shape=jax.ShapeDtypeStruct((m, d), F32),
        compiler_params=pltpu.CompilerParams(
            dimension_semantics=("arbitrary",), vmem_limit_bytes=VMEM_LIMIT),
        name="merge_out",
    )(o_g, o_r, proj, proj, x2d, gate, w_og, w_or, w_out, final_g.reshape(1, d))


def _pad_in_weight(w_in):
    d = w_in.shape[0]
    o_zg = 3 * GDN_W
    o_b = o_zg + GDN_W
    o_rw = o_b + 2 * GDN_HEADS
    o_xw = o_rw + 3 * RWKV_W
    o_xa = o_xw + W_LORA
    o_zr = o_xa + A_LORA
    o_br = o_zr + RWKV_W
    z = lambda n: jnp.zeros((d, n), w_in.dtype)
    cols = [
        w_in[:, 0:o_zg], w_in[:, o_zg:o_b],
        w_in[:, o_rw:o_xw], w_in[:, o_zr:o_br],
        w_in[:, o_br:o_br + 2 * D_MODEL],
        w_in[:, o_b:o_rw], z(LANE - 2 * GDN_HEADS),
        w_in[:, o_xw:o_xa], z(LANE - W_LORA),
        w_in[:, o_xa:o_zr], z(LANE - A_LORA),
    ]
    used = COL_XA + LANE
    cols.append(z(N_PROJ - used))
    return jnp.concatenate(cols, axis=1).astype(BF16)


def _layer(x2d, batch, seq, scale, shift, gate, rows_per_mod, conv_prev, s_gdn, shift_prev, s_rwkv, p, final_g,
           final_norm, rwkv_nb):
    proj = _in_proj(x2d, scale, shift, p["norm_g"], p["w_in_pad"], rows_per_mod)
    o_g, s_gdn_new = _gdn(proj, conv_prev, s_gdn, p["conv_w"], p["a_log"], p["dt_bias"], p["onorm_g"], batch, seq)
    proj3 = proj.reshape(batch, seq, N_PROJ)
    o_r, s_rwkv_new = _rwkv(proj3, shift_prev, s_rwkv, p["mu"], p["w0"], p["w_lora"], p["a0"], p["a_lora"],
                            p["k_k"], p["k_a"], p["r_k"], p["ln_w"], p["ln_b"], rwkv_nb)
    x_new = _merge(o_g, o_r, proj, x2d, gate, p["w_o_gdn"], p["w_o_rwkv"], p["w_out"], final_g, rows_per_mod,
                   final_norm)
    conv_new = proj3[:, seq - (CONV_W - 1):, 0:3 * GDN_W]
    last = proj3[:, seq - 1:, :]
    shift_new = jnp.concatenate(
        [last[..., COL_R * SEG:COL_R * SEG + 3 * RWKV_W], last[..., COL_XW:COL_XW + W_LORA],
         last[..., COL_XA:COL_XA + A_LORA]], axis=-1)
    return x_new, conv_new, s_gdn_new, shift_new, s_rwkv_new


def _forward(x_prompt, x_sample, c_prompt, c_sample, cache_gdn_conv, state_gdn, cache_rwkv_shift, state_rwkv,
             ada_w, ada_b, norm_g, w_in, gdn_conv_w, gdn_a_log, gdn_dt_bias, gdn_out_norm_g,
             rwkv_mu, rwkv_w0, rwkv_w_lora, rwkv_a0, rwkv_a_lora, rwkv_k_k, rwkv_k_a, rwkv_r_k,
             rwkv_ln_w, rwkv_ln_b, w_o_gdn, w_o_rwkv, w_out, final_norm_g):
    depth = ada_w.shape[0]
    bp, tp, d = x_prompt.shape
    bs, ts, _ = x_sample.shape
    assert CONV_W - 1 <= min(tp, ts)
    c_all = jnp.concatenate([c_prompt, c_sample], axis=0)
    rows = -(-(bp + bs) // SUBLANE) * SUBLANE
    c_all = jnp.pad(c_all, ((0, rows - (bp + bs)), (0, 0)))
    mod = _ada_mod(c_all, ada_w, ada_b)
    xp = x_prompt.reshape(bp * tp, d)
    xs = x_sample.reshape(bs * ts, d)
    outs = [[] for _ in range(8)]
    for l in range(depth):
        p = dict(norm_g=norm_g[l], w_in_pad=_pad_in_weight(w_in[l]), conv_w=gdn_conv_w[l], a_log=gdn_a_log[l],
                 dt_bias=gdn_dt_bias[l], onorm_g=gdn_out_norm_g[l], mu=rwkv_mu[l], w0=rwkv_w0[l],
                 w_lora=rwkv_w_lora[l], a0=rwkv_a0[l], a_lora=rwkv_a_lora[l], k_k=rwkv_k_k[l], k_a=rwkv_k_a[l],
                 r_k=rwkv_r_k[l].reshape(-1), ln_w=rwkv_ln_w[l], ln_b=rwkv_ln_b[l],
                 w_o_gdn=w_o_gdn[l].astype(BF16), w_o_rwkv=w_o_rwkv[l].astype(BF16), w_out=w_out[l].astype(BF16))
        last = l == depth - 1
        m_p = mod[l, 0:bp]
        m_s = jnp.repeat(mod[l, bp:bp + bs], ts, axis=0)
        sh_p, sc_p, gt_p = m_p[:, 0:d], m_p[:, d:2 * d], m_p[:, 2 * d:]
        sh_s, sc_s, gt_s = m_s[:, 0:d], m_s[:, d:2 * d], m_s[:, 2 * d:]
        zeros = lambda *s: jnp.zeros(s, F32)
        xp, c1, g1, h1, r1 = _layer(
            xp, bp, tp, sc_p, sh_p, gt_p, tp,
            zeros(bp, CONV_W - 1, 3 * GDN_W), zeros(bp, GDN_HEADS, GDN_DK, GDN_DV),
            zeros(bp, 1, 3 * RWKV_W + W_LORA + A_LORA), zeros(bp, RWKV_HEADS, RWKV_HD, RWKV_HD),
            p, final_norm_g, last, rwkv_nb=bp)
        xs, c2, g2, h2, r2 = _layer(
            xs, bs, ts, sc_s, sh_s, gt_s, 1,
            cache_gdn_conv[l], state_gdn[l], cache_rwkv_shift[l], state_rwkv[l],
            p, final_norm_g, last, rwkv_nb=min(bs, 2))
        for lst, val in zip(outs, (c1, g1, h1, r1, c2, g2, h2, r2)):
            lst.append(val)
    stk = [jnp.stack(v) for v in outs]
    return (xp.reshape(bp, tp, d), xs.reshape(bs, ts, d), *stk)


def kernel(x_prompt, x_sample, c_prompt, c_sample, cache_gdn_conv, state_gdn, cache_rwkv_shift, state_rwkv, ada_w, ada_b, norm_g, w_in, gdn_conv_w, gdn_a_log, gdn_dt_bias, gdn_out_norm_g, rwkv_mu, rwkv_w0, rwkv_w_lora, rwkv_a0, rwkv_a_lora, rwkv_k_k, rwkv_k_a, rwkv_r_k, rwkv_ln_w, rwkv_ln_b, w_o_gdn, w_o_rwkv, w_out, final_norm_g):
    return _forward(x_prompt, x_sample, c_prompt, c_sample, cache_gdn_conv, state_gdn, cache_rwkv_shift,
                    state_rwkv, ada_w, ada_b, norm_g, w_in, gdn_conv_w, gdn_a_log, gdn_dt_bias, gdn_out_norm_g,
                    rwkv_mu, rwkv_w0, rwkv_w_lora, rwkv_a0, rwkv_a_lora, rwkv_k_k, rwkv_k_a, rwkv_r_k,
                    rwkv_ln_w, rwkv_ln_b, w_o_gdn, w_o_rwkv, w_out, final_norm_g)
```

```python
import functools
import math

import jax
import jax.numpy as jnp
from jax import lax
from jax.experimental import pallas as pl
from jax.experimental.pallas import tpu as pltpu

F32 = jnp.float32
BF16 = jnp.bfloat16

D_MODEL = 2048
GDN_HEADS = 8
GDN_DK = 128
GDN_DV = 128
GDN_W = GDN_HEADS * GDN_DK
GDN_CHUNK = 64
CONV_W = 4
RWKV_HEADS = 16
RWKV_HD = 64
RWKV_W = RWKV_HEADS * RWKV_HD
RWKV_CHUNK = 64
W_LORA = 96
A_LORA = 96
EPS = 1e-6
GN_EPS = 64e-5
W_DECAY_OFFSET = 0.5

LANE = 128
SUBLANE = 8
SEG = 1024
COL_Q, COL_K, COL_V, COL_ZG, COL_R, COL_RK, COL_RV, COL_ZR = 0, 1, 2, 3, 4, 5, 6, 7
COL_GATES = 8 * SEG
COL_BA = 12 * SEG
COL_XW = COL_BA + LANE
COL_XA = COL_XW + LANE
N_PROJ = 13 * SEG
GROUP = 256
PACK = 4
RWKV_GROUPS = RWKV_W // GROUP
GDN_PAIRS = GDN_W // GROUP
VMEM_LIMIT = 56 * 1024 * 1024


def _mm(a, b):
    return jnp.dot(a, b, preferred_element_type=F32)


def _mm_nt(a, b):
    return lax.dot_general(a, b, (((1,), (1,)), ((), ())), preferred_element_type=F32)


def _mm_tn(a, b):
    return lax.dot_general(a, b, (((0,), (0,)), ((), ())), preferred_element_type=F32)


def _split_bf16(x):
    hi = x.astype(BF16)
    return hi, (x - hi.astype(F32)).astype(BF16)


def _split3_bf16(x):
    hi = x.astype(BF16)
    mid, lo = _split_bf16(x - hi.astype(F32))
    return hi, mid, lo


def _sigmoid(x):
    return 1.0 / (1.0 + jnp.exp(-x))


def _silu(x):
    return x * _sigmoid(x)


def _softplus(x):
    return jnp.maximum(x, 0.0) + jnp.log1p(jnp.exp(-jnp.abs(x)))


def _iota(shape, dim):
    return lax.broadcasted_iota(jnp.int32, shape, dim)


def _bits(n):
    assert n & (n - 1) == 0
    return n.bit_length() - 1


def _mod_kernel(c_ref, w_ref, b_ref, o_ref):
    o_ref[...] = _mm(c_ref[...], w_ref[...]) + b_ref[...]


def _ada_mod(c_all, ada_w, ada_b):
    depth, d, n3 = ada_w.shape
    rows = c_all.shape[0]
    tn = 512
    return pl.pallas_call(
        _mod_kernel,
        grid=(depth, n3 // tn),
        in_specs=[
            pl.BlockSpec((rows, d), lambda l, j: (0, 0)),
            pl.BlockSpec((None, d, tn), lambda l, j: (l, 0, j)),
            pl.BlockSpec((None, 1, tn), lambda l, j: (l, 0, j)),
        ],
        out_specs=pl.BlockSpec((None, rows, tn), lambda l, j: (l, 0, j)),
        out_shape=jax.ShapeDtypeStruct((depth, rows, n3), F32),
        compiler_params=pltpu.CompilerParams(
            dimension_semantics=("arbitrary", "arbitrary"), vmem_limit_bytes=VMEM_LIMIT),
        name="ada_mod",
    )(c_all, ada_w, ada_b.reshape(depth, 1, n3))


def _in_proj_kernel(x_ref, sc_ref, sh_ref, g_ref, w_ref, o_ref, h_ref):
    @pl.when(pl.program_id(1) == 0)
    def _():
        x = x_ref[...]
        ms = jnp.mean(x * x, axis=-1, keepdims=True)
        h = x * lax.rsqrt(ms + EPS) * g_ref[...] * (1.0 + sc_ref[...]) + sh_ref[...]
        h_ref[...] = h.astype(BF16)

    o_ref[...] = _mm(h_ref[...], w_ref[...])


def _in_proj(x2d, scale, shift, norm_g, w_pad, rows_per_mod):
    m, d = x2d.shape
    tm = min(m, 512) if rows_per_mod == 1 else min(rows_per_mod, 1024)
    tn = SEG
    if rows_per_mod == 1:
        mod_spec = pl.BlockSpec((tm, d), lambda i, j: (i, 0))
    else:
        assert rows_per_mod % tm == 0
        per = rows_per_mod // tm
        scale = scale.reshape(-1, 1, d)
        shift = shift.reshape(-1, 1, d)
        mod_spec = pl.BlockSpec((None, 1, d), lambda i, j: (i // per, 0, 0))
    return pl.pallas_call(
        _in_proj_kernel,
        grid=(m // tm, N_PROJ // tn),
        in_specs=[
            pl.BlockSpec((tm, d), lambda i, j: (i, 0)),
            mod_spec, mod_spec,
            pl.BlockSpec((1, d), lambda i, j: (0, 0)),
            pl.BlockSpec((d, tn), lambda i, j: (0, j)),
        ],
        out_specs=pl.BlockSpec((tm, tn), lambda i, j: (i, j)),
        out_shape=jax.ShapeDtypeStruct((m, N_PROJ), F32),
        scratch_shapes=[pltpu.VMEM((tm, d), BF16)],
        compiler_params=pltpu.CompilerParams(
            dimension_semantics=("arbitrary", "arbitrary"), vmem_limit_bytes=VMEM_LIMIT),
        name="in_proj",
    )(x2d, scale, shift, norm_g.reshape(1, d), w_pad)


def _tile_rows(x, reps):
    return jnp.concatenate([x] * reps, axis=0)


def _head_mask(rows, row_bits, cols, col_bits):
    return (_iota((rows, cols), 0) >> row_bits) == (_iota((rows, cols), 1) >> col_bits)


def _block_sums(xs, gmat_b):
    t, width = xs[0].shape
    ng = width // GROUP
    parts = []
    for x in xs:
        for piece in _split_bf16(x):
            parts += [piece[:, g * GROUP:(g + 1) * GROUP] for g in range(ng)]
    res = _mm(jnp.concatenate(parts, axis=0), gmat_b)
    outs = []
    for i in range(len(xs)):
        blk = lambda p, g: res[((2 * i + p) * ng + g) * t:((2 * i + p) * ng + g + 1) * t]
        outs.append(jnp.concatenate([blk(0, g) + blk(1, g) for g in range(ng)], axis=1))
    return outs


def _cumsum_rows(x, tri_b):
    w = x.shape[1]
    res = _mm(tri_b, jnp.concatenate(_split3_bf16(x), axis=1))
    return res[:, 0:w] + (res[:, w:2 * w] + res[:, 2 * w:])


def _unit_lower_inverse(e_mat, eye, m_pp_b):
    L, pw_w = e_mat.shape
    steps = _bits(L)
    pw = -e_mat
    t_inv = eye + pw
    for s in range(steps):
        first, last = s == 0, s == steps - 1
        if first and last:
            break
        parts = ([] if first else [t_inv]) + ([] if last else [pw])
        bd_hi, bd_lo = [], []
        for x in parts:
            x_hi, x_lo = _split_bf16(x)
            bd_hi.append(_tile_rows(x_hi, PACK) * m_pp_b)
            bd_lo.append(_tile_rows(x_lo, PACK) * m_pp_b)
        bd_hi = jnp.concatenate(bd_hi, axis=1)
        bd_lo = jnp.concatenate(bd_lo, axis=1)
        p_hi, p_lo = _split_bf16(pw)
        res = _mm(p_hi, bd_hi) + (_mm(p_hi, bd_lo) + _mm(p_lo, bd_hi))
        if not first:
            t_inv = t_inv + res[:, 0:pw_w]
        if not last:
            pw = res[:, res.shape[1] - pw_w:]
    return t_inv


def _gdn_kernel(q_ref, k_ref, v_ref, z_ref, ba_ref, cw_ref, prev_ref, s0_ref, alog_ref, dtb_ref, on_ref,
                o_ref, sfin_ref, ubuf, sb, *, nb, chunk, nsteps):
    L = chunk
    i = pl.program_id(1)
    l_bits = _bits(L)
    dk_bits = _bits(GDN_DK)
    pw_w = PACK * L
    hw = PACK * GDN_DK
    n_grp = GDN_HEADS // PACK

    m_pd_b = _head_mask(pw_w, l_bits, hw, dk_bits).astype(BF16)
    m_pp_b = _head_mask(pw_w, l_bits, pw_w, l_bits).astype(BF16)
    m_dd = _head_mask(GROUP, dk_bits, GROUP, dk_bits)
    m_dd_b = m_dd.astype(BF16)
    t_row = _iota((L, pw_w), 0)
    j_lane = _iota((L, pw_w), 1) & (L - 1)
    strict = j_lane < t_row
    incl = j_lane <= t_row
    eye = (j_lane == t_row).astype(F32)
    tri_b = (_iota((L, L), 0) >= _iota((L, L), 1)).astype(BF16)
    ones_b = jnp.ones((L, L), BF16)
    lane = _iota((L, LANE), 1)
    x_rows = _iota((LANE, GDN_HEADS * L + GDN_W), 0) & (GDN_HEADS - 1)
    x_cols = _iota((LANE, GDN_HEADS * L + GDN_W), 1)
    x_head = jnp.where(x_cols < GDN_HEADS * L, x_cols >> l_bits, (x_cols - GDN_HEADS * L) >> dk_bits)
    expand_b = ((x_rows == x_head) & (_iota((LANE, GDN_HEADS * L + GDN_W), 0) < 2 * GDN_HEADS)).astype(BF16)

    def bdiag(x):
        return _tile_rows(x.astype(BF16), PACK) * m_pd_b

    @pl.when(i == 0)
    def _():
        sb[...] = s0_ref[...]
        for n in range(nb):
            ubuf[n, 0:SUBLANE, :] = prev_ref[n]

    base = SUBLANE - (CONV_W - 1)
    for n in range(nb):
        ubuf[n, SUBLANE:SUBLANE + L, 0:GDN_W] = q_ref[n]
        ubuf[n, SUBLANE:SUBLANE + L, GDN_W:2 * GDN_W] = k_ref[n]
        ubuf[n, SUBLANE:SUBLANE + L, 2 * GDN_W:3 * GDN_W] = v_ref[n]
        y = ubuf[n, base:base + L, :] * cw_ref[0:1, :]
        for j in range(1, CONV_W):
            y = y + ubuf[n, base + j:base + j + L, :] * cw_ref[j:j + 1, :]
        ubuf[n, 0:SUBLANE, :] = ubuf[n, L:L + SUBLANE, :]
        act = _silu(y)
        q = act[:, 0:GDN_W]
        k = act[:, GDN_W:2 * GDN_W]
        v = act[:, 2 * GDN_W:]
        ssq, ssk = _block_sums([q * q, k * k], m_dd_b)
        q = q * lax.rsqrt(ssq + EPS) * (GDN_DK ** -0.5)
        k = k * lax.rsqrt(ssk + EPS)

        ba = ba_ref[n]
        beta_c = jnp.where(lane < GDN_HEADS, _sigmoid(ba), 0.0)
        g_c = -jnp.exp(alog_ref[...]) * _softplus(ba + dtb_ref[...])
        gc_c = jnp.where((lane >= GDN_HEADS) & (lane < 2 * GDN_HEADS), _cumsum_rows(g_c, tri_b), 0.0)
        pieces = jnp.concatenate(list(_split_bf16(beta_c)) + list(_split3_bf16(gc_c)), axis=0)
        ex = _mm(pieces, expand_b)
        beta_x = ex[0:L] + ex[L:2 * L]
        gc_x = ex[2 * L:3 * L] + (ex[3 * L:4 * L] + ex[4 * L:])
        beta_p, beta_w = beta_x[:, 0:GDN_HEADS * L], beta_x[:, GDN_HEADS * L:]
        gc_p, gc_w = gc_x[:, 0:GDN_HEADS * L], gc_x[:, GDN_HEADS * L:]
        eye8 = jnp.concatenate([eye] * n_grp, axis=1)
        incl8 = jnp.concatenate([incl] * n_grp, axis=1)
        rw = _mm(ones_b, jnp.concatenate(_split3_bf16(gc_p * eye8), axis=1))
        pw8 = GDN_HEADS * L
        gc_row = rw[:, 0:pw8] + (rw[:, pw8:2 * pw8] + rw[:, 2 * pw8:])
        dm = jnp.where(incl8, jnp.exp(jnp.where(incl8, gc_p - gc_row, 0.0)), 0.0)
        gl_w = gc_w[L - 1:L, :]
        eg_w = jnp.exp(gc_w)
        qd = q * eg_w
        kd = k * jnp.exp(gl_w - gc_w)
        bv = beta_w * v
        bek = beta_w * eg_w * k
        egl_w = jnp.exp(gl_w)
        o_parts = []
        for g in range(n_grp):
            sw = slice(g * hw, (g + 1) * hw)
            sp = slice(g * pw_w, (g + 1) * pw_w)
            prod = _mm_nt(jnp.concatenate([k[:, sw], q[:, sw]], axis=0).astype(BF16), bdiag(k[:, sw]))
            a_mat = jnp.where(strict, beta_p[:, sp] * prod[0:L] * dm[:, sp], 0.0)
            qkd = prod[L:] * dm[:, sp]
            t_inv = _unit_lower_inverse(a_mat, eye, m_pp_b)
            sol = _mm(t_inv.astype(BF16), jnp.concatenate([bdiag(bv[:, sw]), bdiag(bek[:, sw])], axis=1))
            uv, wk = sol[:, 0:hw], sol[:, hw:]
            us, qs = [], []
            for p in range(hw // GROUP):
                pair = g * (hw // GROUP) + p
                s2 = slice(p * GROUP, (p + 1) * GROUP)
                sg = slice(pair * GROUP, (pair + 1) * GROUP)
                s_p = sb[n, pair]
                r2 = _mm(jnp.concatenate([wk[:, s2], qd[:, sg]], axis=0).astype(BF16), s_p.astype(BF16))
                u_p = uv[:, s2] - r2[0:L]
                upd = _mm_tn(kd[:, sg].astype(BF16), u_p.astype(BF16))
                sb[n, pair] = s_p * egl_w[:, sg] + jnp.where(m_dd, upd, 0.0)
                us.append(u_p)
                qs.append(r2[L:])
            u = jnp.concatenate(us, axis=1)
            o_parts.append(jnp.concatenate(qs, axis=1) + _mm(qkd.astype(BF16), bdiag(u)))
        o = jnp.concatenate(o_parts, axis=1)
        sso, = _block_sums([o * o], m_dd_b)
        on = o * lax.rsqrt(sso * (1.0 / GDN_DV) + EPS) * on_ref[...]
        o_ref[n] = (on * _silu(z_ref[n])).astype(o_ref.dtype)

    @pl.when(i == nsteps - 1)
    def _():
        sfin_ref[...] = sb[...]


def _gdn(proj3, conv_prev, s0, conv_w, a_log, dt_bias, onorm_g, nb):
    b, t, _ = proj3.shape
    L = min(GDN_CHUNK, t)
    assert b % nb == 0 and t % L == 0 and L % SUBLANE == 0
    nsteps = t // L
    prev_pad = jnp.pad(conv_prev, ((0, 0), (SUBLANE - (CONV_W - 1), 0), (0, 0)))
    lane_pad = (GDN_HEADS, LANE - 2 * GDN_HEADS)
    alog_row = jnp.pad(a_log, lane_pad).reshape(1, LANE)
    dtb_row = jnp.pad(dt_bias, lane_pad).reshape(1, LANE)
    on_row = jnp.tile(onorm_g, GDN_HEADS).reshape(1, GDN_W)
    hp = GROUP // GDN_DK
    s5 = s0.reshape(b, GDN_PAIRS, hp, GDN_DK, GDN_DV)
    s0_bd = jnp.einsum("bphkv,hj->bphkjv", s5, jnp.eye(hp, dtype=s0.dtype)).reshape(b, GDN_PAIRS, GROUP, GROUP)
    blk = lambda c, w: pl.BlockSpec((nb, L, w), lambda g, i, c=c: (g, i, c))
    const = lambda shape: pl.BlockSpec(shape, lambda g, i: (0,) * len(shape))
    state_spec = pl.BlockSpec((nb, GDN_PAIRS, GROUP, GROUP), lambda g, i: (g, 0, 0, 0))
    kern = functools.partial(_gdn_kernel, nb=nb, chunk=L, nsteps=nsteps)
    o, sfin = pl.pallas_call(
        kern,
        grid=(b // nb, nsteps),
        in_specs=[
            blk(COL_Q, SEG), blk(COL_K, SEG), blk(COL_V, SEG), blk(COL_ZG, SEG), blk(COL_BA // LANE, LANE),
            const((CONV_W, 3 * GDN_W)),
            pl.BlockSpec((nb, SUBLANE, 3 * GDN_W), lambda g, i: (g, 0, 0)),
            state_spec,
            const((1, LANE)), const((1, LANE)), const((1, GDN_W)),
        ],
        out_specs=[
            pl.BlockSpec((nb, L, GDN_W), lambda g, i: (g, i, 0)),
            state_spec,
        ],
        out_shape=[
            jax.ShapeDtypeStruct((b, t, GDN_W), BF16),
            jax.ShapeDtypeStruct((b, GDN_PAIRS, GROUP, GROUP), F32),
        ],
        scratch_shapes=[
            pltpu.VMEM((nb, L + SUBLANE, 3 * GDN_W), F32),
            pltpu.VMEM((nb, GDN_PAIRS, GROUP, GROUP), F32),
        ],
        compiler_params=pltpu.CompilerParams(
            dimension_semantics=("arbitrary", "arbitrary"), vmem_limit_bytes=VMEM_LIMIT),
        name="gdn_scan",
    )(proj3, proj3, proj3, proj3, proj3, conv_w, prev_pad, s0_bd, alog_row, dtb_row, on_row)
    s6 = sfin.reshape(b, GDN_PAIRS, hp, GDN_DK, hp, GDN_DV)
    s_new = jnp.einsum("bphkjv,hj->bphkv", s6, jnp.eye(hp, dtype=sfin.dtype))
    return o.reshape(b * t, GDN_W), s_new.reshape(b, GDN_HEADS, GDN_DK, GDN_DV)


def _rwkv_kernel(r_ref, k_ref, v_ref, xw_ref, xa_ref, z_ref, pr_ref, pk_ref, pv_ref, pxw_ref, pxa_ref, s0_ref,
                 mu_ref, muw_ref, mua_ref, w0_ref, wl_ref, a0_ref, al_ref, kk_ref, ka_ref, rk_ref,
                 lnw_ref, lnb_ref,
                 o_ref, sfin_ref,
                 sbuf, lbuf, sbt, *, nb, chunk, nsteps):
    i = pl.program_id(1)
    L = chunk
    pw_w = PACK * L
    l_bits = _bits(L)
    hd_bits = _bits(RWKV_HD)

    m_pk_b = _head_mask(pw_w, l_bits, GROUP, hd_bits).astype(BF16)
    m_pp_b = _head_mask(pw_w, l_bits, pw_w, l_bits).astype(BF16)
    m_kk = _head_mask(GROUP, hd_bits, GROUP, hd_bits)
    m_kk_b = m_kk.astype(BF16)
    t_row = _iota((L, pw_w), 0)
    j_lane = _iota((L, pw_w), 1) & (L - 1)
    strict = j_lane < t_row
    incl = j_lane <= t_row
    eye = (j_lane == t_row).astype(F32)
    tri_b = (_iota((L, L), 0) >= _iota((L, L), 1)).astype(BF16)

    def bdiag(x):
        return _tile_rows(x.astype(BF16), PACK) * m_pk_b

    @pl.when(i == 0)
    def _():
        sbt[...] = s0_ref[...]
        for n in range(nb):
            sbuf[n, SUBLANE - 1:SUBLANE, 0:SEG] = pr_ref[n]
            sbuf[n, SUBLANE - 1:SUBLANE, SEG:2 * SEG] = pk_ref[n]
            sbuf[n, SUBLANE - 1:SUBLANE, 2 * SEG:3 * SEG] = pv_ref[n]
            lbuf[n, SUBLANE - 1:SUBLANE, 0:LANE] = pxw_ref[n]
            lbuf[n, SUBLANE - 1:SUBLANE, LANE:2 * LANE] = pxa_ref[n]

    inv_hd = 1.0 / RWKV_HD
    for n in range(nb):
        sbuf[n, SUBLANE:SUBLANE + L, 0:SEG] = r_ref[n]
        sbuf[n, SUBLANE:SUBLANE + L, SEG:2 * SEG] = k_ref[n]
        sbuf[n, SUBLANE:SUBLANE + L, 2 * SEG:3 * SEG] = v_ref[n]
        lbuf[n, SUBLANE:SUBLANE + L, 0:LANE] = xw_ref[n]
        lbuf[n, SUBLANE:SUBLANE + L, LANE:2 * LANE] = xa_ref[n]
        cur = sbuf[n, SUBLANE:SUBLANE + L, :]
        prv = sbuf[n, SUBLANE - 1:SUBLANE - 1 + L, :]
        rkv = cur + (prv - cur) * mu_ref[...]
        curl = lbuf[n, SUBLANE:SUBLANE + L, :]
        prvl = lbuf[n, SUBLANE - 1:SUBLANE - 1 + L, :]
        xw = curl[:, 0:LANE] + (prvl[:, 0:LANE] - curl[:, 0:LANE]) * muw_ref[...]
        xa = curl[:, LANE:] + (prvl[:, LANE:] - curl[:, LANE:]) * mua_ref[...]
        sbuf[n, 0:SUBLANE, :] = sbuf[n, L:L + SUBLANE, :]
        lbuf[n, 0:SUBLANE, :] = lbuf[n, L:L + SUBLANE, :]
        r = rkv[:, 0:SEG]
        kr = rkv[:, SEG:2 * SEG]
        vr = rkv[:, 2 * SEG:3 * SEG]
        wl = w0_ref[...] + _mm(jnp.tanh(xw).astype(BF16), wl_ref[...])
        logw = -math.exp(-W_DECAY_OFFSET) * _sigmoid(wl)
        a = _sigmoid(a0_ref[...] + _mm(xa.astype(BF16), al_ref[...]))
        kkr = kr * kk_ref[...]
        kt = kr * (1.0 + (a - 1.0) * ka_ref[...])
        ss, bonus = _block_sums([kkr * kkr, r * kt * rk_ref[...]], m_kk_b)
        kk = kkr * lax.rsqrt(ss + EPS)
        ah = a * kk
        cum = _cumsum_rows(logw, tri_b)
        c_last = cum[L - 1:L, :]
        e_neg = jnp.exp(-cum)
        e_last = jnp.exp(c_last - cum)
        kx = kk * jnp.exp(cum - logw)
        rx = r * jnp.exp(cum)
        kb = kt * e_neg
        ab = ah * e_neg
        kh = kt * e_last
        ahh = ah * e_last
        g_last = jnp.exp(c_last)
        ys = []
        for g in range(RWKV_GROUPS):
            sl = slice(g * GROUP, (g + 1) * GROUP)
            lhs = jnp.concatenate([kx[:, sl], rx[:, sl]], axis=0).astype(BF16)
            rhs_nt = jnp.concatenate([bdiag(kb[:, sl]), bdiag(ab[:, sl])], axis=0)
            prod = _mm_nt(lhs, rhs_nt)
            c_mat = jnp.where(strict, prod[0:L, 0:pw_w], 0.0)
            e_mat = jnp.where(strict, prod[0:L, pw_w:], 0.0)
            rk_mat = jnp.where(incl, prod[L:, 0:pw_w], 0.0)
            ra_mat = jnp.where(incl, prod[L:, pw_w:], 0.0)
            t_inv = _unit_lower_inverse(e_mat, eye, m_pp_b)
            s_g = sbt[n, g]
            ks_rs = _mm_nt(lhs, s_g.astype(BF16))
            cv = _mm(jnp.concatenate([c_mat, rk_mat], axis=0).astype(BF16), bdiag(vr[:, sl]))
            w_mat = ks_rs[0:L] + cv[0:L]
            u = _mm(t_inv.astype(BF16), bdiag(w_mat))
            y_g = ks_rs[L:] + cv[L:] - _mm(ra_mat.astype(BF16), bdiag(u))
            upd = _mm_tn(jnp.concatenate([vr[:, sl], -u], axis=0).astype(BF16),
                         jnp.concatenate([kh[:, sl], ahh[:, sl]], axis=0).astype(BF16))
            sbt[n, g] = s_g * g_last[:, sl] + jnp.where(m_kk, upd, 0.0)
            ys.append(y_g)
        y = jnp.concatenate(ys, axis=1)
        dlt = y - _block_sums([y], m_kk_b)[0] * inv_hd
        var = _block_sums([dlt * dlt], m_kk_b)[0] * inv_hd
        yn = dlt * lax.rsqrt(var + GN_EPS) * lnw_ref[...] + lnb_ref[...]
        o_ref[n] = ((yn + bonus * vr) * _silu(z_ref[n])).astype(o_ref.dtype)

    @pl.when(i == nsteps - 1)
    def _():
        sfin_ref[...] = sbt[...]


def _rwkv(proj3, shift_prev, s0, mu, w0, w_lora, a0, a_lora, k_k, k_a, r_k, ln_w, ln_b, nb):
    b, t, _ = proj3.shape
    L = min(RWKV_CHUNK, t)
    assert b % nb == 0 and t % L == 0 and L % SUBLANE == 0
    nsteps = t // L
    w3 = 3 * RWKV_W
    padl = lambda x, n: jnp.pad(x, [(0, 0)] * (x.ndim - 1) + [(0, LANE - n)])
    p_r, p_k, p_v = shift_prev[..., 0:RWKV_W], shift_prev[..., RWKV_W:2 * RWKV_W], shift_prev[..., 2 * RWKV_W:w3]
    p_xw = padl(shift_prev[..., w3:w3 + W_LORA], W_LORA)
    p_xa = padl(shift_prev[..., w3 + W_LORA:], A_LORA)
    mu_rkv = mu[0:w3].reshape(1, w3)
    mu_w = padl(mu[w3:w3 + W_LORA], W_LORA).reshape(1, LANE)
    mu_a = padl(mu[w3 + W_LORA:], A_LORA).reshape(1, LANE)
    wl_pad = jnp.pad(w_lora, ((0, LANE - W_LORA), (0, 0))).astype(BF16)
    al_pad = jnp.pad(a_lora, ((0, LANE - A_LORA), (0, 0))).astype(BF16)
    s5 = s0.reshape(b, RWKV_GROUPS, PACK, RWKV_HD, RWKV_HD).transpose(0, 1, 2, 4, 3)
    s0_bd = jnp.einsum("bghvk,hj->bghvjk", s5, jnp.eye(PACK, dtype=s0.dtype))
    s0_bd = s0_bd.reshape(b, RWKV_GROUPS, GROUP, GROUP)
    row1 = lambda x: x.reshape(1, RWKV_W)
    blk = lambda c, w: pl.BlockSpec((nb, L, w), lambda g, i, c=c: (g, i, c))
    prevspec = lambda w: pl.BlockSpec((nb, 1, w), lambda g, i: (g, 0, 0))
    const = lambda shape: pl.BlockSpec(shape, lambda g, i: (0,) * len(shape))
    state_spec = pl.BlockSpec((nb, RWKV_GROUPS, GROUP, GROUP), lambda g, i: (g, 0, 0, 0))
    kern = functools.partial(_rwkv_kernel, nb=nb, chunk=L, nsteps=nsteps)
    o, sfin = pl.pallas_call(
        kern,
        grid=(b // nb, nsteps),
        in_specs=[
            blk(COL_R, SEG), blk(COL_RK, SEG), blk(COL_RV, SEG),
            blk(COL_XW // LANE, LANE), blk(COL_XA // LANE, LANE), blk(COL_ZR, SEG),
            prevspec(SEG), prevspec(SEG), prevspec(SEG), prevspec(LANE), prevspec(LANE),
            state_spec,
            const((1, w3)), const((1, LANE)), const((1, LANE)),
            const((1, RWKV_W)), const((LANE, RWKV_W)), const((1, RWKV_W)), const((LANE, RWKV_W)),
            const((1, RWKV_W)), const((1, RWKV_W)), const((1, RWKV_W)), const((1, RWKV_W)), const((1, RWKV_W)),
        ],
        out_specs=[
            pl.BlockSpec((nb, L, RWKV_W), lambda g, i: (g, i, 0)),
            state_spec,
        ],
        out_shape=[
            jax.ShapeDtypeStruct((b, t, RWKV_W), BF16),
            jax.ShapeDtypeStruct((b, RWKV_GROUPS, GROUP, GROUP), F32),
        ],
        scratch_shapes=[
            pltpu.VMEM((nb, L + SUBLANE, w3), F32),
            pltpu.VMEM((nb, L + SUBLANE, 2 * LANE), F32),
            pltpu.VMEM((nb, RWKV_GROUPS, GROUP, GROUP), F32),
        ],
        compiler_params=pltpu.CompilerParams(
            dimension_semantics=("arbitrary", "arbitrary"), vmem_limit_bytes=VMEM_LIMIT),
        name="rwkv_scan",
    )(proj3, proj3, proj3, proj3, proj3, proj3, p_r, p_k, p_v, p_xw, p_xa, s0_bd,
      mu_rkv, mu_w, mu_a, row1(w0), wl_pad, row1(a0), al_pad, row1(k_k), row1(k_a), row1(r_k),
      row1(ln_w), row1(ln_b))
    s6 = sfin.reshape(b, RWKV_GROUPS, PACK, RWKV_HD, PACK, RWKV_HD)
    s_new = jnp.einsum("bghvjk,hj->bghkv", s6, jnp.eye(PACK, dtype=sfin.dtype))
    return o.reshape(b * t, RWKV_W), s_new.reshape(b, RWKV_HEADS, RWKV_HD, RWKV_HD)


def _merge_kernel(og_ref, or_ref, gg_ref, gr_ref, x_ref, gate_ref, wog_ref, wor_ref, wout_ref, fg_ref, o_ref,
                  *, final_norm):
    m = _sigmoid(gg_ref[...]) * _mm(og_ref[...], wog_ref[...]) \
        + _sigmoid(gr_ref[...]) * _mm(or_ref[...], wor_ref[...])
    out = _mm(m.astype(BF16), wout_ref[...])
    xn = x_ref[...] + gate_ref[...] * out
    if final_norm:
        xn = xn * lax.rsqrt(jnp.mean(xn * xn, axis=-1, keepdims=True) + EPS) * fg_ref[...]
    o_ref[...] = xn


def _merge(o_g, o_r, proj, x2d, gate, w_og, w_or, w_out, final_g, rows_per_mod, final_norm):
    m, d = x2d.shape
    tm = min(m, 256) if rows_per_mod == 1 else min(rows_per_mod, 256)
    if rows_per_mod == 1:
        gate_spec = pl.BlockSpec((tm, d), lambda i: (i, 0))
    else:
        assert rows_per_mod % tm == 0
        per = rows_per_mod // tm
        gate = gate.reshape(-1, 1, d)
        gate_spec = pl.BlockSpec((None, 1, d), lambda i: (i // per, 0, 0))
    whole = lambda shape: pl.BlockSpec(shape, lambda i: (0, 0))
    gcol = COL_GATES // d
    return pl.pallas_call(
        functools.partial(_merge_kernel, final_norm=final_norm),
        grid=(m // tm,),
        in_specs=[
            pl.BlockSpec((tm, GDN_W), lambda i: (i, 0)),
            pl.BlockSpec((tm, RWKV_W), lambda i: (i, 0)),
            pl.BlockSpec((tm, d), lambda i: (i, gcol)),
            pl.BlockSpec((tm, d), lambda i: (i, gcol + 1)),
            pl.BlockSpec((tm, d), lambda i: (i, 0)),
            gate_spec,
            whole((GDN_W, d)), whole((RWKV_W, d)), whole((d, d)), whole((1, d)),
        ],
        out_specs=pl.BlockSpec((tm, d), lambda i: (i, 0)),
        out_shape=jax.ShapeDtypeStruct((m, d), F32),
        compiler_params=pltpu.CompilerParams(
            dimension_semantics=("arbitrary",), vmem_limit_bytes=VMEM_LIMIT),
        name="merge_out",
    )(o_g, o_r, proj, proj, x2d, gate, w_og, w_or, w_out, final_g.reshape(1, d))


def _pad_in_weight(w_in):
    d = w_in.shape[0]
    o_zg = 3 * GDN_W
    o_b = o_zg + GDN_W
    o_rw = o_b + 2 * GDN_HEADS
    o_xw = o_rw + 3 * RWKV_W
    o_xa = o_xw + W_LORA
    o_zr = o_xa + A_LORA
    o_br = o_zr + RWKV_W
    z = lambda n: jnp.zeros((d, n), w_in.dtype)
    cols = [
        w_in[:, 0:o_zg], w_in[:, o_zg:o_b],
        w_in[:, o_rw:o_xw], w_in[:, o_zr:o_br],
        w_in[:, o_br:o_br + 2 * D_MODEL],
        w_in[:, o_b:o_rw], z(LANE - 2 * GDN_HEADS),
        w_in[:, o_xw:o_xa], z(LANE - W_LORA),
        w_in[:, o_xa:o_zr], z(LANE - A_LORA),
    ]
    used = COL_XA + LANE
    cols.append(z(N_PROJ - used))
    return jnp.concatenate(cols, axis=1).astype(BF16)


def _layer(x2d, batch, seq, scale, shift, gate, rows_per_mod, conv_prev, s_gdn, shift_prev, s_rwkv, p, final_g,
           final_norm, scan_nb):
    proj = _in_proj(x2d, scale, shift, p["norm_g"], p["w_in_pad"], rows_per_mod)
    proj3 = proj.reshape(batch, seq, N_PROJ)
    o_g, s_gdn_new = _gdn(proj3, conv_prev, s_gdn, p["conv_w"], p["a_log"], p["dt_bias"], p["onorm_g"], scan_nb)
    o_r, s_rwkv_new = _rwkv(proj3, shift_prev, s_rwkv, p["mu"], p["w0"], p["w_lora"], p["a0"], p["a_lora"],
                            p["k_k"], p["k_a"], p["r_k"], p["ln_w"], p["ln_b"], scan_nb)
    x_new = _merge(o_g, o_r, proj, x2d, gate, p["w_o_gdn"], p["w_o_rwkv"], p["w_out"], final_g, rows_per_mod,
                   final_norm)
    conv_new = proj3[:, seq - (CONV_W - 1):, 0:3 * GDN_W]
    last = proj3[:, seq - 1:, :]
    shift_new = jnp.concatenate(
        [last[..., COL_R * SEG:COL_R * SEG + 3 * RWKV_W], last[..., COL_XW:COL_XW + W_LORA],
         last[..., COL_XA:COL_XA + A_LORA]], axis=-1)
    return x_new, conv_new, s_gdn_new, shift_new, s_rwkv_new


def _forward(x_prompt, x_sample, c_prompt, c_sample, cache_gdn_conv, state_gdn, cache_rwkv_shift, state_rwkv,
             ada_w, ada_b, norm_g, w_in, gdn_conv_w, gdn_a_log, gdn_dt_bias, gdn_out_norm_g,
             rwkv_mu, rwkv_w0, rwkv_w_lora, rwkv_a0, rwkv_a_lora, rwkv_k_k, rwkv_k_a, rwkv_r_k,
             rwkv_ln_w, rwkv_ln_b, w_o_gdn, w_o_rwkv, w_out, final_norm_g):
    depth = ada_w.shape[0]
    bp, tp, d = x_prompt.shape
    bs, ts, _ = x_sample.shape
    assert CONV_W - 1 <= min(tp, ts)
    c_all = jnp.concatenate([c_prompt, c_sample], axis=0)
    rows = -(-(bp + bs) // SUBLANE) * SUBLANE
    c_all = jnp.pad(c_all, ((0, rows - (bp + bs)), (0, 0)))
    mod = _ada_mod(c_all, ada_w, ada_b)
    xp = x_prompt.reshape(bp * tp, d)
    xs = x_sample.reshape(bs * ts, d)
    outs = [[] for _ in range(8)]
    for l in range(depth):
        p = dict(norm_g=norm_g[l], w_in_pad=_pad_in_weight(w_in[l]), conv_w=gdn_conv_w[l], a_log=gdn_a_log[l],
                 dt_bias=gdn_dt_bias[l], onorm_g=gdn_out_norm_g[l], mu=rwkv_mu[l], w0=rwkv_w0[l],
                 w_lora=rwkv_w_lora[l], a0=rwkv_a0[l], a_lora=rwkv_a_lora[l], k_k=rwkv_k_k[l], k_a=rwkv_k_a[l],
                 r_k=rwkv_r_k[l].reshape(-1), ln_w=rwkv_ln_w[l], ln_b=rwkv_ln_b[l],
                 w_o_gdn=w_o_gdn[l].astype(BF16), w_o_rwkv=w_o_rwkv[l].astype(BF16), w_out=w_out[l].astype(BF16))
        last = l == depth - 1
        m_p = mod[l, 0:bp]
        m_s = jnp.repeat(mod[l, bp:bp + bs], ts, axis=0)
        sh_p, sc_p, gt_p = m_p[:, 0:d], m_p[:, d:2 * d], m_p[:, 2 * d:]
        sh_s, sc_s, gt_s = m_s[:, 0:d], m_s[:, d:2 * d], m_s[:, 2 * d:]
        zeros = lambda *s: jnp.zeros(s, F32)
        xp, c1, g1, h1, r1 = _layer(
            xp, bp, tp, sc_p, sh_p, gt_p, tp,
            zeros(bp, CONV_W - 1, 3 * GDN_W), zeros(bp, GDN_HEADS, GDN_DK, GDN_DV),
            zeros(bp, 1, 3 * RWKV_W + W_LORA + A_LORA), zeros(bp, RWKV_HEADS, RWKV_HD, RWKV_HD),
            p, final_norm_g, last, scan_nb=bp)
        xs, c2, g2, h2, r2 = _layer(
            xs, bs, ts, sc_s, sh_s, gt_s, 1,
            cache_gdn_conv[l], state_gdn[l], cache_rwkv_shift[l], state_rwkv[l],
            p, final_norm_g, last, scan_nb=min(bs, 2))
        for lst, val in zip(outs, (c1, g1, h1, r1, c2, g2, h2, r2)):
            lst.append(val)
    stk = [jnp.stack(v) for v in outs]
    return (xp.reshape(bp, tp, d), xs.reshape(bs, ts, d), *stk)


def kernel(x_prompt, x_sample, c_prompt, c_sample, cache_gdn_conv, state_gdn, cache_rwkv_shift, state_rwkv, ada_w, ada_b, norm_g, w_in, gdn_conv_w, gdn_a_log, gdn_dt_bias, gdn_out_norm_g, rwkv_mu, rwkv_w0, rwkv_w_lora, rwkv_a0, rwkv_a_lora, rwkv_k_k, rwkv_k_a, rwkv_r_k, rwkv_ln_w, rwkv_ln_b, w_o_gdn, w_o_rwkv, w_out, final_norm_g):
    return _forward(x_prompt, x_sample, c_prompt, c_sample, cache_gdn_conv, state_gdn, cache_rwkv_shift,
                    state_rwkv, ada_w, ada_b, norm_g, w_in, gdn_conv_w, gdn_a_log, gdn_dt_bias, gdn_out_norm_g,
                    rwkv_mu, rwkv_w0, rwkv_w_lora, rwkv_a0, rwkv_a_lora, rwkv_k_k, rwkv_k_a, rwkv_r_k,
                    rwkv_ln_w, rwkv_ln_b, w_o_gdn, w_o_rwkv, w_out, final_norm_g)
```

```python
import functools
import math

import jax
import jax.numpy as jnp
from jax import lax
from jax.experimental import pallas as pl
from jax.experimental.pallas import tpu as pltpu

F32 = jnp.float32
BF16 = jnp.bfloat16

D_MODEL = 2048
GDN_HEADS = 8
GDN_DK = 128
GDN_DV = 128
GDN_W = GDN_HEADS * GDN_DK
GDN_CHUNK = 64
CONV_W = 4
RWKV_HEADS = 16
RWKV_HD = 64
RWKV_W = RWKV_HEADS * RWKV_HD
RWKV_CHUNK = 64
W_LORA = 96
A_LORA = 96
EPS = 1e-6
GN_EPS = 64e-5
W_DECAY_OFFSET = 0.5

LANE = 128
SUBLANE = 8
SEG = 1024
COL_Q, COL_K, COL_V, COL_ZG, COL_R, COL_RK, COL_RV, COL_ZR = 0, 1, 2, 3, 4, 5, 6, 7
COL_GATES = 8 * SEG
COL_BA = 12 * SEG
COL_XW = COL_BA + LANE
COL_XA = COL_XW + LANE
N_PROJ = 13 * SEG
GROUP = 256
PACK = 4
RWKV_GROUPS = RWKV_W // GROUP
GDN_PAIRS = GDN_W // GROUP
VMEM_LIMIT = 56 * 1024 * 1024


def _mm(a, b):
    return jnp.dot(a, b, preferred_element_type=F32)


def _mm_nt(a, b):
    return lax.dot_general(a, b, (((1,), (1,)), ((), ())), preferred_element_type=F32)


def _mm_tn(a, b):
    return lax.dot_general(a, b, (((0,), (0,)), ((), ())), preferred_element_type=F32)


def _split_bf16(x):
    hi = x.astype(BF16)
    return hi, (x - hi.astype(F32)).astype(BF16)


def _split3_bf16(x):
    hi = x.astype(BF16)
    mid, lo = _split_bf16(x - hi.astype(F32))
    return hi, mid, lo


def _sigmoid(x):
    return 1.0 / (1.0 + jnp.exp(-x))


def _silu(x):
    return x * _sigmoid(x)


def _softplus(x):
    return jnp.maximum(x, 0.0) + jnp.log1p(jnp.exp(-jnp.abs(x)))


def _iota(shape, dim):
    return lax.broadcasted_iota(jnp.int32, shape, dim)


def _bits(n):
    assert n & (n - 1) == 0
    return n.bit_length() - 1


def _mod_kernel(c_ref, w_ref, b_ref, o_ref):
    o_ref[...] = _mm(c_ref[...], w_ref[...]) + b_ref[...]


def _ada_mod(c_all, ada_w, ada_b):
    depth, d, n3 = ada_w.shape
    rows = c_all.shape[0]
    tn = 512
    return pl.pallas_call(
        _mod_kernel,
        grid=(depth, n3 // tn),
        in_specs=[
            pl.BlockSpec((rows, d), lambda l, j: (0, 0)),
            pl.BlockSpec((None, d, tn), lambda l, j: (l, 0, j)),
            pl.BlockSpec((None, 1, tn), lambda l, j: (l, 0, j)),
        ],
        out_specs=pl.BlockSpec((None, rows, tn), lambda l, j: (l, 0, j)),
        out_shape=jax.ShapeDtypeStruct((depth, rows, n3), F32),
        compiler_params=pltpu.CompilerParams(
            dimension_semantics=("arbitrary", "arbitrary"), vmem_limit_bytes=VMEM_LIMIT),
        name="ada_mod",
    )(c_all, ada_w, ada_b.reshape(depth, 1, n3))


def _in_proj_kernel(x_ref, sc_ref, sh_ref, g_ref, w_ref, o_ref, h_ref):
    @pl.when(pl.program_id(1) == 0)
    def _():
        x = x_ref[...]
        ms = jnp.mean(x * x, axis=-1, keepdims=True)
        h = x * lax.rsqrt(ms + EPS) * g_ref[...] * (1.0 + sc_ref[...]) + sh_ref[...]
        h_ref[...] = h.astype(BF16)

    o_ref[...] = _mm(h_ref[...], w_ref[...])


def _in_proj(x2d, scale, shift, norm_g, w_pad, rows_per_mod):
    m, d = x2d.shape
    tm = min(m, 512) if rows_per_mod == 1 else min(rows_per_mod, 1024)
    tn = SEG
    if rows_per_mod == 1:
        mod_spec = pl.BlockSpec((tm, d), lambda i, j: (i, 0))
    else:
        assert rows_per_mod % tm == 0
        per = rows_per_mod // tm
        scale = scale.reshape(-1, 1, d)
        shift = shift.reshape(-1, 1, d)
        mod_spec = pl.BlockSpec((None, 1, d), lambda i, j: (i // per, 0, 0))
    return pl.pallas_call(
        _in_proj_kernel,
        grid=(m // tm, N_PROJ // tn),
        in_specs=[
            pl.BlockSpec((tm, d), lambda i, j: (i, 0)),
            mod_spec, mod_spec,
            pl.BlockSpec((1, d), lambda i, j: (0, 0)),
            pl.BlockSpec((d, tn), lambda i, j: (0, j)),
        ],
        out_specs=pl.BlockSpec((tm, tn), lambda i, j: (i, j)),
        out_shape=jax.ShapeDtypeStruct((m, N_PROJ), F32),
        scratch_shapes=[pltpu.VMEM((tm, d), BF16)],
        compiler_params=pltpu.CompilerParams(
            dimension_semantics=("arbitrary", "arbitrary"), vmem_limit_bytes=VMEM_LIMIT),
        name="in_proj",
    )(x2d, scale, shift, norm_g.reshape(1, d), w_pad)


def _tile_rows(x, reps):
    return jnp.concatenate([x] * reps, axis=0)


def _head_mask(rows, row_bits, cols, col_bits):
    return (_iota((rows, cols), 0) >> row_bits) == (_iota((rows, cols), 1) >> col_bits)


def _block_sums(xs, gmat_b):
    t, width = xs[0].shape
    ng = width // GROUP
    parts = []
    for x in xs:
        for piece in _split_bf16(x):
            parts += [piece[:, g * GROUP:(g + 1) * GROUP] for g in range(ng)]
    res = _mm(jnp.concatenate(parts, axis=0), gmat_b)
    outs = []
    for i in range(len(xs)):
        blk = lambda p, g: res[((2 * i + p) * ng + g) * t:((2 * i + p) * ng + g + 1) * t]
        outs.append(jnp.concatenate([blk(0, g) + blk(1, g) for g in range(ng)], axis=1))
    return outs


def _cumsum_rows(x, tri_b):
    w = x.shape[1]
    res = _mm(tri_b, jnp.concatenate(_split3_bf16(x), axis=1))
    return res[:, 0:w] + (res[:, w:2 * w] + res[:, 2 * w:])


def _unit_lower_inverses(e_mats, eye, m_pp_b):
    L, pw_w = e_mats[0].shape
    steps = _bits(L)
    pws = [-e for e in e_mats]
    t_invs = [eye + pw for pw in pws]
    for s in range(steps):
        first, last = s == 0, s == steps - 1
        if first and last:
            break
        for c in range(len(e_mats)):
            parts = ([] if first else [t_invs[c]]) + ([] if last else [pws[c]])
            bd_hi, bd_lo = [], []
            for x in parts:
                x_hi, x_lo = _split_bf16(x)
                bd_hi.append(_tile_rows(x_hi, PACK) * m_pp_b)
                bd_lo.append(_tile_rows(x_lo, PACK) * m_pp_b)
            bd_hi = jnp.concatenate(bd_hi, axis=1)
            bd_lo = jnp.concatenate(bd_lo, axis=1)
            p_hi, p_lo = _split_bf16(pws[c])
            res = _mm(p_hi, bd_hi) + (_mm(p_hi, bd_lo) + _mm(p_lo, bd_hi))
            if not first:
                t_invs[c] = t_invs[c] + res[:, 0:pw_w]
            if not last:
                pws[c] = res[:, res.shape[1] - pw_w:]
    return t_invs


def _gdn_kernel(q_ref, k_ref, v_ref, z_ref, ba_ref, cw_ref, prev_ref, s0_ref, alog_ref, dtb_ref, on_ref,
                o_ref, sfin_ref, ubuf, sb, *, nb, chunk, nsteps):
    L = chunk
    i = pl.program_id(1)
    l_bits = _bits(L)
    dk_bits = _bits(GDN_DK)
    pw_w = PACK * L
    hw = PACK * GDN_DK
    n_grp = GDN_HEADS // PACK

    m_pd_b = _head_mask(pw_w, l_bits, hw, dk_bits).astype(BF16)
    m_pp_b = _head_mask(pw_w, l_bits, pw_w, l_bits).astype(BF16)
    m_dd = _head_mask(GROUP, dk_bits, GROUP, dk_bits)
    m_dd_b = m_dd.astype(BF16)
    t_row = _iota((L, pw_w), 0)
    j_lane = _iota((L, pw_w), 1) & (L - 1)
    strict = j_lane < t_row
    incl = j_lane <= t_row
    eye = (j_lane == t_row).astype(F32)
    tri_b = (_iota((L, L), 0) >= _iota((L, L), 1)).astype(BF16)
    ones_b = jnp.ones((L, L), BF16)
    lane = _iota((L, LANE), 1)
    x_rows = _iota((LANE, GDN_HEADS * L + GDN_W), 0) & (GDN_HEADS - 1)
    x_cols = _iota((LANE, GDN_HEADS * L + GDN_W), 1)
    x_head = jnp.where(x_cols < GDN_HEADS * L, x_cols >> l_bits, (x_cols - GDN_HEADS * L) >> dk_bits)
    expand_b = ((x_rows == x_head) & (_iota((LANE, GDN_HEADS * L + GDN_W), 0) < 2 * GDN_HEADS)).astype(BF16)

    def bdiag(x):
        return _tile_rows(x.astype(BF16), PACK) * m_pd_b

    @pl.when(i == 0)
    def _():
        sb[...] = s0_ref[...]
        for n in range(nb):
            ubuf[n, 0:SUBLANE, :] = prev_ref[n]

    base = SUBLANE - (CONV_W - 1)
    pre = []
    for n in range(nb):
        ubuf[n, SUBLANE:SUBLANE + L, 0:GDN_W] = q_ref[n]
        ubuf[n, SUBLANE:SUBLANE + L, GDN_W:2 * GDN_W] = k_ref[n]
        ubuf[n, SUBLANE:SUBLANE + L, 2 * GDN_W:3 * GDN_W] = v_ref[n]
        y = ubuf[n, base:base + L, :] * cw_ref[0:1, :]
        for j in range(1, CONV_W):
            y = y + ubuf[n, base + j:base + j + L, :] * cw_ref[j:j + 1, :]
        ubuf[n, 0:SUBLANE, :] = ubuf[n, L:L + SUBLANE, :]
        act = _silu(y)
        q = act[:, 0:GDN_W]
        k = act[:, GDN_W:2 * GDN_W]
        v = act[:, 2 * GDN_W:]
        ssq, ssk = _block_sums([q * q, k * k], m_dd_b)
        q = q * lax.rsqrt(ssq + EPS) * (GDN_DK ** -0.5)
        k = k * lax.rsqrt(ssk + EPS)

        ba = ba_ref[n]
        beta_c = jnp.where(lane < GDN_HEADS, _sigmoid(ba), 0.0)
        g_c = -jnp.exp(alog_ref[...]) * _softplus(ba + dtb_ref[...])
        gc_c = jnp.where((lane >= GDN_HEADS) & (lane < 2 * GDN_HEADS), _cumsum_rows(g_c, tri_b), 0.0)
        pieces = jnp.concatenate(list(_split_bf16(beta_c)) + list(_split3_bf16(gc_c)), axis=0)
        ex = _mm(pieces, expand_b)
        beta_x = ex[0:L] + ex[L:2 * L]
        gc_x = ex[2 * L:3 * L] + (ex[3 * L:4 * L] + ex[4 * L:])
        beta_p, beta_w = beta_x[:, 0:GDN_HEADS * L], beta_x[:, GDN_HEADS * L:]
        gc_p, gc_w = gc_x[:, 0:GDN_HEADS * L], gc_x[:, GDN_HEADS * L:]
        eye8 = jnp.concatenate([eye] * n_grp, axis=1)
        incl8 = jnp.concatenate([incl] * n_grp, axis=1)
        rw = _mm(ones_b, jnp.concatenate(_split3_bf16(gc_p * eye8), axis=1))
        pw8 = GDN_HEADS * L
        gc_row = rw[:, 0:pw8] + (rw[:, pw8:2 * pw8] + rw[:, 2 * pw8:])
        dm = jnp.where(incl8, jnp.exp(jnp.where(incl8, gc_p - gc_row, 0.0)), 0.0)
        gl_w = gc_w[L - 1:L, :]
        eg_w = jnp.exp(gc_w)
        qd = q * eg_w
        kd = k * jnp.exp(gl_w - gc_w)
        bv = beta_w * v
        bek = beta_w * eg_w * k
        pre.append(dict(q=q, k=k, beta_p=beta_p, dm=dm, qd=qd, kd=kd, bv=bv, bek=bek, egl_w=jnp.exp(gl_w)))

    chains = [(n, g) for g in range(n_grp) for n in range(nb)]
    wsl = lambda g: slice(g * hw, (g + 1) * hw)
    psl = lambda g: slice(g * pw_w, (g + 1) * pw_w)
    ppg = hw // GROUP
    a_mat, qkd = {}, {}
    for c in chains:
        p, sw, sp = pre[c[0]], wsl(c[1]), psl(c[1])
        prod = _mm_nt(jnp.concatenate([p["k"][:, sw], p["q"][:, sw]], axis=0).astype(BF16), bdiag(p["k"][:, sw]))
        a_mat[c] = jnp.where(strict, p["beta_p"][:, sp] * prod[0:L] * p["dm"][:, sp], 0.0)
        qkd[c] = prod[L:] * p["dm"][:, sp]
    t_inv = dict(zip(chains, _unit_lower_inverses([a_mat[c] for c in chains], eye, m_pp_b)))
    sol = {}
    for c in chains:
        p, sw = pre[c[0]], wsl(c[1])
        sol[c] = _mm(t_inv[c].astype(BF16), jnp.concatenate([bdiag(p["bv"][:, sw]), bdiag(p["bek"][:, sw])], axis=1))
    pairs = [(c, j) for c in chains for j in range(ppg)]
    s_p, r2, u_p = {}, {}, {}
    for c, j in pairs:
        pair = c[1] * ppg + j
        sg = slice(pair * GROUP, (pair + 1) * GROUP)
        s_p[c, j] = sb[c[0], pair]
        r2[c, j] = _mm(jnp.concatenate([sol[c][:, hw + j * GROUP:hw + (j + 1) * GROUP], pre[c[0]]["qd"][:, sg]],
                                       axis=0).astype(BF16), s_p[c, j].astype(BF16))
    for c, j in pairs:
        pair = c[1] * ppg + j
        sg = slice(pair * GROUP, (pair + 1) * GROUP)
        u_p[c, j] = sol[c][:, j * GROUP:(j + 1) * GROUP] - r2[c, j][0:L]
        upd = _mm_tn(pre[c[0]]["kd"][:, sg].astype(BF16), u_p[c, j].astype(BF16))
        sb[c[0], pair] = s_p[c, j] * pre[c[0]]["egl_w"][:, sg] + jnp.where(m_dd, upd, 0.0)
    o_g = {}
    for c in chains:
        u = jnp.concatenate([u_p[c, j] for j in range(ppg)], axis=1)
        qs = jnp.concatenate([r2[c, j][L:] for j in range(ppg)], axis=1)
        o_g[c] = qs + _mm(qkd[c].astype(BF16), bdiag(u))

    for n in range(nb):
        o = jnp.concatenate([o_g[(n, g)] for g in range(n_grp)], axis=1)
        sso, = _block_sums([o * o], m_dd_b)
        on = o * lax.rsqrt(sso * (1.0 / GDN_DV) + EPS) * on_ref[...]
        o_ref[n] = (on * _silu(z_ref[n])).astype(o_ref.dtype)

    @pl.when(i == nsteps - 1)
    def _():
        sfin_ref[...] = sb[...]


def _gdn(proj3, conv_prev, s0, conv_w, a_log, dt_bias, onorm_g, nb):
    b, t, _ = proj3.shape
    L = min(GDN_CHUNK, t)
    assert b % nb == 0 and t % L == 0 and L % SUBLANE == 0
    nsteps = t // L
    prev_pad = jnp.pad(conv_prev, ((0, 0), (SUBLANE - (CONV_W - 1), 0), (0, 0)))
    lane_pad = (GDN_HEADS, LANE - 2 * GDN_HEADS)
    alog_row = jnp.pad(a_log, lane_pad).reshape(1, LANE)
    dtb_row = jnp.pad(dt_bias, lane_pad).reshape(1, LANE)
    on_row = jnp.tile(onorm_g, GDN_HEADS).reshape(1, GDN_W)
    hp = GROUP // GDN_DK
    s5 = s0.reshape(b, GDN_PAIRS, hp, GDN_DK, GDN_DV)
    s0_bd = jnp.einsum("bphkv,hj->bphkjv", s5, jnp.eye(hp, dtype=s0.dtype)).reshape(b, GDN_PAIRS, GROUP, GROUP)
    blk = lambda c, w: pl.BlockSpec((nb, L, w), lambda g, i, c=c: (g, i, c))
    const = lambda shape: pl.BlockSpec(shape, lambda g, i: (0,) * len(shape))
    state_spec = pl.BlockSpec((nb, GDN_PAIRS, GROUP, GROUP), lambda g, i: (g, 0, 0, 0))
    kern = functools.partial(_gdn_kernel, nb=nb, chunk=L, nsteps=nsteps)
    o, sfin = pl.pallas_call(
        kern,
        grid=(b // nb, nsteps),
        in_specs=[
            blk(COL_Q, SEG), blk(COL_K, SEG), blk(COL_V, SEG), blk(COL_ZG, SEG), blk(COL_BA // LANE, LANE),
            const((CONV_W, 3 * GDN_W)),
            pl.BlockSpec((nb, SUBLANE, 3 * GDN_W), lambda g, i: (g, 0, 0)),
            state_spec,
            const((1, LANE)), const((1, LANE)), const((1, GDN_W)),
        ],
        out_specs=[
            pl.BlockSpec((nb, L, GDN_W), lambda g, i: (g, i, 0)),
            state_spec,
        ],
        out_shape=[
            jax.ShapeDtypeStruct((b, t, GDN_W), BF16),
            jax.ShapeDtypeStruct((b, GDN_PAIRS, GROUP, GROUP), F32),
        ],
        scratch_shapes=[
            pltpu.VMEM((nb, L + SUBLANE, 3 * GDN_W), F32),
            pltpu.VMEM((nb, GDN_PAIRS, GROUP, GROUP), F32),
        ],
        compiler_params=pltpu.CompilerParams(
            dimension_semantics=("arbitrary", "arbitrary"), vmem_limit_bytes=VMEM_LIMIT),
        name="gdn_scan",
    )(proj3, proj3, proj3, proj3, proj3, conv_w, prev_pad, s0_bd, alog_row, dtb_row, on_row)
    s6 = sfin.reshape(b, GDN_PAIRS, hp, GDN_DK, hp, GDN_DV)
    s_new = jnp.einsum("bphkjv,hj->bphkv", s6, jnp.eye(hp, dtype=sfin.dtype))
    return o.reshape(b * t, GDN_W), s_new.reshape(b, GDN_HEADS, GDN_DK, GDN_DV)


def _rwkv_kernel(r_ref, k_ref, v_ref, xw_ref, xa_ref, z_ref, pr_ref, pk_ref, pv_ref, pxw_ref, pxa_ref, s0_ref,
                 mu_ref, muw_ref, mua_ref, w0_ref, wl_ref, a0_ref, al_ref, kk_ref, ka_ref, rk_ref,
                 lnw_ref, lnb_ref,
                 o_ref, sfin_ref,
                 sbuf, lbuf, sbt, *, nb, chunk, nsteps):
    i = pl.program_id(1)
    L = chunk
    pw_w = PACK * L
    l_bits = _bits(L)
    hd_bits = _bits(RWKV_HD)

    m_pk_b = _head_mask(pw_w, l_bits, GROUP, hd_bits).astype(BF16)
    m_pp_b = _head_mask(pw_w, l_bits, pw_w, l_bits).astype(BF16)
    m_kk = _head_mask(GROUP, hd_bits, GROUP, hd_bits)
    m_kk_b = m_kk.astype(BF16)
    t_row = _iota((L, pw_w), 0)
    j_lane = _iota((L, pw_w), 1) & (L - 1)
    strict = j_lane < t_row
    incl = j_lane <= t_row
    eye = (j_lane == t_row).astype(F32)
    tri_b = (_iota((L, L), 0) >= _iota((L, L), 1)).astype(BF16)

    def bdiag(x):
        return _tile_rows(x.astype(BF16), PACK) * m_pk_b

    @pl.when(i == 0)
    def _():
        sbt[...] = s0_ref[...]
        for n in range(nb):
            sbuf[n, SUBLANE - 1:SUBLANE, 0:SEG] = pr_ref[n]
            sbuf[n, SUBLANE - 1:SUBLANE, SEG:2 * SEG] = pk_ref[n]
            sbuf[n, SUBLANE - 1:SUBLANE, 2 * SEG:3 * SEG] = pv_ref[n]
            lbuf[n, SUBLANE - 1:SUBLANE, 0:LANE] = pxw_ref[n]
            lbuf[n, SUBLANE - 1:SUBLANE, LANE:2 * LANE] = pxa_ref[n]

    inv_hd = 1.0 / RWKV_HD
    pre = []
    for n in range(nb):
        sbuf[n, SUBLANE:SUBLANE + L, 0:SEG] = r_ref[n]
        sbuf[n, SUBLANE:SUBLANE + L, SEG:2 * SEG] = k_ref[n]
        sbuf[n, SUBLANE:SUBLANE + L, 2 * SEG:3 * SEG] = v_ref[n]
        lbuf[n, SUBLANE:SUBLANE + L, 0:LANE] = xw_ref[n]
        lbuf[n, SUBLANE:SUBLANE + L, LANE:2 * LANE] = xa_ref[n]
        cur = sbuf[n, SUBLANE:SUBLANE + L, :]
        prv = sbuf[n, SUBLANE - 1:SUBLANE - 1 + L, :]
        rkv = cur + (prv - cur) * mu_ref[...]
        curl = lbuf[n, SUBLANE:SUBLANE + L, :]
        prvl = lbuf[n, SUBLANE - 1:SUBLANE - 1 + L, :]
        xw = curl[:, 0:LANE] + (prvl[:, 0:LANE] - curl[:, 0:LANE]) * muw_ref[...]
        xa = curl[:, LANE:] + (prvl[:, LANE:] - curl[:, LANE:]) * mua_ref[...]
        sbuf[n, 0:SUBLANE, :] = sbuf[n, L:L + SUBLANE, :]
        lbuf[n, 0:SUBLANE, :] = lbuf[n, L:L + SUBLANE, :]
        r = rkv[:, 0:SEG]
        kr = rkv[:, SEG:2 * SEG]
        vr = rkv[:, 2 * SEG:3 * SEG]
        wl = w0_ref[...] + _mm(jnp.tanh(xw).astype(BF16), wl_ref[...])
        logw = -math.exp(-W_DECAY_OFFSET) * _sigmoid(wl)
        a = _sigmoid(a0_ref[...] + _mm(xa.astype(BF16), al_ref[...]))
        kkr = kr * kk_ref[...]
        kt = kr * (1.0 + (a - 1.0) * ka_ref[...])
        ss, bonus = _block_sums([kkr * kkr, r * kt * rk_ref[...]], m_kk_b)
        kk = kkr * lax.rsqrt(ss + EPS)
        ah = a * kk
        cum = _cumsum_rows(logw, tri_b)
        c_last = cum[L - 1:L, :]
        e_neg = jnp.exp(-cum)
        e_last = jnp.exp(c_last - cum)
        kx = kk * jnp.exp(cum - logw)
        rx = r * jnp.exp(cum)
        kb = kt * e_neg
        ab = ah * e_neg
        kh = kt * e_last
        ahh = ah * e_last
        pre.append(dict(vr=vr, bonus=bonus, kx=kx, rx=rx, kb=kb, ab=ab, kh=kh, ahh=ahh, g_last=jnp.exp(c_last)))

    chains = [(n, g) for g in range(RWKV_GROUPS) for n in range(nb)]
    gsl = lambda g: slice(g * GROUP, (g + 1) * GROUP)
    lhs, c_mat, e_mat, rk_mat, ra_mat = {}, {}, {}, {}, {}
    for c in chains:
        p, sl = pre[c[0]], gsl(c[1])
        lhs[c] = jnp.concatenate([p["kx"][:, sl], p["rx"][:, sl]], axis=0).astype(BF16)
        rhs_nt = jnp.concatenate([bdiag(p["kb"][:, sl]), bdiag(p["ab"][:, sl])], axis=0)
        prod = _mm_nt(lhs[c], rhs_nt)
        c_mat[c] = jnp.where(strict, prod[0:L, 0:pw_w], 0.0)
        e_mat[c] = jnp.where(strict, prod[0:L, pw_w:], 0.0)
        rk_mat[c] = jnp.where(incl, prod[L:, 0:pw_w], 0.0)
        ra_mat[c] = jnp.where(incl, prod[L:, pw_w:], 0.0)
    t_inv = dict(zip(chains, _unit_lower_inverses([e_mat[c] for c in chains], eye, m_pp_b)))
    s_g, ks_rs, cv, u = {}, {}, {}, {}
    for c in chains:
        s_g[c] = sbt[c[0], c[1]]
        ks_rs[c] = _mm_nt(lhs[c], s_g[c].astype(BF16))
        cv[c] = _mm(jnp.concatenate([c_mat[c], rk_mat[c]], axis=0).astype(BF16),
                    bdiag(pre[c[0]]["vr"][:, gsl(c[1])]))
    for c in chains:
        u[c] = _mm(t_inv[c].astype(BF16), bdiag(ks_rs[c][0:L] + cv[c][0:L]))
    y_g = {}
    for c in chains:
        p, sl = pre[c[0]], gsl(c[1])
        y_g[c] = ks_rs[c][L:] + cv[c][L:] - _mm(ra_mat[c].astype(BF16), bdiag(u[c]))
        upd = _mm_tn(jnp.concatenate([p["vr"][:, sl], -u[c]], axis=0).astype(BF16),
                     jnp.concatenate([p["kh"][:, sl], p["ahh"][:, sl]], axis=0).astype(BF16))
        sbt[c[0], c[1]] = s_g[c] * p["g_last"][:, sl] + jnp.where(m_kk, upd, 0.0)

    for n in range(nb):
        y = jnp.concatenate([y_g[(n, g)] for g in range(RWKV_GROUPS)], axis=1)
        dlt = y - _block_sums([y], m_kk_b)[0] * inv_hd
        var = _block_sums([dlt * dlt], m_kk_b)[0] * inv_hd
        yn = dlt * lax.rsqrt(var + GN_EPS) * lnw_ref[...] + lnb_ref[...]
        o_ref[n] = ((yn + pre[n]["bonus"] * pre[n]["vr"]) * _silu(z_ref[n])).astype(o_ref.dtype)

    @pl.when(i == nsteps - 1)
    def _():
        sfin_ref[...] = sbt[...]


def _rwkv(proj3, shift_prev, s0, mu, w0, w_lora, a0, a_lora, k_k, k_a, r_k, ln_w, ln_b, nb):
    b, t, _ = proj3.shape
    L = min(RWKV_CHUNK, t)
    assert b % nb == 0 and t % L == 0 and L % SUBLANE == 0
    nsteps = t // L
    w3 = 3 * RWKV_W
    padl = lambda x, n: jnp.pad(x, [(0, 0)] * (x.ndim - 1) + [(0, LANE - n)])
    p_r, p_k, p_v = shift_prev[..., 0:RWKV_W], shift_prev[..., RWKV_W:2 * RWKV_W], shift_prev[..., 2 * RWKV_W:w3]
    p_xw = padl(shift_prev[..., w3:w3 + W_LORA], W_LORA)
    p_xa = padl(shift_prev[..., w3 + W_LORA:], A_LORA)
    mu_rkv = mu[0:w3].reshape(1, w3)
    mu_w = padl(mu[w3:w3 + W_LORA], W_LORA).reshape(1, LANE)
    mu_a = padl(mu[w3 + W_LORA:], A_LORA).reshape(1, LANE)
    wl_pad = jnp.pad(w_lora, ((0, LANE - W_LORA), (0, 0))).astype(BF16)
    al_pad = jnp.pad(a_lora, ((0, LANE - A_LORA), (0, 0))).astype(BF16)
    s5 = s0.reshape(b, RWKV_GROUPS, PACK, RWKV_HD, RWKV_HD).transpose(0, 1, 2, 4, 3)
    s0_bd = jnp.einsum("bghvk,hj->bghvjk", s5, jnp.eye(PACK, dtype=s0.dtype))
    s0_bd = s0_bd.reshape(b, RWKV_GROUPS, GROUP, GROUP)
    row1 = lambda x: x.reshape(1, RWKV_W)
    blk = lambda c, w: pl.BlockSpec((nb, L, w), lambda g, i, c=c: (g, i, c))
    prevspec = lambda w: pl.BlockSpec((nb, 1, w), lambda g, i: (g, 0, 0))
    const = lambda shape: pl.BlockSpec(shape, lambda g, i: (0,) * len(shape))
    state_spec = pl.BlockSpec((nb, RWKV_GROUPS, GROUP, GROUP), lambda g, i: (g, 0, 0, 0))
    kern = functools.partial(_rwkv_kernel, nb=nb, chunk=L, nsteps=nsteps)
    o, sfin = pl.pallas_call(
        kern,
        grid=(b // nb, nsteps),
        in_specs=[
            blk(COL_R, SEG), blk(COL_RK, SEG), blk(COL_RV, SEG),
            blk(COL_XW // LANE, LANE), blk(COL_XA // LANE, LANE), blk(COL_ZR, SEG),
            prevspec(SEG), prevspec(SEG), prevspec(SEG), prevspec(LANE), prevspec(LANE),
            state_spec,
            const((1, w3)), const((1, LANE)), const((1, LANE)),
            const((1, RWKV_W)), const((LANE, RWKV_W)), const((1, RWKV_W)), const((LANE, RWKV_W)),
            const((1, RWKV_W)), const((1, RWKV_W)), const((1, RWKV_W)), const((1, RWKV_W)), const((1, RWKV_W)),
        ],
        out_specs=[
            pl.BlockSpec((nb, L, RWKV_W), lambda g, i: (g, i, 0)),
            state_spec,
        ],
        out_shape=[
            jax.ShapeDtypeStruct((b, t, RWKV_W), BF16),
            jax.ShapeDtypeStruct((b, RWKV_GROUPS, GROUP, GROUP), F32),
        ],
        scratch_shapes=[
            pltpu.VMEM((nb, L + SUBLANE, w3), F32),
            pltpu.VMEM((nb, L + SUBLANE, 2 * LANE), F32),
            pltpu.VMEM((nb, RWKV_GROUPS, GROUP, GROUP), F32),
        ],
        compiler_params=pltpu.CompilerParams(
            dimension_semantics=("arbitrary", "arbitrary"), vmem_limit_bytes=VMEM_LIMIT),
        name="rwkv_scan",
    )(proj3, proj3, proj3, proj3, proj3, proj3, p_r, p_k, p_v, p_xw, p_xa, s0_bd,
      mu_rkv, mu_w, mu_a, row1(w0), wl_pad, row1(a0), al_pad, row1(k_k), row1(k_a), row1(r_k),
      row1(ln_w), row1(ln_b))
    s6 = sfin.reshape(b, RWKV_GROUPS, PACK, RWKV_HD, PACK, RWKV_HD)
    s_new = jnp.einsum("bghvjk,hj->bghkv", s6, jnp.eye(PACK, dtype=sfin.dtype))
    return o.reshape(b * t, RWKV_W), s_new.reshape(b, RWKV_HEADS, RWKV_HD, RWKV_HD)


def _merge_kernel(og_ref, or_ref, gg_ref, gr_ref, x_ref, gate_ref, wog_ref, wor_ref, wout_ref, fg_ref, o_ref,
                  *, final_norm):
    m = _sigmoid(gg_ref[...]) * _mm(og_ref[...], wog_ref[...]) \
        + _sigmoid(gr_ref[...]) * _mm(or_ref[...], wor_ref[...])
    out = _mm(m.astype(BF16), wout_ref[...])
    xn = x_ref[...] + gate_ref[...] * out
    if final_norm:
        xn = xn * lax.rsqrt(jnp.mean(xn * xn, axis=-1, keepdims=True) + EPS) * fg_ref[...]
    o_ref[...] = xn


def _merge(o_g, o_r, proj, x2d, gate, w_og, w_or, w_out, final_g, rows_per_mod, final_norm):
    m, d = x2d.shape
    tm = min(m, 256) if rows_per_mod == 1 else min(rows_per_mod, 256)
    if rows_per_mod == 1:
        gate_spec = pl.BlockSpec((tm, d), lambda i: (i, 0))
    else:
        assert rows_per_mod % tm == 0
        per = rows_per_mod // tm
        gate = gate.reshape(-1, 1, d)
        gate_spec = pl.BlockSpec((None, 1, d), lambda i: (i // per, 0, 0))
    whole = lambda shape: pl.BlockSpec(shape, lambda i: (0, 0))
    gcol = COL_GATES // d
    return pl.pallas_call(
        functools.partial(_merge_kernel, final_norm=final_norm),
        grid=(m // tm,),
        in_specs=[
            pl.BlockSpec((tm, GDN_W), lambda i: (i, 0)),
            pl.BlockSpec((tm, RWKV_W), lambda i: (i, 0)),
            pl.BlockSpec((tm, d), lambda i: (i, gcol)),
            pl.BlockSpec((tm, d), lambda i: (i, gcol + 1)),
            pl.BlockSpec((tm, d), lambda i: (i, 0)),
            gate_spec,
            whole((GDN_W, d)), whole((RWKV_W, d)), whole((d, d)), whole((1, d)),
        ],
        out_specs=pl.BlockSpec((tm, d), lambda i: (i, 0)),
        out_shape=jax.ShapeDtypeStruct((m, d), F32),
        compiler_params=pltpu.CompilerParams(
            dimension_semantics=("arbitrary",), vmem_limit_bytes=VMEM_LIMIT),
        name="merge_out",
    )(o_g, o_r, proj, proj, x2d, gate, w_og, w_or, w_out, final_g.reshape(1, d))


def _pad_in_weight(w_in):
    d = w_in.shape[0]
    o_zg = 3 * GDN_W
    o_b = o_zg + GDN_W
    o_rw = o_b + 2 * GDN_HEADS
    o_xw = o_rw + 3 * RWKV_W
    o_xa = o_xw + W_LORA
    o_zr = o_xa + A_LORA
    o_br = o_zr + RWKV_W
    z = lambda n: jnp.zeros((d, n), w_in.dtype)
    cols = [
        w_in[:, 0:o_zg], w_in[:, o_zg:o_b],
        w_in[:, o_rw:o_xw], w_in[:, o_zr:o_br],
        w_in[:, o_br:o_br + 2 * D_MODEL],
        w_in[:, o_b:o_rw], z(LANE - 2 * GDN_HEADS),
        w_in[:, o_xw:o_xa], z(LANE - W_LORA),
        w_in[:, o_xa:o_zr], z(LANE - A_LORA),
    ]
    used = COL_XA + LANE
    cols.append(z(N_PROJ - used))
    return jnp.concatenate(cols, axis=1).astype(BF16)


def _layer(x2d, batch, seq, scale, shift, gate, rows_per_mod, conv_prev, s_gdn, shift_prev, s_rwkv, p, final_g,
           final_norm, scan_nb):
    proj = _in_proj(x2d, scale, shift, p["norm_g"], p["w_in_pad"], rows_per_mod)
    proj3 = proj.reshape(batch, seq, N_PROJ)
    o_g, s_gdn_new = _gdn(proj3, conv_prev, s_gdn, p["conv_w"], p["a_log"], p["dt_bias"], p["onorm_g"], scan_nb)
    o_r, s_rwkv_new = _rwkv(proj3, shift_prev, s_rwkv, p["mu"], p["w0"], p["w_lora"], p["a0"], p["a_lora"],
                            p["k_k"], p["k_a"], p["r_k"], p["ln_w"], p["ln_b"], scan_nb)
    x_new = _merge(o_g, o_r, proj, x2d, gate, p["w_o_gdn"], p["w_o_rwkv"], p["w_out"], final_g, rows_per_mod,
                   final_norm)
    conv_new = proj3[:, seq - (CONV_W - 1):, 0:3 * GDN_W]
    last = proj3[:, seq - 1:, :]
    shift_new = jnp.concatenate(
        [last[..., COL_R * SEG:COL_R * SEG + 3 * RWKV_W], last[..., COL_XW:COL_XW + W_LORA],
         last[..., COL_XA:COL_XA + A_LORA]], axis=-1)
    return x_new, conv_new, s_gdn_new, shift_new, s_rwkv_new


def _forward(x_prompt, x_sample, c_prompt, c_sample, cache_gdn_conv, state_gdn, cache_rwkv_shift, state_rwkv,
             ada_w, ada_b, norm_g, w_in, gdn_conv_w, gdn_a_log, gdn_dt_bias, gdn_out_norm_g,
             rwkv_mu, rwkv_w0, rwkv_w_lora, rwkv_a0, rwkv_a_lora, rwkv_k_k, rwkv_k_a, rwkv_r_k,
             rwkv_ln_w, rwkv_ln_b, w_o_gdn, w_o_rwkv, w_out, final_norm_g):
    depth = ada_w.shape[0]
    bp, tp, d = x_prompt.shape
    bs, ts, _ = x_sample.shape
    assert CONV_W - 1 <= min(tp, ts)
    c_all = jnp.concatenate([c_prompt, c_sample], axis=0)
    rows = -(-(bp + bs) // SUBLANE) * SUBLANE
    c_all = jnp.pad(c_all, ((0, rows - (bp + bs)), (0, 0)))
    mod = _ada_mod(c_all, ada_w, ada_b)
    xp = x_prompt.reshape(bp * tp, d)
    xs = x_sample.reshape(bs * ts, d)
    outs = [[] for _ in range(8)]
    for l in range(depth):
        p = dict(norm_g=norm_g[l], w_in_pad=_pad_in_weight(w_in[l]), conv_w=gdn_conv_w[l], a_log=gdn_a_log[l],
                 dt_bias=gdn_dt_bias[l], onorm_g=gdn_out_norm_g[l], mu=rwkv_mu[l], w0=rwkv_w0[l],
                 w_lora=rwkv_w_lora[l], a0=rwkv_a0[l], a_lora=rwkv_a_lora[l], k_k=rwkv_k_k[l], k_a=rwkv_k_a[l],
                 r_k=rwkv_r_k[l].reshape(-1), ln_w=rwkv_ln_w[l], ln_b=rwkv_ln_b[l],
                 w_o_gdn=w_o_gdn[l].astype(BF16), w_o_rwkv=w_o_rwkv[l].astype(BF16), w_out=w_out[l].astype(BF16))
        last = l == depth - 1
        m_p = mod[l, 0:bp]
        m_s = jnp.repeat(mod[l, bp:bp + bs], ts, axis=0)
        sh_p, sc_p, gt_p = m_p[:, 0:d], m_p[:, d:2 * d], m_p[:, 2 * d:]
        sh_s, sc_s, gt_s = m_s[:, 0:d], m_s[:, d:2 * d], m_s[:, 2 * d:]
        zeros = lambda *s: jnp.zeros(s, F32)
        xp, c1, g1, h1, r1 = _layer(
            xp, bp, tp, sc_p, sh_p, gt_p, tp,
            zeros(bp, CONV_W - 1, 3 * GDN_W), zeros(bp, GDN_HEADS, GDN_DK, GDN_DV),
            zeros(bp, 1, 3 * RWKV_W + W_LORA + A_LORA), zeros(bp, RWKV_HEADS, RWKV_HD, RWKV_HD),
            p, final_norm_g, last, scan_nb=bp)
        xs, c2, g2, h2, r2 = _layer(
            xs, bs, ts, sc_s, sh_s, gt_s, 1,
            cache_gdn_conv[l], state_gdn[l], cache_rwkv_shift[l], state_rwkv[l],
            p, final_norm_g, last, scan_nb=min(bs, 2))
        for lst, val in zip(outs, (c1, g1, h1, r1, c2, g2, h2, r2)):
            lst.append(val)
    stk = [jnp.stack(v) for v in outs]
    return (xp.reshape(bp, tp, d), xs.reshape(bs, ts, d), *stk)


def kernel(x_prompt, x_sample, c_prompt, c_sample, cache_gdn_conv, state_gdn, cache_rwkv_shift, state_rwkv, ada_w, ada_b, norm_g, w_in, gdn_conv_w, gdn_a_log, gdn_dt_bias, gdn_out_norm_g, rwkv_mu, rwkv_w0, rwkv_w_lora, rwkv_a0, rwkv_a_lora, rwkv_k_k, rwkv_k_a, rwkv_r_k, rwkv_ln_w, rwkv_ln_b, w_o_gdn, w_o_rwkv, w_out, final_norm_g):
    return _forward(x_prompt, x_sample, c_prompt, c_sample, cache_gdn_conv, state_gdn, cache_rwkv_shift,
                    state_rwkv, ada_w, ada_b, norm_g, w_in, gdn_conv_w, gdn_a_log, gdn_dt_bias, gdn_out_norm_g,
                    rwkv_mu, rwkv_w0, rwkv_w_lora, rwkv_a0, rwkv_a_lora, rwkv_k_k, rwkv_k_a, rwkv_r_k,
                    rwkv_ln_w, rwkv_ln_b, w_o_gdn, w_o_rwkv, w_out, final_norm_g)
```

```python
import functools
import math

import jax
import jax.numpy as jnp
from jax import lax
from jax.experimental import pallas as pl
from jax.experimental.pallas import tpu as pltpu

F32 = jnp.float32
BF16 = jnp.bfloat16

D_MODEL = 2048
GDN_HEADS = 8
GDN_DK = 128
GDN_DV = 128
GDN_W = GDN_HEADS * GDN_DK
GDN_CHUNK = 64
CONV_W = 4
RWKV_HEADS = 16
RWKV_HD = 64
RWKV_W = RWKV_HEADS * RWKV_HD
RWKV_CHUNK = 64
W_LORA = 96
A_LORA = 96
EPS = 1e-6
GN_EPS = 64e-5
W_DECAY_OFFSET = 0.5

LANE = 128
SUBLANE = 8
SEG = 1024
COL_Q, COL_K, COL_V, COL_ZG, COL_R, COL_RK, COL_RV, COL_ZR = 0, 1, 2, 3, 4, 5, 6, 7
COL_GATES = 8 * SEG
COL_BA = 12 * SEG
COL_XW = COL_BA + LANE
COL_XA = COL_XW + LANE
N_PROJ = 13 * SEG
GROUP = 256
PACK = 4
RWKV_GROUPS = RWKV_W // GROUP
GDN_PAIRS = GDN_W // GROUP
VMEM_LIMIT = 56 * 1024 * 1024


def _mm(a, b):
    return jnp.dot(a, b, preferred_element_type=F32)


def _mm_nt(a, b):
    return lax.dot_general(a, b, (((1,), (1,)), ((), ())), preferred_element_type=F32)


def _mm_tn(a, b):
    return lax.dot_general(a, b, (((0,), (0,)), ((), ())), preferred_element_type=F32)


def _split_bf16(x):
    hi = x.astype(BF16)
    return hi, (x - hi.astype(F32)).astype(BF16)


def _split3_bf16(x):
    hi = x.astype(BF16)
    mid, lo = _split_bf16(x - hi.astype(F32))
    return hi, mid, lo


def _sigmoid(x):
    return 1.0 / (1.0 + jnp.exp(-x))


def _silu(x):
    return x * _sigmoid(x)


def _softplus(x):
    return jnp.maximum(x, 0.0) + jnp.log1p(jnp.exp(-jnp.abs(x)))


def _iota(shape, dim):
    return lax.broadcasted_iota(jnp.int32, shape, dim)


def _bits(n):
    assert n & (n - 1) == 0
    return n.bit_length() - 1


def _mod_kernel(c_ref, w_ref, b_ref, o_ref):
    o_ref[...] = _mm(c_ref[...], w_ref[...]) + b_ref[...]


def _ada_mod(c_all, ada_w, ada_b):
    depth, d, n3 = ada_w.shape
    rows = c_all.shape[0]
    tn = 512
    return pl.pallas_call(
        _mod_kernel,
        grid=(depth, n3 // tn),
        in_specs=[
            pl.BlockSpec((rows, d), lambda l, j: (0, 0)),
            pl.BlockSpec((None, d, tn), lambda l, j: (l, 0, j)),
            pl.BlockSpec((None, 1, tn), lambda l, j: (l, 0, j)),
        ],
        out_specs=pl.BlockSpec((None, rows, tn), lambda l, j: (l, 0, j)),
        out_shape=jax.ShapeDtypeStruct((depth, rows, n3), F32),
        compiler_params=pltpu.CompilerParams(
            dimension_semantics=("arbitrary", "arbitrary"), vmem_limit_bytes=VMEM_LIMIT),
        name="ada_mod",
    )(c_all, ada_w, ada_b.reshape(depth, 1, n3))


def _in_proj_kernel(x_ref, sc_ref, sh_ref, g_ref, w_ref, o_ref, h_ref):
    @pl.when(pl.program_id(1) == 0)
    def _():
        x = x_ref[...]
        ms = jnp.mean(x * x, axis=-1, keepdims=True)
        h = x * lax.rsqrt(ms + EPS) * g_ref[...] * (1.0 + sc_ref[...]) + sh_ref[...]
        h_ref[...] = h.astype(BF16)

    o_ref[...] = _mm(h_ref[...], w_ref[...])


def _in_proj(x2d, scale, shift, norm_g, w_pad, rows_per_mod):
    m, d = x2d.shape
    tm = min(m, 512) if rows_per_mod == 1 else min(rows_per_mod, 1024)
    tn = SEG
    if rows_per_mod == 1:
        mod_spec = pl.BlockSpec((tm, d), lambda i, j: (i, 0))
    else:
        assert rows_per_mod % tm == 0
        per = rows_per_mod // tm
        scale = scale.reshape(-1, 1, d)
        shift = shift.reshape(-1, 1, d)
        mod_spec = pl.BlockSpec((None, 1, d), lambda i, j: (i // per, 0, 0))
    return pl.pallas_call(
        _in_proj_kernel,
        grid=(m // tm, N_PROJ // tn),
        in_specs=[
            pl.BlockSpec((tm, d), lambda i, j: (i, 0)),
            mod_spec, mod_spec,
            pl.BlockSpec((1, d), lambda i, j: (0, 0)),
            pl.BlockSpec((d, tn), lambda i, j: (0, j)),
        ],
        out_specs=pl.BlockSpec((tm, tn), lambda i, j: (i, j)),
        out_shape=jax.ShapeDtypeStruct((m, N_PROJ), F32),
        scratch_shapes=[pltpu.VMEM((tm, d), BF16)],
        compiler_params=pltpu.CompilerParams(
            dimension_semantics=("arbitrary", "arbitrary"), vmem_limit_bytes=VMEM_LIMIT),
        name="in_proj",
    )(x2d, scale, shift, norm_g.reshape(1, d), w_pad)


def _tile_rows(x, reps):
    return jnp.concatenate([x] * reps, axis=0)


def _head_mask(rows, row_bits, cols, col_bits):
    return (_iota((rows, cols), 0) >> row_bits) == (_iota((rows, cols), 1) >> col_bits)


def _block_sums(xs, gmat_b):
    t, width = xs[0].shape
    ng = width // GROUP
    parts = []
    for x in xs:
        for piece in _split_bf16(x):
            parts += [piece[:, g * GROUP:(g + 1) * GROUP] for g in range(ng)]
    res = _mm(jnp.concatenate(parts, axis=0), gmat_b)
    outs = []
    for i in range(len(xs)):
        blk = lambda p, g: res[((2 * i + p) * ng + g) * t:((2 * i + p) * ng + g + 1) * t]
        outs.append(jnp.concatenate([blk(0, g) + blk(1, g) for g in range(ng)], axis=1))
    return outs


def _cumsum_rows(x, tri_b):
    w = x.shape[1]
    res = _mm(tri_b, jnp.concatenate(_split3_bf16(x), axis=1))
    return res[:, 0:w] + (res[:, w:2 * w] + res[:, 2 * w:])


def _unit_lower_inverses(e_mats, eye, m_pp_b):
    L, pw_w = e_mats[0].shape
    steps = _bits(L)
    pws = [-e for e in e_mats]
    t_invs = [eye + pw for pw in pws]
    for s in range(steps):
        first, last = s == 0, s == steps - 1
        if first and last:
            break
        for c in range(len(e_mats)):
            p_hi, p_lo = _split_bf16(pws[c])
            bd_hi = _tile_rows(p_hi, PACK) * m_pp_b
            bd_lo = _tile_rows(p_lo, PACK) * m_pp_b
            rows = ([] if first else [_split_bf16(t_invs[c])]) + ([] if last else [(p_hi, p_lo)])
            x_hi = jnp.concatenate([hi for hi, _ in rows], axis=0)
            x_lo = jnp.concatenate([lo for _, lo in rows], axis=0)
            nr = x_hi.shape[0]
            hh = _mm(jnp.concatenate([x_hi, x_lo], axis=0), bd_hi)
            res = hh[0:nr] + (hh[nr:] + _mm(x_hi, bd_lo))
            if not first:
                t_invs[c] = t_invs[c] + res[0:L]
            if not last:
                pws[c] = res[nr - L:]
    return t_invs


def _gdn_kernel(q_ref, k_ref, v_ref, z_ref, ba_ref, cw_ref, prev_ref, s0_ref, alog_ref, dtb_ref, on_ref,
                o_ref, sfin_ref, ubuf, sb, *, nb, chunk, nsteps):
    L = chunk
    i = pl.program_id(1)
    l_bits = _bits(L)
    dk_bits = _bits(GDN_DK)
    pw_w = PACK * L
    hw = PACK * GDN_DK
    n_grp = GDN_HEADS // PACK
    hp = GROUP // GDN_DK
    dsl = lambda h: slice((h % hp) * GDN_DK, (h % hp + 1) * GDN_DK)

    m_pd_b = _head_mask(pw_w, l_bits, hw, dk_bits).astype(BF16)
    m_pp_b = _head_mask(pw_w, l_bits, pw_w, l_bits).astype(BF16)
    m_dd = _head_mask(GROUP, dk_bits, GROUP, dk_bits)
    m_dd_b = m_dd.astype(BF16)
    t_row = _iota((L, pw_w), 0)
    j_lane = _iota((L, pw_w), 1) & (L - 1)
    strict = j_lane < t_row
    incl = j_lane <= t_row
    eye = (j_lane == t_row).astype(F32)
    tri_b = (_iota((L, L), 0) >= _iota((L, L), 1)).astype(BF16)
    ones_b = jnp.ones((L, L), BF16)
    lane = _iota((L, LANE), 1)
    x_rows = _iota((LANE, GDN_HEADS * L + GDN_W), 0) & (GDN_HEADS - 1)
    x_cols = _iota((LANE, GDN_HEADS * L + GDN_W), 1)
    x_head = jnp.where(x_cols < GDN_HEADS * L, x_cols >> l_bits, (x_cols - GDN_HEADS * L) >> dk_bits)
    expand_b = ((x_rows == x_head) & (_iota((LANE, GDN_HEADS * L + GDN_W), 0) < 2 * GDN_HEADS)).astype(BF16)

    def bdiag(x):
        return _tile_rows(x.astype(BF16), PACK) * m_pd_b

    @pl.when(i == 0)
    def _():
        sb[...] = jnp.zeros(sb.shape, F32)
        for n in range(nb):
            ubuf[n, 0:SUBLANE, :] = prev_ref[n]
            for h in range(GDN_HEADS):
                sb[n, h // hp, dsl(h), dsl(h)] = s0_ref[n, h]

    base = SUBLANE - (CONV_W - 1)
    pre = []
    for n in range(nb):
        ubuf[n, SUBLANE:SUBLANE + L, 0:GDN_W] = q_ref[n]
        ubuf[n, SUBLANE:SUBLANE + L, GDN_W:2 * GDN_W] = k_ref[n]
        ubuf[n, SUBLANE:SUBLANE + L, 2 * GDN_W:3 * GDN_W] = v_ref[n]
        y = ubuf[n, base:base + L, :] * cw_ref[0:1, :]
        for j in range(1, CONV_W):
            y = y + ubuf[n, base + j:base + j + L, :] * cw_ref[j:j + 1, :]
        ubuf[n, 0:SUBLANE, :] = ubuf[n, L:L + SUBLANE, :]
        act = _silu(y)
        q = act[:, 0:GDN_W]
        k = act[:, GDN_W:2 * GDN_W]
        v = act[:, 2 * GDN_W:]
        ssq, ssk = _block_sums([q * q, k * k], m_dd_b)
        q = q * lax.rsqrt(ssq + EPS) * (GDN_DK ** -0.5)
        k = k * lax.rsqrt(ssk + EPS)

        ba = ba_ref[n]
        beta_c = jnp.where(lane < GDN_HEADS, _sigmoid(ba), 0.0)
        g_c = -jnp.exp(alog_ref[...]) * _softplus(ba + dtb_ref[...])
        gc_c = jnp.where((lane >= GDN_HEADS) & (lane < 2 * GDN_HEADS), _cumsum_rows(g_c, tri_b), 0.0)
        pieces = jnp.concatenate(list(_split_bf16(beta_c)) + list(_split3_bf16(gc_c)), axis=0)
        ex = _mm(pieces, expand_b)
        beta_x = ex[0:L] + ex[L:2 * L]
        gc_x = ex[2 * L:3 * L] + (ex[3 * L:4 * L] + ex[4 * L:])
        beta_p, beta_w = beta_x[:, 0:GDN_HEADS * L], beta_x[:, GDN_HEADS * L:]
        gc_p, gc_w = gc_x[:, 0:GDN_HEADS * L], gc_x[:, GDN_HEADS * L:]
        eye8 = jnp.concatenate([eye] * n_grp, axis=1)
        incl8 = jnp.concatenate([incl] * n_grp, axis=1)
        rw = _mm(ones_b, jnp.concatenate(_split3_bf16(gc_p * eye8), axis=1))
        pw8 = GDN_HEADS * L
        gc_row = rw[:, 0:pw8] + (rw[:, pw8:2 * pw8] + rw[:, 2 * pw8:])
        dm = jnp.where(incl8, jnp.exp(jnp.where(incl8, gc_p - gc_row, 0.0)), 0.0)
        gl_w = gc_w[L - 1:L, :]
        eg_w = jnp.exp(gc_w)
        qd = q * eg_w
        kd = k * jnp.exp(gl_w - gc_w)
        bv = beta_w * v
        bek = beta_w * eg_w * k
        pre.append(dict(q=q, k=k, beta_p=beta_p, dm=dm, qd=qd, kd=kd, bv=bv, bek=bek, egl_w=jnp.exp(gl_w)))

    chains = [(n, g) for g in range(n_grp) for n in range(nb)]
    wsl = lambda g: slice(g * hw, (g + 1) * hw)
    psl = lambda g: slice(g * pw_w, (g + 1) * pw_w)
    ppg = hw // GROUP
    a_mat, qkd = {}, {}
    for c in chains:
        p, sw, sp = pre[c[0]], wsl(c[1]), psl(c[1])
        prod = _mm_nt(jnp.concatenate([p["k"][:, sw], p["q"][:, sw]], axis=0).astype(BF16), bdiag(p["k"][:, sw]))
        a_mat[c] = jnp.where(strict, p["beta_p"][:, sp] * prod[0:L] * p["dm"][:, sp], 0.0)
        qkd[c] = prod[L:] * p["dm"][:, sp]
    t_inv = dict(zip(chains, _unit_lower_inverses([a_mat[c] for c in chains], eye, m_pp_b)))
    sol = {}
    for c in chains:
        p, sw = pre[c[0]], wsl(c[1])
        sol[c] = _mm(t_inv[c].astype(BF16), jnp.concatenate([bdiag(p["bv"][:, sw]), bdiag(p["bek"][:, sw])], axis=1))
    pairs = [(c, j) for c in chains for j in range(ppg)]
    s_p, r2, u_p = {}, {}, {}
    for c, j in pairs:
        pair = c[1] * ppg + j
        sg = slice(pair * GROUP, (pair + 1) * GROUP)
        s_p[c, j] = sb[c[0], pair]
        r2[c, j] = _mm(jnp.concatenate([sol[c][:, hw + j * GROUP:hw + (j + 1) * GROUP], pre[c[0]]["qd"][:, sg]],
                                       axis=0).astype(BF16), s_p[c, j].astype(BF16))
    for c, j in pairs:
        pair = c[1] * ppg + j
        sg = slice(pair * GROUP, (pair + 1) * GROUP)
        u_p[c, j] = sol[c][:, j * GROUP:(j + 1) * GROUP] - r2[c, j][0:L]
        upd = _mm_tn(pre[c[0]]["kd"][:, sg].astype(BF16), u_p[c, j].astype(BF16))
        sb[c[0], pair] = s_p[c, j] * pre[c[0]]["egl_w"][:, sg] + jnp.where(m_dd, upd, 0.0)
    o_g = {}
    for c in chains:
        u = jnp.concatenate([u_p[c, j] for j in range(ppg)], axis=1)
        qs = jnp.concatenate([r2[c, j][L:] for j in range(ppg)], axis=1)
        o_g[c] = qs + _mm(qkd[c].astype(BF16), bdiag(u))

    for n in range(nb):
        o = jnp.concatenate([o_g[(n, g)] for g in range(n_grp)], axis=1)
        sso, = _block_sums([o * o], m_dd_b)
        on = o * lax.rsqrt(sso * (1.0 / GDN_DV) + EPS) * on_ref[...]
        o_ref[n] = (on * _silu(z_ref[n])).astype(o_ref.dtype)

    @pl.when(i == nsteps - 1)
    def _():
        for n in range(nb):
            for h in range(GDN_HEADS):
                sfin_ref[n, h] = sb[n, h // hp, dsl(h), dsl(h)]


def _gdn(proj3, conv_prev, s0, conv_w, a_log, dt_bias, onorm_g, nb):
    b, t, _ = proj3.shape
    L = min(GDN_CHUNK, t)
    assert b % nb == 0 and t % L == 0 and L % SUBLANE == 0
    nsteps = t // L
    prev_pad = jnp.pad(conv_prev, ((0, 0), (SUBLANE - (CONV_W - 1), 0), (0, 0)))
    lane_pad = (GDN_HEADS, LANE - 2 * GDN_HEADS)
    alog_row = jnp.pad(a_log, lane_pad).reshape(1, LANE)
    dtb_row = jnp.pad(dt_bias, lane_pad).reshape(1, LANE)
    on_row = jnp.tile(onorm_g, GDN_HEADS).reshape(1, GDN_W)
    blk = lambda c, w: pl.BlockSpec((nb, L, w), lambda g, i, c=c: (g, i, c))
    const = lambda shape: pl.BlockSpec(shape, lambda g, i: (0,) * len(shape))
    state_spec = pl.BlockSpec((nb, GDN_HEADS, GDN_DK, GDN_DV), lambda g, i: (g, 0, 0, 0))
    kern = functools.partial(_gdn_kernel, nb=nb, chunk=L, nsteps=nsteps)
    o, sfin = pl.pallas_call(
        kern,
        grid=(b // nb, nsteps),
        in_specs=[
            blk(COL_Q, SEG), blk(COL_K, SEG), blk(COL_V, SEG), blk(COL_ZG, SEG), blk(COL_BA // LANE, LANE),
            const((CONV_W, 3 * GDN_W)),
            pl.BlockSpec((nb, SUBLANE, 3 * GDN_W), lambda g, i: (g, 0, 0)),
            state_spec,
            const((1, LANE)), const((1, LANE)), const((1, GDN_W)),
        ],
        out_specs=[
            pl.BlockSpec((nb, L, GDN_W), lambda g, i: (g, i, 0)),
            state_spec,
        ],
        out_shape=[
            jax.ShapeDtypeStruct((b, t, GDN_W), BF16),
            jax.ShapeDtypeStruct((b, GDN_HEADS, GDN_DK, GDN_DV), F32),
        ],
        scratch_shapes=[
            pltpu.VMEM((nb, L + SUBLANE, 3 * GDN_W), F32),
            pltpu.VMEM((nb, GDN_PAIRS, GROUP, GROUP), F32),
        ],
        compiler_params=pltpu.CompilerParams(
            dimension_semantics=("arbitrary", "arbitrary"), vmem_limit_bytes=VMEM_LIMIT),
        name="gdn_scan",
    )(proj3, proj3, proj3, proj3, proj3, conv_w, prev_pad, s0, alog_row, dtb_row, on_row)
    return o.reshape(b * t, GDN_W), sfin


def _rwkv_kernel(r_ref, k_ref, v_ref, xw_ref, xa_ref, z_ref, pr_ref, pk_ref, pv_ref, pxw_ref, pxa_ref, s0_ref,
                 mu_ref, muw_ref, mua_ref, w0_ref, wl_ref, a0_ref, al_ref, kk_ref, ka_ref, rk_ref,
                 lnw_ref, lnb_ref,
                 o_ref, sfin_ref,
                 sbuf, lbuf, sbt, *, nb, chunk, nsteps):
    i = pl.program_id(1)
    L = chunk
    pw_w = PACK * L
    l_bits = _bits(L)
    hd_bits = _bits(RWKV_HD)

    m_pk_b = _head_mask(pw_w, l_bits, GROUP, hd_bits).astype(BF16)
    m_pp_b = _head_mask(pw_w, l_bits, pw_w, l_bits).astype(BF16)
    m_kk = _head_mask(GROUP, hd_bits, GROUP, hd_bits)
    m_kk_b = m_kk.astype(BF16)
    t_row = _iota((L, pw_w), 0)
    j_lane = _iota((L, pw_w), 1) & (L - 1)
    strict = j_lane < t_row
    incl = j_lane <= t_row
    eye = (j_lane == t_row).astype(F32)
    tri_b = (_iota((L, L), 0) >= _iota((L, L), 1)).astype(BF16)
    hsl = lambda h: slice((h % PACK) * RWKV_HD, (h % PACK + 1) * RWKV_HD)

    def bdiag(x):
        return _tile_rows(x.astype(BF16), PACK) * m_pk_b

    @pl.when(i == 0)
    def _():
        sbt[...] = jnp.zeros(sbt.shape, F32)
        for n in range(nb):
            for h in range(RWKV_HEADS):
                sbt[n, h // PACK, hsl(h), hsl(h)] = s0_ref[n, h]
            sbuf[n, SUBLANE - 1:SUBLANE, 0:SEG] = pr_ref[n]
            sbuf[n, SUBLANE - 1:SUBLANE, SEG:2 * SEG] = pk_ref[n]
            sbuf[n, SUBLANE - 1:SUBLANE, 2 * SEG:3 * SEG] = pv_ref[n]
            lbuf[n, SUBLANE - 1:SUBLANE, 0:LANE] = pxw_ref[n]
            lbuf[n, SUBLANE - 1:SUBLANE, LANE:2 * LANE] = pxa_ref[n]

    inv_hd = 1.0 / RWKV_HD
    pre = []
    for n in range(nb):
        sbuf[n, SUBLANE:SUBLANE + L, 0:SEG] = r_ref[n]
        sbuf[n, SUBLANE:SUBLANE + L, SEG:2 * SEG] = k_ref[n]
        sbuf[n, SUBLANE:SUBLANE + L, 2 * SEG:3 * SEG] = v_ref[n]
        lbuf[n, SUBLANE:SUBLANE + L, 0:LANE] = xw_ref[n]
        lbuf[n, SUBLANE:SUBLANE + L, LANE:2 * LANE] = xa_ref[n]
        cur = sbuf[n, SUBLANE:SUBLANE + L, :]
        prv = sbuf[n, SUBLANE - 1:SUBLANE - 1 + L, :]
        rkv = cur + (prv - cur) * mu_ref[...]
        curl = lbuf[n, SUBLANE:SUBLANE + L, :]
        prvl = lbuf[n, SUBLANE - 1:SUBLANE - 1 + L, :]
        xw = curl[:, 0:LANE] + (prvl[:, 0:LANE] - curl[:, 0:LANE]) * muw_ref[...]
        xa = curl[:, LANE:] + (prvl[:, LANE:] - curl[:, LANE:]) * mua_ref[...]
        sbuf[n, 0:SUBLANE, :] = sbuf[n, L:L + SUBLANE, :]
        lbuf[n, 0:SUBLANE, :] = lbuf[n, L:L + SUBLANE, :]
        r = rkv[:, 0:SEG]
        kr = rkv[:, SEG:2 * SEG]
        vr = rkv[:, 2 * SEG:3 * SEG]
        wl = w0_ref[...] + _mm(jnp.tanh(xw).astype(BF16), wl_ref[...])
        logw = -math.exp(-W_DECAY_OFFSET) * _sigmoid(wl)
        a = _sigmoid(a0_ref[...] + _mm(xa.astype(BF16), al_ref[...]))
        kkr = kr * kk_ref[...]
        kt = kr * (1.0 + (a - 1.0) * ka_ref[...])
        ss, bonus = _block_sums([kkr * kkr, r * kt * rk_ref[...]], m_kk_b)
        kk = kkr * lax.rsqrt(ss + EPS)
        ah = a * kk
        cum = _cumsum_rows(logw, tri_b)
        c_last = cum[L - 1:L, :]
        e_neg = jnp.exp(-cum)
        e_last = jnp.exp(c_last - cum)
        kx = kk * jnp.exp(cum - logw)
        rx = r * jnp.exp(cum)
        kb = kt * e_neg
        ab = ah * e_neg
        kh = kt * e_last
        ahh = ah * e_last
        pre.append(dict(vr=vr, bonus=bonus, kx=kx, rx=rx, kb=kb, ab=ab, kh=kh, ahh=ahh, g_last=jnp.exp(c_last)))

    chains = [(n, g) for g in range(RWKV_GROUPS) for n in range(nb)]
    gsl = lambda g: slice(g * GROUP, (g + 1) * GROUP)
    lhs, c_mat, e_mat, rk_mat, ra_mat = {}, {}, {}, {}, {}
    for c in chains:
        p, sl = pre[c[0]], gsl(c[1])
        lhs[c] = jnp.concatenate([p["kx"][:, sl], p["rx"][:, sl]], axis=0).astype(BF16)
        rhs_nt = jnp.concatenate([bdiag(p["kb"][:, sl]), bdiag(p["ab"][:, sl])], axis=0)
        prod = _mm_nt(lhs[c], rhs_nt)
        c_mat[c] = jnp.where(strict, prod[0:L, 0:pw_w], 0.0)
        e_mat[c] = jnp.where(strict, prod[0:L, pw_w:], 0.0)
        rk_mat[c] = jnp.where(incl, prod[L:, 0:pw_w], 0.0)
        ra_mat[c] = jnp.where(incl, prod[L:, pw_w:], 0.0)
    t_inv = dict(zip(chains, _unit_lower_inverses([e_mat[c] for c in chains], eye, m_pp_b)))
    s_g, ks_rs, cv, u = {}, {}, {}, {}
    for c in chains:
        s_g[c] = sbt[c[0], c[1]]
        ks_rs[c] = _mm_nt(lhs[c], s_g[c].astype(BF16))
        cv[c] = _mm(jnp.concatenate([c_mat[c], rk_mat[c]], axis=0).astype(BF16),
                    bdiag(pre[c[0]]["vr"][:, gsl(c[1])]))
    for c in chains:
        u[c] = _mm(t_inv[c].astype(BF16), bdiag(ks_rs[c][0:L] + cv[c][0:L]))
    y_g = {}
    for c in chains:
        p, sl = pre[c[0]], gsl(c[1])
        y_g[c] = ks_rs[c][L:] + cv[c][L:] - _mm(ra_mat[c].astype(BF16), bdiag(u[c]))
        upd = _mm_tn(jnp.concatenate([p["vr"][:, sl], -u[c]], axis=0).astype(BF16),
                     jnp.concatenate([p["kh"][:, sl], p["ahh"][:, sl]], axis=0).astype(BF16))
        sbt[c[0], c[1]] = s_g[c] * p["g_last"][:, sl] + jnp.where(m_kk, upd, 0.0)

    for n in range(nb):
        y = jnp.concatenate([y_g[(n, g)] for g in range(RWKV_GROUPS)], axis=1)
        dlt = y - _block_sums([y], m_kk_b)[0] * inv_hd
        var = _block_sums([dlt * dlt], m_kk_b)[0] * inv_hd
        yn = dlt * lax.rsqrt(var + GN_EPS) * lnw_ref[...] + lnb_ref[...]
        o_ref[n] = ((yn + pre[n]["bonus"] * pre[n]["vr"]) * _silu(z_ref[n])).astype(o_ref.dtype)

    @pl.when(i == nsteps - 1)
    def _():
        for n in range(nb):
            for h in range(RWKV_HEADS):
                sfin_ref[n, h] = sbt[n, h // PACK, hsl(h), hsl(h)]


def _rwkv(proj3, shift_prev, s0, mu, w0, w_lora, a0, a_lora, k_k, k_a, r_k, ln_w, ln_b, nb):
    b, t, _ = proj3.shape
    L = min(RWKV_CHUNK, t)
    assert b % nb == 0 and t % L == 0 and L % SUBLANE == 0
    nsteps = t // L
    w3 = 3 * RWKV_W
    padl = lambda x, n: jnp.pad(x, [(0, 0)] * (x.ndim - 1) + [(0, LANE - n)])
    p_r, p_k, p_v = shift_prev[..., 0:RWKV_W], shift_prev[..., RWKV_W:2 * RWKV_W], shift_prev[..., 2 * RWKV_W:w3]
    p_xw = padl(shift_prev[..., w3:w3 + W_LORA], W_LORA)
    p_xa = padl(shift_prev[..., w3 + W_LORA:], A_LORA)
    mu_rkv = mu[0:w3].reshape(1, w3)
    mu_w = padl(mu[w3:w3 + W_LORA], W_LORA).reshape(1, LANE)
    mu_a = padl(mu[w3 + W_LORA:], A_LORA).reshape(1, LANE)
    wl_pad = jnp.pad(w_lora, ((0, LANE - W_LORA), (0, 0))).astype(BF16)
    al_pad = jnp.pad(a_lora, ((0, LANE - A_LORA), (0, 0))).astype(BF16)
    s0_t = s0.transpose(0, 1, 3, 2)
    row1 = lambda x: x.reshape(1, RWKV_W)
    blk = lambda c, w: pl.BlockSpec((nb, L, w), lambda g, i, c=c: (g, i, c))
    prevspec = lambda w: pl.BlockSpec((nb, 1, w), lambda g, i: (g, 0, 0))
    const = lambda shape: pl.BlockSpec(shape, lambda g, i: (0,) * len(shape))
    state_spec = pl.BlockSpec((nb, RWKV_HEADS, RWKV_HD, RWKV_HD), lambda g, i: (g, 0, 0, 0))
    kern = functools.partial(_rwkv_kernel, nb=nb, chunk=L, nsteps=nsteps)
    o, sfin = pl.pallas_call(
        kern,
        grid=(b // nb, nsteps),
        in_specs=[
            blk(COL_R, SEG), blk(COL_RK, SEG), blk(COL_RV, SEG),
            blk(COL_XW // LANE, LANE), blk(COL_XA // LANE, LANE), blk(COL_ZR, SEG),
            prevspec(SEG), prevspec(SEG), prevspec(SEG), prevspec(LANE), prevspec(LANE),
            state_spec,
            const((1, w3)), const((1, LANE)), const((1, LANE)),
            const((1, RWKV_W)), const((LANE, RWKV_W)), const((1, RWKV_W)), const((LANE, RWKV_W)),
            const((1, RWKV_W)), const((1, RWKV_W)), const((1, RWKV_W)), const((1, RWKV_W)), const((1, RWKV_W)),
        ],
        out_specs=[
            pl.BlockSpec((nb, L, RWKV_W), lambda g, i: (g, i, 0)),
            state_spec,
        ],
        out_shape=[
            jax.ShapeDtypeStruct((b, t, RWKV_W), BF16),
            jax.ShapeDtypeStruct((b, RWKV_HEADS, RWKV_HD, RWKV_HD), F32),
        ],
        scratch_shapes=[
            pltpu.VMEM((nb, L + SUBLANE, w3), F32),
            pltpu.VMEM((nb, L + SUBLANE, 2 * LANE), F32),
            pltpu.VMEM((nb, RWKV_GROUPS, GROUP, GROUP), F32),
        ],
        compiler_params=pltpu.CompilerParams(
            dimension_semantics=("arbitrary", "arbitrary"), vmem_limit_bytes=VMEM_LIMIT),
        name="rwkv_scan",
    )(proj3, proj3, proj3, proj3, proj3, proj3, p_r, p_k, p_v, p_xw, p_xa, s0_t,
      mu_rkv, mu_w, mu_a, row1(w0), wl_pad, row1(a0), al_pad, row1(k_k), row1(k_a), row1(r_k),
      row1(ln_w), row1(ln_b))
    return o.reshape(b * t, RWKV_W), sfin.transpose(0, 1, 3, 2)


def _merge_kernel(og_ref, or_ref, gg_ref, gr_ref, x_ref, gate_ref, wog_ref, wor_ref, wout_ref, fg_ref, o_ref,
                  *, final_norm):
    m = _sigmoid(gg_ref[...]) * _mm(og_ref[...], wog_ref[...]) \
        + _sigmoid(gr_ref[...]) * _mm(or_ref[...], wor_ref[...])
    out = _mm(m.astype(BF16), wout_ref[...])
    xn = x_ref[...] + gate_ref[...] * out
    if final_norm:
        xn = xn * lax.rsqrt(jnp.mean(xn * xn, axis=-1, keepdims=True) + EPS) * fg_ref[...]
    o_ref[...] = xn


def _merge(o_g, o_r, proj, x2d, gate, w_og, w_or, w_out, final_g, rows_per_mod, final_norm):
    m, d = x2d.shape
    tm = min(m, 256) if rows_per_mod == 1 else min(rows_per_mod, 256)
    if rows_per_mod == 1:
        gate_spec = pl.BlockSpec((tm, d), lambda i: (i, 0))
    else:
        assert rows_per_mod % tm == 0
        per = rows_per_mod // tm
        gate = gate.reshape(-1, 1, d)
        gate_spec = pl.BlockSpec((None, 1, d), lambda i: (i // per, 0, 0))
    whole = lambda shape: pl.BlockSpec(shape, lambda i: (0, 0))
    gcol = COL_GATES // d
    return pl.pallas_call(
        functools.partial(_merge_kernel, final_norm=final_norm),
        grid=(m // tm,),
        in_specs=[
            pl.BlockSpec((tm, GDN_W), lambda i: (i, 0)),
            pl.BlockSpec((tm, RWKV_W), lambda i: (i, 0)),
            pl.BlockSpec((tm, d), lambda i: (i, gcol)),
            pl.BlockSpec((tm, d), lambda i: (i, gcol + 1)),
            pl.BlockSpec((tm, d), lambda i: (i, 0)),
            gate_spec,
            whole((GDN_W, d)), whole((RWKV_W, d)), whole((d, d)), whole((1, d)),
        ],
        out_specs=pl.BlockSpec((tm, d), lambda i: (i, 0)),
        out_shape=jax.ShapeDtypeStruct((m, d), F32),
        compiler_params=pltpu.CompilerParams(
            dimension_semantics=("arbitrary",), vmem_limit_bytes=VMEM_LIMIT),
        name="merge_out",
    )(o_g, o_r, proj, proj, x2d, gate, w_og, w_or, w_out, final_g.reshape(1, d))


def _pad_in_weight(w_in):
    d = w_in.shape[0]
    o_zg = 3 * GDN_W
    o_b = o_zg + GDN_W
    o_rw = o_b + 2 * GDN_HEADS
    o_xw = o_rw + 3 * RWKV_W
    o_xa = o_xw + W_LORA
    o_zr = o_xa + A_LORA
    o_br = o_zr + RWKV_W
    z = lambda n: jnp.zeros((d, n), w_in.dtype)
    cols = [
        w_in[:, 0:o_zg], w_in[:, o_zg:o_b],
        w_in[:, o_rw:o_xw], w_in[:, o_zr:o_br],
        w_in[:, o_br:o_br + 2 * D_MODEL],
        w_in[:, o_b:o_rw], z(LANE - 2 * GDN_HEADS),
        w_in[:, o_xw:o_xa], z(LANE - W_LORA),
        w_in[:, o_xa:o_zr], z(LANE - A_LORA),
    ]
    used = COL_XA + LANE
    cols.append(z(N_PROJ - used))
    return jnp.concatenate(cols, axis=1).astype(BF16)


def _layer(x2d, batch, seq, scale, shift, gate, rows_per_mod, conv_prev, s_gdn, shift_prev, s_rwkv, p, final_g,
           final_norm, scan_nb):
    proj = _in_proj(x2d, scale, shift, p["norm_g"], p["w_in_pad"], rows_per_mod)
    proj3 = proj.reshape(batch, seq, N_PROJ)
    o_g, s_gdn_new = _gdn(proj3, conv_prev, s_gdn, p["conv_w"], p["a_log"], p["dt_bias"], p["onorm_g"], scan_nb)
    o_r, s_rwkv_new = _rwkv(proj3, shift_prev, s_rwkv, p["mu"], p["w0"], p["w_lora"], p["a0"], p["a_lora"],
                            p["k_k"], p["k_a"], p["r_k"], p["ln_w"], p["ln_b"], scan_nb)
    x_new = _merge(o_g, o_r, proj, x2d, gate, p["w_o_gdn"], p["w_o_rwkv"], p["w_out"], final_g, rows_per_mod,
                   final_norm)
    conv_new = proj3[:, seq - (CONV_W - 1):, 0:3 * GDN_W]
    last = proj3[:, seq - 1:, :]
    shift_new = jnp.concatenate(
        [last[..., COL_R * SEG:COL_R * SEG + 3 * RWKV_W], last[..., COL_XW:COL_XW + W_LORA],
         last[..., COL_XA:COL_XA + A_LORA]], axis=-1)
    return x_new, conv_new, s_gdn_new, shift_new, s_rwkv_new


def _forward(x_prompt, x_sample, c_prompt, c_sample, cache_gdn_conv, state_gdn, cache_rwkv_shift, state_rwkv,
             ada_w, ada_b, norm_g, w_in, gdn_conv_w, gdn_a_log, gdn_dt_bias, gdn_out_norm_g,
             rwkv_mu, rwkv_w0, rwkv_w_lora, rwkv_a0, rwkv_a_lora, rwkv_k_k, rwkv_k_a, rwkv_r_k,
             rwkv_ln_w, rwkv_ln_b, w_o_gdn, w_o_rwkv, w_out, final_norm_g):
    depth = ada_w.shape[0]
    bp, tp, d = x_prompt.shape
    bs, ts, _ = x_sample.shape
    assert CONV_W - 1 <= min(tp, ts)
    c_all = jnp.concatenate([c_prompt, c_sample], axis=0)
    rows = -(-(bp + bs) // SUBLANE) * SUBLANE
    c_all = jnp.pad(c_all, ((0, rows - (bp + bs)), (0, 0)))
    mod = _ada_mod(c_all, ada_w, ada_b)
    xp = x_prompt.reshape(bp * tp, d)
    xs = x_sample.reshape(bs * ts, d)
    outs = [[] for _ in range(8)]
    for l in range(depth):
        p = dict(norm_g=norm_g[l], w_in_pad=_pad_in_weight(w_in[l]), conv_w=gdn_conv_w[l], a_log=gdn_a_log[l],
                 dt_bias=gdn_dt_bias[l], onorm_g=gdn_out_norm_g[l], mu=rwkv_mu[l], w0=rwkv_w0[l],
                 w_lora=rwkv_w_lora[l], a0=rwkv_a0[l], a_lora=rwkv_a_lora[l], k_k=rwkv_k_k[l], k_a=rwkv_k_a[l],
                 r_k=rwkv_r_k[l].reshape(-1), ln_w=rwkv_ln_w[l], ln_b=rwkv_ln_b[l],
                 w_o_gdn=w_o_gdn[l].astype(BF16), w_o_rwkv=w_o_rwkv[l].astype(BF16), w_out=w_out[l].astype(BF16))
        last = l == depth - 1
        m_p = mod[l, 0:bp]
        m_s = jnp.repeat(mod[l, bp:bp + bs], ts, axis=0)
        sh_p, sc_p, gt_p = m_p[:, 0:d], m_p[:, d:2 * d], m_p[:, 2 * d:]
        sh_s, sc_s, gt_s = m_s[:, 0:d], m_s[:, d:2 * d], m_s[:, 2 * d:]
        zeros = lambda *s: jnp.zeros(s, F32)
        xp, c1, g1, h1, r1 = _layer(
            xp, bp, tp, sc_p, sh_p, gt_p, tp,
            zeros(bp, CONV_W - 1, 3 * GDN_W), zeros(bp, GDN_HEADS, GDN_DK, GDN_DV),
            zeros(bp, 1, 3 * RWKV_W + W_LORA + A_LORA), zeros(bp, RWKV_HEADS, RWKV_HD, RWKV_HD),
            p, final_norm_g, last, scan_nb=bp)
        xs, c2, g2, h2, r2 = _layer(
            xs, bs, ts, sc_s, sh_s, gt_s, 1,
            cache_gdn_conv[l], state_gdn[l], cache_rwkv_shift[l], state_rwkv[l],
            p, final_norm_g, last, scan_nb=min(bs, 2))
        for lst, val in zip(outs, (c1, g1, h1, r1, c2, g2, h2, r2)):
            lst.append(val)
    stk = [jnp.stack(v) for v in outs]
    return (xp.reshape(bp, tp, d), xs.reshape(bs, ts, d), *stk)


def kernel(x_prompt, x_sample, c_prompt, c_sample, cache_gdn_conv, state_gdn, cache_rwkv_shift, state_rwkv, ada_w, ada_b, norm_g, w_in, gdn_conv_w, gdn_a_log, gdn_dt_bias, gdn_out_norm_g, rwkv_mu, rwkv_w0, rwkv_w_lora, rwkv_a0, rwkv_a_lora, rwkv_k_k, rwkv_k_a, rwkv_r_k, rwkv_ln_w, rwkv_ln_b, w_o_gdn, w_o_rwkv, w_out, final_norm_g):
    return _forward(x_prompt, x_sample, c_prompt, c_sample, cache_gdn_conv, state_gdn, cache_rwkv_shift,
                    state_rwkv, ada_w, ada_b, norm_g, w_in, gdn_conv_w, gdn_a_log, gdn_dt_bias, gdn_out_norm_g,
                    rwkv_mu, rwkv_w0, rwkv_w_lora, rwkv_a0, rwkv_a_lora, rwkv_k_k, rwkv_k_a, rwkv_r_k,
                    rwkv_ln_w, rwkv_ln_b, w_o_gdn, w_o_rwkv, w_out, final_norm_g)
```

```python
import functools
import math

import jax
import jax.numpy as jnp
from jax import lax
from jax.experimental import pallas as pl
from jax.experimental.pallas import tpu as pltpu

F32 = jnp.float32
BF16 = jnp.bfloat16

D_MODEL = 2048
GDN_HEADS = 8
GDN_DK = 128
GDN_DV = 128
GDN_W = GDN_HEADS * GDN_DK
GDN_CHUNK = 64
CONV_W = 4
RWKV_HEADS = 16
RWKV_HD = 64
RWKV_W = RWKV_HEADS * RWKV_HD
RWKV_CHUNK = 64
W_LORA = 96
A_LORA = 96
EPS = 1e-6
GN_EPS = 64e-5
W_DECAY_OFFSET = 0.5

LANE = 128
SUBLANE = 8
SEG = 1024
COL_Q, COL_K, COL_V, COL_ZG, COL_R, COL_RK, COL_RV, COL_ZR = 0, 1, 2, 3, 4, 5, 6, 7
COL_GATES = 8 * SEG
COL_BA = 12 * SEG
COL_XW = COL_BA + LANE
COL_XA = COL_XW + LANE
GROUP = 256
PROJ_TILE = 5 * GROUP
N_PROJ = 10 * PROJ_TILE
PACK = 4
RWKV_GROUPS = RWKV_W // GROUP
GDN_PAIRS = GDN_W // GROUP
VMEM_LIMIT = 56 * 1024 * 1024


def _mm(a, b):
    return jnp.dot(a, b, preferred_element_type=F32)


def _mm_nt(a, b):
    return lax.dot_general(a, b, (((1,), (1,)), ((), ())), preferred_element_type=F32)


def _mm_tn(a, b):
    return lax.dot_general(a, b, (((0,), (0,)), ((), ())), preferred_element_type=F32)


def _split_bf16(x):
    hi = x.astype(BF16)
    return hi, (x - hi.astype(F32)).astype(BF16)


def _split3_bf16(x):
    hi = x.astype(BF16)
    mid, lo = _split_bf16(x - hi.astype(F32))
    return hi, mid, lo


def _sigmoid(x):
    return 1.0 / (1.0 + jnp.exp(-x))


def _silu(x):
    return x * _sigmoid(x)


def _softplus(x):
    return jnp.maximum(x, 0.0) + jnp.log1p(jnp.exp(-jnp.abs(x)))


def _iota(shape, dim):
    return lax.broadcasted_iota(jnp.int32, shape, dim)


def _bits(n):
    assert n & (n - 1) == 0
    return n.bit_length() - 1


def _mod_kernel(c_ref, w_ref, b_ref, o_ref):
    o_ref[...] = _mm(c_ref[...], w_ref[...]) + b_ref[...]


def _ada_mod(c_all, ada_w, ada_b):
    depth, d, n3 = ada_w.shape
    rows = c_all.shape[0]
    tn = 512
    return pl.pallas_call(
        _mod_kernel,
        grid=(depth, n3 // tn),
        in_specs=[
            pl.BlockSpec((rows, d), lambda l, j: (0, 0)),
            pl.BlockSpec((None, d, tn), lambda l, j: (l, 0, j)),
            pl.BlockSpec((None, 1, tn), lambda l, j: (l, 0, j)),
        ],
        out_specs=pl.BlockSpec((None, rows, tn), lambda l, j: (l, 0, j)),
        out_shape=jax.ShapeDtypeStruct((depth, rows, n3), F32),
        compiler_params=pltpu.CompilerParams(
            dimension_semantics=("arbitrary", "arbitrary"), vmem_limit_bytes=VMEM_LIMIT),
        name="ada_mod",
    )(c_all, ada_w, ada_b.reshape(depth, 1, n3))


def _in_proj_kernel(x_ref, sc_ref, sh_ref, g_ref, w_ref, o_ref, h_ref):
    @pl.when(pl.program_id(1) == 0)
    def _():
        x = x_ref[...]
        ms = jnp.mean(x * x, axis=-1, keepdims=True)
        h = x * lax.rsqrt(ms + EPS) * g_ref[...] * (1.0 + sc_ref[...]) + sh_ref[...]
        h_ref[...] = h.astype(BF16)

    o_ref[...] = _mm(h_ref[...], w_ref[...])


def _in_proj(x2d, scale, shift, norm_g, w_pad, rows_per_mod):
    m, d = x2d.shape
    tm = min(m, 512) if rows_per_mod == 1 else min(rows_per_mod, 1024)
    tn = PROJ_TILE
    if rows_per_mod == 1:
        mod_spec = pl.BlockSpec((tm, d), lambda i, j: (i, 0))
    else:
        assert rows_per_mod % tm == 0
        per = rows_per_mod // tm
        scale = scale.reshape(-1, 1, d)
        shift = shift.reshape(-1, 1, d)
        mod_spec = pl.BlockSpec((None, 1, d), lambda i, j: (i // per, 0, 0))
    return pl.pallas_call(
        _in_proj_kernel,
        grid=(m // tm, N_PROJ // tn),
        in_specs=[
            pl.BlockSpec((tm, d), lambda i, j: (i, 0)),
            mod_spec, mod_spec,
            pl.BlockSpec((1, d), lambda i, j: (0, 0)),
            pl.BlockSpec((d, tn), lambda i, j: (0, j)),
        ],
        out_specs=pl.BlockSpec((tm, tn), lambda i, j: (i, j)),
        out_shape=jax.ShapeDtypeStruct((m, N_PROJ), F32),
        scratch_shapes=[pltpu.VMEM((tm, d), BF16)],
        compiler_params=pltpu.CompilerParams(
            dimension_semantics=("arbitrary", "arbitrary"), vmem_limit_bytes=VMEM_LIMIT),
        name="in_proj",
    )(x2d, scale, shift, norm_g.reshape(1, d), w_pad)


def _tile_rows(x, reps):
    return jnp.concatenate([x] * reps, axis=0)


def _head_mask(rows, row_bits, cols, col_bits):
    return (_iota((rows, cols), 0) >> row_bits) == (_iota((rows, cols), 1) >> col_bits)


def _block_sums(xs, gmat_b):
    t, width = xs[0].shape
    ng = width // GROUP
    parts = []
    for x in xs:
        xb = x.astype(BF16)
        parts += [xb[:, g * GROUP:(g + 1) * GROUP] for g in range(ng)]
    res = _mm(jnp.concatenate(parts, axis=0), gmat_b)
    return [jnp.concatenate([res[(i * ng + g) * t:(i * ng + g + 1) * t] for g in range(ng)], axis=1)
            for i in range(len(xs))]


def _cumsum_rows(x, tri_b):
    w = x.shape[1]
    res = _mm(tri_b, jnp.concatenate(_split3_bf16(x), axis=1))
    return res[:, 0:w] + (res[:, w:2 * w] + res[:, 2 * w:])


def _unit_lower_inverses(e_mats, eye, m_pp_b):
    L, pw_w = e_mats[0].shape
    steps = _bits(L)
    pws = [-e for e in e_mats]
    t_invs = [eye + pw for pw in pws]
    for s in range(steps):
        first, last = s == 0, s == steps - 1
        if first and last:
            break
        for c in range(len(e_mats)):
            p_hi, p_lo = _split_bf16(pws[c])
            bd_hi = _tile_rows(p_hi, PACK) * m_pp_b
            bd_lo = _tile_rows(p_lo, PACK) * m_pp_b
            rows = ([] if first else [_split_bf16(t_invs[c])]) + ([] if last else [(p_hi, p_lo)])
            x_hi = jnp.concatenate([hi for hi, _ in rows], axis=0)
            x_lo = jnp.concatenate([lo for _, lo in rows], axis=0)
            nr = x_hi.shape[0]
            hh = _mm(jnp.concatenate([x_hi, x_lo], axis=0), bd_hi)
            res = hh[0:nr] + (hh[nr:] + _mm(x_hi, bd_lo))
            if not first:
                t_invs[c] = t_invs[c] + res[0:L]
            if not last:
                pws[c] = res[nr - L:]
    return t_invs


def _gdn_kernel(q_ref, k_ref, v_ref, z_ref, ba_ref, cw_ref, prev_ref, s0_ref, alog_ref, dtb_ref, on_ref,
                o_ref, sfin_ref, ubuf, sb, *, nb, chunk, nsteps):
    L = chunk
    i = pl.program_id(1)
    l_bits = _bits(L)
    dk_bits = _bits(GDN_DK)
    pw_w = PACK * L
    hw = PACK * GDN_DK
    n_grp = GDN_HEADS // PACK
    hp = GROUP // GDN_DK
    dsl = lambda h: slice((h % hp) * GDN_DK, (h % hp + 1) * GDN_DK)

    m_pd_b = _head_mask(pw_w, l_bits, hw, dk_bits).astype(BF16)
    m_pp_b = _head_mask(pw_w, l_bits, pw_w, l_bits).astype(BF16)
    m_dd = _head_mask(GROUP, dk_bits, GROUP, dk_bits)
    m_dd_b = m_dd.astype(BF16)
    t_row = _iota((L, pw_w), 0)
    j_lane = _iota((L, pw_w), 1) & (L - 1)
    strict = j_lane < t_row
    incl = j_lane <= t_row
    eye = (j_lane == t_row).astype(F32)
    tri_b = (_iota((L, L), 0) >= _iota((L, L), 1)).astype(BF16)
    ones_b = jnp.ones((L, L), BF16)
    lane = _iota((L, LANE), 1)
    x_rows = _iota((LANE, GDN_HEADS * L + GDN_W), 0) & (GDN_HEADS - 1)
    x_cols = _iota((LANE, GDN_HEADS * L + GDN_W), 1)
    x_head = jnp.where(x_cols < GDN_HEADS * L, x_cols >> l_bits, (x_cols - GDN_HEADS * L) >> dk_bits)
    expand_b = ((x_rows == x_head) & (_iota((LANE, GDN_HEADS * L + GDN_W), 0) < 2 * GDN_HEADS)).astype(BF16)

    def bdiag(x):
        return _tile_rows(x.astype(BF16), PACK) * m_pd_b

    @pl.when(i == 0)
    def _():
        sb[...] = jnp.zeros(sb.shape, F32)
        for n in range(nb):
            ubuf[n, 0:SUBLANE, :] = prev_ref[n]
            for h in range(GDN_HEADS):
                sb[n, h // hp, dsl(h), dsl(h)] = s0_ref[n, h]

    base = SUBLANE - (CONV_W - 1)
    pre = []
    for n in range(nb):
        ubuf[n, SUBLANE:SUBLANE + L, 0:GDN_W] = q_ref[n]
        ubuf[n, SUBLANE:SUBLANE + L, GDN_W:2 * GDN_W] = k_ref[n]
        ubuf[n, SUBLANE:SUBLANE + L, 2 * GDN_W:3 * GDN_W] = v_ref[n]
        y = ubuf[n, base:base + L, :] * cw_ref[0:1, :]
        for j in range(1, CONV_W):
            y = y + ubuf[n, base + j:base + j + L, :] * cw_ref[j:j + 1, :]
        ubuf[n, 0:SUBLANE, :] = ubuf[n, L:L + SUBLANE, :]
        act = _silu(y)
        q = act[:, 0:GDN_W]
        k = act[:, GDN_W:2 * GDN_W]
        v = act[:, 2 * GDN_W:]
        ssq, ssk = _block_sums([q * q, k * k], m_dd_b)
        q = q * lax.rsqrt(ssq + EPS) * (GDN_DK ** -0.5)
        k = k * lax.rsqrt(ssk + EPS)

        ba = ba_ref[n]
        beta_c = jnp.where(lane < GDN_HEADS, _sigmoid(ba), 0.0)
        g_c = -jnp.exp(alog_ref[...]) * _softplus(ba + dtb_ref[...])
        gc_c = jnp.where((lane >= GDN_HEADS) & (lane < 2 * GDN_HEADS), _cumsum_rows(g_c, tri_b), 0.0)
        pieces = jnp.concatenate(list(_split_bf16(beta_c)) + list(_split3_bf16(gc_c)), axis=0)
        ex = _mm(pieces, expand_b)
        beta_x = ex[0:L] + ex[L:2 * L]
        gc_x = ex[2 * L:3 * L] + (ex[3 * L:4 * L] + ex[4 * L:])
        beta_p, beta_w = beta_x[:, 0:GDN_HEADS * L], beta_x[:, GDN_HEADS * L:]
        gc_p, gc_w = gc_x[:, 0:GDN_HEADS * L], gc_x[:, GDN_HEADS * L:]
        eye8 = jnp.concatenate([eye] * n_grp, axis=1)
        incl8 = jnp.concatenate([incl] * n_grp, axis=1)
        rw = _mm(ones_b, jnp.concatenate(_split3_bf16(gc_p * eye8), axis=1))
        pw8 = GDN_HEADS * L
        gc_row = rw[:, 0:pw8] + (rw[:, pw8:2 * pw8] + rw[:, 2 * pw8:])
        dm = jnp.where(incl8, jnp.exp(jnp.where(incl8, gc_p - gc_row, 0.0)), 0.0)
        gl_w = gc_w[L - 1:L, :]
        eg_w = jnp.exp(gc_w)
        qd = q * eg_w
        kd = k * jnp.exp(gl_w - gc_w)
        bv = beta_w * v
        bek = beta_w * eg_w * k
        pre.append(dict(q=q, k=k, beta_p=beta_p, dm=dm, qd=qd, kd=kd, bv=bv, bek=bek, egl_w=jnp.exp(gl_w)))

    chains = [(n, g) for g in range(n_grp) for n in range(nb)]
    wsl = lambda g: slice(g * hw, (g + 1) * hw)
    psl = lambda g: slice(g * pw_w, (g + 1) * pw_w)
    ppg = hw // GROUP
    a_mat, qkd = {}, {}
    for c in chains:
        p, sw, sp = pre[c[0]], wsl(c[1]), psl(c[1])
        prod = _mm_nt(jnp.concatenate([p["k"][:, sw], p["q"][:, sw]], axis=0).astype(BF16), bdiag(p["k"][:, sw]))
        a_mat[c] = jnp.where(strict, p["beta_p"][:, sp] * prod[0:L] * p["dm"][:, sp], 0.0)
        qkd[c] = prod[L:] * p["dm"][:, sp]
    t_inv = dict(zip(chains, _unit_lower_inverses([a_mat[c] for c in chains], eye, m_pp_b)))
    sol = {}
    for c in chains:
        p, sw = pre[c[0]], wsl(c[1])
        sol[c] = _mm(t_inv[c].astype(BF16), jnp.concatenate([bdiag(p["bv"][:, sw]), bdiag(p["bek"][:, sw])], axis=1))
    pairs = [(c, j) for c in chains for j in range(ppg)]
    s_p, r2, u_p = {}, {}, {}
    for c, j in pairs:
        pair = c[1] * ppg + j
        sg = slice(pair * GROUP, (pair + 1) * GROUP)
        s_p[c, j] = sb[c[0], pair]
        r2[c, j] = _mm(jnp.concatenate([sol[c][:, hw + j * GROUP:hw + (j + 1) * GROUP], pre[c[0]]["qd"][:, sg]],
                                       axis=0).astype(BF16), s_p[c, j].astype(BF16))
    for c, j in pairs:
        pair = c[1] * ppg + j
        sg = slice(pair * GROUP, (pair + 1) * GROUP)
        u_p[c, j] = sol[c][:, j * GROUP:(j + 1) * GROUP] - r2[c, j][0:L]
        upd = _mm_tn(pre[c[0]]["kd"][:, sg].astype(BF16), u_p[c, j].astype(BF16))
        sb[c[0], pair] = s_p[c, j] * pre[c[0]]["egl_w"][:, sg] + jnp.where(m_dd, upd, 0.0)
    o_g = {}
    for c in chains:
        u = jnp.concatenate([u_p[c, j] for j in range(ppg)], axis=1)
        qs = jnp.concatenate([r2[c, j][L:] for j in range(ppg)], axis=1)
        o_g[c] = qs + _mm(qkd[c].astype(BF16), bdiag(u))

    for n in range(nb):
        o = jnp.concatenate([o_g[(n, g)] for g in range(n_grp)], axis=1)
        sso, = _block_sums([o * o], m_dd_b)
        on = o * lax.rsqrt(sso * (1.0 / GDN_DV) + EPS) * on_ref[...]
        o_ref[n] = (on * _silu(z_ref[n])).astype(o_ref.dtype)

    @pl.when(i == nsteps - 1)
    def _():
        for n in range(nb):
            for h in range(GDN_HEADS):
                sfin_ref[n, h] = sb[n, h // hp, dsl(h), dsl(h)]


def _gdn(proj3, conv_prev, s0, conv_w, a_log, dt_bias, onorm_g, nb):
    b, t, _ = proj3.shape
    L = min(GDN_CHUNK, t)
    assert b % nb == 0 and t % L == 0 and L % SUBLANE == 0
    nsteps = t // L
    prev_pad = jnp.pad(conv_prev, ((0, 0), (SUBLANE - (CONV_W - 1), 0), (0, 0)))
    lane_pad = (GDN_HEADS, LANE - 2 * GDN_HEADS)
    alog_row = jnp.pad(a_log, lane_pad).reshape(1, LANE)
    dtb_row = jnp.pad(dt_bias, lane_pad).reshape(1, LANE)
    on_row = jnp.tile(onorm_g, GDN_HEADS).reshape(1, GDN_W)
    blk = lambda c, w: pl.BlockSpec((nb, L, w), lambda g, i, c=c: (g, i, c))
    const = lambda shape: pl.BlockSpec(shape, lambda g, i: (0,) * len(shape))
    state_spec = pl.BlockSpec((nb, GDN_HEADS, GDN_DK, GDN_DV), lambda g, i: (g, 0, 0, 0))
    kern = functools.partial(_gdn_kernel, nb=nb, chunk=L, nsteps=nsteps)
    o, sfin = pl.pallas_call(
        kern,
        grid=(b // nb, nsteps),
        in_specs=[
            blk(COL_Q, SEG), blk(COL_K, SEG), blk(COL_V, SEG), blk(COL_ZG, SEG), blk(COL_BA // LANE, LANE),
            const((CONV_W, 3 * GDN_W)),
            pl.BlockSpec((nb, SUBLANE, 3 * GDN_W), lambda g, i: (g, 0, 0)),
            state_spec,
            const((1, LANE)), const((1, LANE)), const((1, GDN_W)),
        ],
        out_specs=[
            pl.BlockSpec((nb, L, GDN_W), lambda g, i: (g, i, 0)),
            state_spec,
        ],
        out_shape=[
            jax.ShapeDtypeStruct((b, t, GDN_W), BF16),
            jax.ShapeDtypeStruct((b, GDN_HEADS, GDN_DK, GDN_DV), F32),
        ],
        scratch_shapes=[
            pltpu.VMEM((nb, L + SUBLANE, 3 * GDN_W), F32),
            pltpu.VMEM((nb, GDN_PAIRS, GROUP, GROUP), F32),
        ],
        compiler_params=pltpu.CompilerParams(
            dimension_semantics=("arbitrary", "arbitrary"), vmem_limit_bytes=VMEM_LIMIT),
        name="gdn_scan",
    )(proj3, proj3, proj3, proj3, proj3, conv_w, prev_pad, s0, alog_row, dtb_row, on_row)
    return o.reshape(b * t, GDN_W), sfin


def _rwkv_kernel(r_ref, k_ref, v_ref, xw_ref, xa_ref, z_ref, pr_ref, pk_ref, pv_ref, pxw_ref, pxa_ref, s0_ref,
                 mu_ref, muw_ref, mua_ref, w0_ref, wl_ref, a0_ref, al_ref, kk_ref, ka_ref, rk_ref,
                 lnw_ref, lnb_ref,
                 o_ref, sfin_ref,
                 sbuf, lbuf, sbt, *, nb, chunk, nsteps):
    i = pl.program_id(1)
    L = chunk
    pw_w = PACK * L
    l_bits = _bits(L)
    hd_bits = _bits(RWKV_HD)

    m_pk_b = _head_mask(pw_w, l_bits, GROUP, hd_bits).astype(BF16)
    m_pp_b = _head_mask(pw_w, l_bits, pw_w, l_bits).astype(BF16)
    m_kk = _head_mask(GROUP, hd_bits, GROUP, hd_bits)
    m_kk_b = m_kk.astype(BF16)
    t_row = _iota((L, pw_w), 0)
    j_lane = _iota((L, pw_w), 1) & (L - 1)
    strict = j_lane < t_row
    incl = j_lane <= t_row
    eye = (j_lane == t_row).astype(F32)
    tri_b = (_iota((L, L), 0) >= _iota((L, L), 1)).astype(BF16)
    hsl = lambda h: slice((h % PACK) * RWKV_HD, (h % PACK + 1) * RWKV_HD)

    def bdiag(x):
        return _tile_rows(x.astype(BF16), PACK) * m_pk_b

    @pl.when(i == 0)
    def _():
        sbt[...] = jnp.zeros(sbt.shape, F32)
        for n in range(nb):
            for h in range(RWKV_HEADS):
                sbt[n, h // PACK, hsl(h), hsl(h)] = s0_ref[n, h]
            sbuf[n, SUBLANE - 1:SUBLANE, 0:SEG] = pr_ref[n]
            sbuf[n, SUBLANE - 1:SUBLANE, SEG:2 * SEG] = pk_ref[n]
            sbuf[n, SUBLANE - 1:SUBLANE, 2 * SEG:3 * SEG] = pv_ref[n]
            lbuf[n, SUBLANE - 1:SUBLANE, 0:LANE] = pxw_ref[n]
            lbuf[n, SUBLANE - 1:SUBLANE, LANE:2 * LANE] = pxa_ref[n]

    inv_hd = 1.0 / RWKV_HD
    pre = []
    for n in range(nb):
        sbuf[n, SUBLANE:SUBLANE + L, 0:SEG] = r_ref[n]
        sbuf[n, SUBLANE:SUBLANE + L, SEG:2 * SEG] = k_ref[n]
        sbuf[n, SUBLANE:SUBLANE + L, 2 * SEG:3 * SEG] = v_ref[n]
        lbuf[n, SUBLANE:SUBLANE + L, 0:LANE] = xw_ref[n]
        lbuf[n, SUBLANE:SUBLANE + L, LANE:2 * LANE] = xa_ref[n]
        cur = sbuf[n, SUBLANE:SUBLANE + L, :]
        prv = sbuf[n, SUBLANE - 1:SUBLANE - 1 + L, :]
        rkv = cur + (prv - cur) * mu_ref[...]
        curl = lbuf[n, SUBLANE:SUBLANE + L, :]
        prvl = lbuf[n, SUBLANE - 1:SUBLANE - 1 + L, :]
        xw = curl[:, 0:LANE] + (prvl[:, 0:LANE] - curl[:, 0:LANE]) * muw_ref[...]
        xa = curl[:, LANE:] + (prvl[:, LANE:] - curl[:, LANE:]) * mua_ref[...]
        sbuf[n, 0:SUBLANE, :] = sbuf[n, L:L + SUBLANE, :]
        lbuf[n, 0:SUBLANE, :] = lbuf[n, L:L + SUBLANE, :]
        r = rkv[:, 0:SEG]
        kr = rkv[:, SEG:2 * SEG]
        vr = rkv[:, 2 * SEG:3 * SEG]
        wl = w0_ref[...] + _mm(jnp.tanh(xw).astype(BF16), wl_ref[...])
        logw = -math.exp(-W_DECAY_OFFSET) * _sigmoid(wl)
        a = _sigmoid(a0_ref[...] + _mm(xa.astype(BF16), al_ref[...]))
        kkr = kr * kk_ref[...]
        kt = kr * (1.0 + (a - 1.0) * ka_ref[...])
        ss, bonus = _block_sums([kkr * kkr, r * kt * rk_ref[...]], m_kk_b)
        kk = kkr * lax.rsqrt(ss + EPS)
        ah = a * kk
        cum = _cumsum_rows(logw, tri_b)
        c_last = cum[L - 1:L, :]
        e_neg = jnp.exp(-cum)
        e_last = jnp.exp(c_last - cum)
        kx = kk * jnp.exp(cum - logw)
        rx = r * jnp.exp(cum)
        kb = kt * e_neg
        ab = ah * e_neg
        kh = kt * e_last
        ahh = ah * e_last
        pre.append(dict(vr=vr, bonus=bonus, kx=kx, rx=rx, kb=kb, ab=ab, kh=kh, ahh=ahh, g_last=jnp.exp(c_last)))

    chains = [(n, g) for g in range(RWKV_GROUPS) for n in range(nb)]
    gsl = lambda g: slice(g * GROUP, (g + 1) * GROUP)
    lhs, c_mat, e_mat, rk_mat, ra_mat = {}, {}, {}, {}, {}
    for c in chains:
        p, sl = pre[c[0]], gsl(c[1])
        lhs[c] = jnp.concatenate([p["kx"][:, sl], p["rx"][:, sl]], axis=0).astype(BF16)
        rhs_nt = jnp.concatenate([bdiag(p["kb"][:, sl]), bdiag(p["ab"][:, sl])], axis=0)
        prod = _mm_nt(lhs[c], rhs_nt)
        c_mat[c] = jnp.where(strict, prod[0:L, 0:pw_w], 0.0)
        e_mat[c] = jnp.where(strict, prod[0:L, pw_w:], 0.0)
        rk_mat[c] = jnp.where(incl, prod[L:, 0:pw_w], 0.0)
        ra_mat[c] = jnp.where(incl, prod[L:, pw_w:], 0.0)
    t_inv = dict(zip(chains, _unit_lower_inverses([e_mat[c] for c in chains], eye, m_pp_b)))
    s_g, ks_rs, cv, u = {}, {}, {}, {}
    for c in chains:
        s_g[c] = sbt[c[0], c[1]]
        ks_rs[c] = _mm_nt(lhs[c], s_g[c].astype(BF16))
        cv[c] = _mm(jnp.concatenate([c_mat[c], rk_mat[c]], axis=0).astype(BF16),
                    bdiag(pre[c[0]]["vr"][:, gsl(c[1])]))
    for c in chains:
        u[c] = _mm(t_inv[c].astype(BF16), bdiag(ks_rs[c][0:L] + cv[c][0:L]))
    y_g = {}
    for c in chains:
        p, sl = pre[c[0]], gsl(c[1])
        y_g[c] = ks_rs[c][L:] + cv[c][L:] - _mm(ra_mat[c].astype(BF16), bdiag(u[c]))
        upd = _mm_tn(jnp.concatenate([p["vr"][:, sl], -u[c]], axis=0).astype(BF16),
                     jnp.concatenate([p["kh"][:, sl], p["ahh"][:, sl]], axis=0).astype(BF16))
        sbt[c[0], c[1]] = s_g[c] * p["g_last"][:, sl] + jnp.where(m_kk, upd, 0.0)

    for n in range(nb):
        y = jnp.concatenate([y_g[(n, g)] for g in range(RWKV_GROUPS)], axis=1)
        dlt = y - _block_sums([y], m_kk_b)[0] * inv_hd
        var = _block_sums([dlt * dlt], m_kk_b)[0] * inv_hd
        yn = dlt * lax.rsqrt(var + GN_EPS) * lnw_ref[...] + lnb_ref[...]
        o_ref[n] = ((yn + pre[n]["bonus"] * pre[n]["vr"]) * _silu(z_ref[n])).astype(o_ref.dtype)

    @pl.when(i == nsteps - 1)
    def _():
        for n in range(nb):
            for h in range(RWKV_HEADS):
                sfin_ref[n, h] = sbt[n, h // PACK, hsl(h), hsl(h)]


def _rwkv(proj3, shift_prev, s0, mu, w0, w_lora, a0, a_lora, k_k, k_a, r_k, ln_w, ln_b, nb):
    b, t, _ = proj3.shape
    L = min(RWKV_CHUNK, t)
    assert b % nb == 0 and t % L == 0 and L % SUBLANE == 0
    nsteps = t // L
    w3 = 3 * RWKV_W
    padl = lambda x, n: jnp.pad(x, [(0, 0)] * (x.ndim - 1) + [(0, LANE - n)])
    p_r, p_k, p_v = shift_prev[..., 0:RWKV_W], shift_prev[..., RWKV_W:2 * RWKV_W], shift_prev[..., 2 * RWKV_W:w3]
    p_xw = padl(shift_prev[..., w3:w3 + W_LORA], W_LORA)
    p_xa = padl(shift_prev[..., w3 + W_LORA:], A_LORA)
    mu_rkv = mu[0:w3].reshape(1, w3)
    mu_w = padl(mu[w3:w3 + W_LORA], W_LORA).reshape(1, LANE)
    mu_a = padl(mu[w3 + W_LORA:], A_LORA).reshape(1, LANE)
    wl_pad = jnp.pad(w_lora, ((0, LANE - W_LORA), (0, 0))).astype(BF16)
    al_pad = jnp.pad(a_lora, ((0, LANE - A_LORA), (0, 0))).astype(BF16)
    s0_t = s0.transpose(0, 1, 3, 2)
    row1 = lambda x: x.reshape(1, RWKV_W)
    blk = lambda c, w: pl.BlockSpec((nb, L, w), lambda g, i, c=c: (g, i, c))
    prevspec = lambda w: pl.BlockSpec((nb, 1, w), lambda g, i: (g, 0, 0))
    const = lambda shape: pl.BlockSpec(shape, lambda g, i: (0,) * len(shape))
    state_spec = pl.BlockSpec((nb, RWKV_HEADS, RWKV_HD, RWKV_HD), lambda g, i: (g, 0, 0, 0))
    kern = functools.partial(_rwkv_kernel, nb=nb, chunk=L, nsteps=nsteps)
    o, sfin = pl.pallas_call(
        kern,
        grid=(b // nb, nsteps),
        in_specs=[
            blk(COL_R, SEG), blk(COL_RK, SEG), blk(COL_RV, SEG),
            blk(COL_XW // LANE, LANE), blk(COL_XA // LANE, LANE), blk(COL_ZR, SEG),
            prevspec(SEG), prevspec(SEG), prevspec(SEG), prevspec(LANE), prevspec(LANE),
            state_spec,
            const((1, w3)), const((1, LANE)), const((1, LANE)),
            const((1, RWKV_W)), const((LANE, RWKV_W)), const((1, RWKV_W)), const((LANE, RWKV_W)),
            const((1, RWKV_W)), const((1, RWKV_W)), const((1, RWKV_W)), const((1, RWKV_W)), const((1, RWKV_W)),
        ],
        out_specs=[
            pl.BlockSpec((nb, L, RWKV_W), lambda g, i: (g, i, 0)),
            state_spec,
        ],
        out_shape=[
            jax.ShapeDtypeStruct((b, t, RWKV_W), BF16),
            jax.ShapeDtypeStruct((b, RWKV_HEADS, RWKV_HD, RWKV_HD), F32),
        ],
        scratch_shapes=[
            pltpu.VMEM((nb, L + SUBLANE, w3), F32),
            pltpu.VMEM((nb, L + SUBLANE, 2 * LANE), F32),
            pltpu.VMEM((nb, RWKV_GROUPS, GROUP, GROUP), F32),
        ],
        compiler_params=pltpu.CompilerParams(
            dimension_semantics=("arbitrary", "arbitrary"), vmem_limit_bytes=VMEM_LIMIT),
        name="rwkv_scan",
    )(proj3, proj3, proj3, proj3, proj3, proj3, p_r, p_k, p_v, p_xw, p_xa, s0_t,
      mu_rkv, mu_w, mu_a, row1(w0), wl_pad, row1(a0), al_pad, row1(k_k), row1(k_a), row1(r_k),
      row1(ln_w), row1(ln_b))
    return o.reshape(b * t, RWKV_W), sfin.transpose(0, 1, 3, 2)


def _merge_kernel(og_ref, or_ref, gg_ref, gr_ref, x_ref, gate_ref, wog_ref, wor_ref, wout_ref, fg_ref, o_ref,
                  *, final_norm):
    m = _sigmoid(gg_ref[...]) * _mm(og_ref[...], wog_ref[...]) \
        + _sigmoid(gr_ref[...]) * _mm(or_ref[...], wor_ref[...])
    out = _mm(m.astype(BF16), wout_ref[...])
    xn = x_ref[...] + gate_ref[...] * out
    if final_norm:
        xn = xn * lax.rsqrt(jnp.mean(xn * xn, axis=-1, keepdims=True) + EPS) * fg_ref[...]
    o_ref[...] = xn


def _merge(o_g, o_r, proj, x2d, gate, w_og, w_or, w_out, final_g, rows_per_mod, final_norm):
    m, d = x2d.shape
    tm = min(m, 256) if rows_per_mod == 1 else min(rows_per_mod, 256)
    if rows_per_mod == 1:
        gate_spec = pl.BlockSpec((tm, d), lambda i: (i, 0))
    else:
        assert rows_per_mod % tm == 0
        per = rows_per_mod // tm
        gate = gate.reshape(-1, 1, d)
        gate_spec = pl.BlockSpec((None, 1, d), lambda i: (i // per, 0, 0))
    whole = lambda shape: pl.BlockSpec(shape, lambda i: (0, 0))
    gcol = COL_GATES // d
    return pl.pallas_call(
        functools.partial(_merge_kernel, final_norm=final_norm),
        grid=(m // tm,),
        in_specs=[
            pl.BlockSpec((tm, GDN_W), lambda i: (i, 0)),
            pl.BlockSpec((tm, RWKV_W), lambda i: (i, 0)),
            pl.BlockSpec((tm, d), lambda i: (i, gcol)),
            pl.BlockSpec((tm, d), lambda i: (i, gcol + 1)),
            pl.BlockSpec((tm, d), lambda i: (i, 0)),
            gate_spec,
            whole((GDN_W, d)), whole((RWKV_W, d)), whole((d, d)), whole((1, d)),
        ],
        out_specs=pl.BlockSpec((tm, d), lambda i: (i, 0)),
        out_shape=jax.ShapeDtypeStruct((m, d), F32),
        compiler_params=pltpu.CompilerParams(
            dimension_semantics=("arbitrary",), vmem_limit_bytes=VMEM_LIMIT),
        name="merge_out",
    )(o_g, o_r, proj, proj, x2d, gate, w_og, w_or, w_out, final_g.reshape(1, d))


def _pad_in_weight(w_in):
    d = w_in.shape[0]
    o_zg = 3 * GDN_W
    o_b = o_zg + GDN_W
    o_rw = o_b + 2 * GDN_HEADS
    o_xw = o_rw + 3 * RWKV_W
    o_xa = o_xw + W_LORA
    o_zr = o_xa + A_LORA
    o_br = o_zr + RWKV_W
    z = lambda n: jnp.zeros((d, n), w_in.dtype)
    cols = [
        w_in[:, 0:o_zg], w_in[:, o_zg:o_b],
        w_in[:, o_rw:o_xw], w_in[:, o_zr:o_br],
        w_in[:, o_br:o_br + 2 * D_MODEL],
        w_in[:, o_b:o_rw], z(LANE - 2 * GDN_HEADS),
        w_in[:, o_xw:o_xa], z(LANE - W_LORA),
        w_in[:, o_xa:o_zr], z(LANE - A_LORA),
    ]
    used = COL_XA + LANE
    cols.append(z(N_PROJ - used))
    return jnp.concatenate(cols, axis=1).astype(BF16)


def _layer(x2d, batch, seq, scale, shift, gate, rows_per_mod, conv_prev, s_gdn, shift_prev, s_rwkv, p, final_g,
           final_norm, scan_nb):
    proj = _in_proj(x2d, scale, shift, p["norm_g"], p["w_in_pad"], rows_per_mod)
    proj3 = proj.reshape(batch, seq, N_PROJ)
    o_g, s_gdn_new = _gdn(proj3, conv_prev, s_gdn, p["conv_w"], p["a_log"], p["dt_bias"], p["onorm_g"], scan_nb)
    o_r, s_rwkv_new = _rwkv(proj3, shift_prev, s_rwkv, p["mu"], p["w0"], p["w_lora"], p["a0"], p["a_lora"],
                            p["k_k"], p["k_a"], p["r_k"], p["ln_w"], p["ln_b"], scan_nb)
    x_new = _merge(o_g, o_r, proj, x2d, gate, p["w_o_gdn"], p["w_o_rwkv"], p["w_out"], final_g, rows_per_mod,
                   final_norm)
    conv_new = proj3[:, seq - (CONV_W - 1):, 0:3 * GDN_W]
    last = proj3[:, seq - 1:, :]
    shift_new = jnp.concatenate(
        [last[..., COL_R * SEG:COL_R * SEG + 3 * RWKV_W], last[..., COL_XW:COL_XW + W_LORA],
         last[..., COL_XA:COL_XA + A_LORA]], axis=-1)
    return x_new, conv_new, s_gdn_new, shift_new, s_rwkv_new


def _forward(x_prompt, x_sample, c_prompt, c_sample, cache_gdn_conv, state_gdn, cache_rwkv_shift, state_rwkv,
             ada_w, ada_b, norm_g, w_in, gdn_conv_w, gdn_a_log, gdn_dt_bias, gdn_out_norm_g,
             rwkv_mu, rwkv_w0, rwkv_w_lora, rwkv_a0, rwkv_a_lora, rwkv_k_k, rwkv_k_a, rwkv_r_k,
             rwkv_ln_w, rwkv_ln_b, w_o_gdn, w_o_rwkv, w_out, final_norm_g):
    depth = ada_w.shape[0]
    bp, tp, d = x_prompt.shape
    bs, ts, _ = x_sample.shape
    assert CONV_W - 1 <= min(tp, ts)
    c_all = jnp.concatenate([c_prompt, c_sample], axis=0)
    rows = -(-(bp + bs) // SUBLANE) * SUBLANE
    c_all = jnp.pad(c_all, ((0, rows - (bp + bs)), (0, 0)))
    mod = _ada_mod(c_all, ada_w, ada_b)
    xp = x_prompt.reshape(bp * tp, d)
    xs = x_sample.reshape(bs * ts, d)
    outs = [[] for _ in range(8)]
    for l in range(depth):
        p = dict(norm_g=norm_g[l], w_in_pad=_pad_in_weight(w_in[l]), conv_w=gdn_conv_w[l], a_log=gdn_a_log[l],
                 dt_bias=gdn_dt_bias[l], onorm_g=gdn_out_norm_g[l], mu=rwkv_mu[l], w0=rwkv_w0[l],
                 w_lora=rwkv_w_lora[l], a0=rwkv_a0[l], a_lora=rwkv_a_lora[l], k_k=rwkv_k_k[l], k_a=rwkv_k_a[l],
                 r_k=rwkv_r_k[l].reshape(-1), ln_w=rwkv_ln_w[l], ln_b=rwkv_ln_b[l],
                 w_o_gdn=w_o_gdn[l].astype(BF16), w_o_rwkv=w_o_rwkv[l].astype(BF16), w_out=w_out[l].astype(BF16))
        last = l == depth - 1
        m_p = mod[l, 0:bp]
        m_s = jnp.repeat(mod[l, bp:bp + bs], ts, axis=0)
        sh_p, sc_p, gt_p = m_p[:, 0:d], m_p[:, d:2 * d], m_p[:, 2 * d:]
        sh_s, sc_s, gt_s = m_s[:, 0:d], m_s[:, d:2 * d], m_s[:, 2 * d:]
        zeros = lambda *s: jnp.zeros(s, F32)
        xp, c1, g1, h1, r1 = _layer(
            xp, bp, tp, sc_p, sh_p, gt_p, tp,
            zeros(bp, CONV_W - 1, 3 * GDN_W), zeros(bp, GDN_HEADS, GDN_DK, GDN_DV),
            zeros(bp, 1, 3 * RWKV_W + W_LORA + A_LORA), zeros(bp, RWKV_HEADS, RWKV_HD, RWKV_HD),
            p, final_norm_g, last, scan_nb=bp)
        xs, c2, g2, h2, r2 = _layer(
            xs, bs, ts, sc_s, sh_s, gt_s, 1,
            cache_gdn_conv[l], state_gdn[l], cache_rwkv_shift[l], state_rwkv[l],
            p, final_norm_g, last, scan_nb=min(bs, 2))
        for lst, val in zip(outs, (c1, g1, h1, r1, c2, g2, h2, r2)):
            lst.append(val)
    stk = [jnp.stack(v) for v in outs]
    return (xp.reshape(bp, tp, d), xs.reshape(bs, ts, d), *stk)


def kernel(x_prompt, x_sample, c_prompt, c_sample, cache_gdn_conv, state_gdn, cache_rwkv_shift, state_rwkv, ada_w, ada_b, norm_g, w_in, gdn_conv_w, gdn_a_log, gdn_dt_bias, gdn_out_norm_g, rwkv_mu, rwkv_w0, rwkv_w_lora, rwkv_a0, rwkv_a_lora, rwkv_k_k, rwkv_k_a, rwkv_r_k, rwkv_ln_w, rwkv_ln_b, w_o_gdn, w_o_rwkv, w_out, final_norm_g):
    return _forward(x_prompt, x_sample, c_prompt, c_sample, cache_gdn_conv, state_gdn, cache_rwkv_shift,
                    state_rwkv, ada_w, ada_b, norm_g, w_in, gdn_conv_w, gdn_a_log, gdn_dt_bias, gdn_out_norm_g,
                    rwkv_mu, rwkv_w0, rwkv_w_lora, rwkv_a0, rwkv_a_lora, rwkv_k_k, rwkv_k_a, rwkv_r_k,
                    rwkv_ln_w, rwkv_ln_b, w_o_gdn, w_o_rwkv, w_out, final_norm_g)
```

```python
import functools
import math

import jax
import jax.numpy as jnp
from jax import lax
from jax.experimental import pallas as pl
from jax.experimental.pallas import tpu as pltpu

F32 = jnp.float32
BF16 = jnp.bfloat16

D_MODEL = 2048
GDN_HEADS = 8
GDN_DK = 128
GDN_DV = 128
GDN_W = GDN_HEADS * GDN_DK
GDN_CHUNK = 64
CONV_W = 4
RWKV_HEADS = 16
RWKV_HD = 64
RWKV_W = RWKV_HEADS * RWKV_HD
W_LORA = 96
A_LORA = 96
EPS = 1e-6
GN_EPS = 64e-5
W_DECAY_OFFSET = 0.5

LANE = 128
SUBLANE = 8
SEG = 1024
COL_Q, COL_K, COL_V, COL_ZG, COL_R, COL_RK, COL_RV, COL_ZR = 0, 1, 2, 3, 4, 5, 6, 7
COL_GATES = 8 * SEG
COL_BA = 12 * SEG
COL_XW = COL_BA + LANE
COL_XA = COL_XW + LANE
GROUP = 256
PROJ_TILE = 5 * GROUP
N_PROJ = 10 * PROJ_TILE
PACK = 4
RWKV_GROUPS = RWKV_W // GROUP
GDN_PAIRS = GDN_W // GROUP
VMEM_LIMIT = 56 * 1024 * 1024


def _mm(a, b):
    return jnp.dot(a, b, preferred_element_type=F32)


def _mm_nt(a, b):
    return lax.dot_general(a, b, (((1,), (1,)), ((), ())), preferred_element_type=F32)


def _mm_tn(a, b):
    return lax.dot_general(a, b, (((0,), (0,)), ((), ())), preferred_element_type=F32)


def _split_bf16(x):
    hi = x.astype(BF16)
    return hi, (x - hi.astype(F32)).astype(BF16)


def _split3_bf16(x):
    hi = x.astype(BF16)
    mid, lo = _split_bf16(x - hi.astype(F32))
    return hi, mid, lo


def _sigmoid(x):
    return 1.0 / (1.0 + jnp.exp(-x))


def _silu(x):
    return x * _sigmoid(x)


def _softplus(x):
    return jnp.maximum(x, 0.0) + jnp.log1p(jnp.exp(-jnp.abs(x)))


def _iota(shape, dim):
    return lax.broadcasted_iota(jnp.int32, shape, dim)


def _bits(n):
    assert n & (n - 1) == 0
    return n.bit_length() - 1


def _mod_kernel(c_ref, w_ref, b_ref, o_ref):
    o_ref[...] = _mm(c_ref[...], w_ref[...]) + b_ref[...]


def _ada_mod(c_all, ada_w, ada_b):
    depth, d, n3 = ada_w.shape
    rows = c_all.shape[0]
    tn = 512
    return pl.pallas_call(
        _mod_kernel,
        grid=(depth, n3 // tn),
        in_specs=[
            pl.BlockSpec((rows, d), lambda l, j: (0, 0)),
            pl.BlockSpec((None, d, tn), lambda l, j: (l, 0, j)),
            pl.BlockSpec((None, 1, tn), lambda l, j: (l, 0, j)),
        ],
        out_specs=pl.BlockSpec((None, rows, tn), lambda l, j: (l, 0, j)),
        out_shape=jax.ShapeDtypeStruct((depth, rows, n3), F32),
        compiler_params=pltpu.CompilerParams(
            dimension_semantics=("arbitrary", "arbitrary"), vmem_limit_bytes=VMEM_LIMIT),
        name="ada_mod",
    )(c_all, ada_w, ada_b.reshape(depth, 1, n3))


def _in_proj_kernel(x_ref, sc_ref, sh_ref, g_ref, w_ref, o_ref, h_ref):
    @pl.when(pl.program_id(1) == 0)
    def _():
        x = x_ref[...]
        ms = jnp.mean(x * x, axis=-1, keepdims=True)
        h = x * lax.rsqrt(ms + EPS) * g_ref[...] * (1.0 + sc_ref[...]) + sh_ref[...]
        h_ref[...] = h.astype(BF16)

    o_ref[...] = _mm(h_ref[...], w_ref[...])


def _in_proj(x2d, scale, shift, norm_g, w_pad, rows_per_mod):
    m, d = x2d.shape
    tm = min(m, 512) if rows_per_mod == 1 else min(rows_per_mod, 1024)
    tn = PROJ_TILE
    if rows_per_mod == 1:
        mod_spec = pl.BlockSpec((tm, d), lambda i, j: (i, 0))
    else:
        assert rows_per_mod % tm == 0
        per = rows_per_mod // tm
        scale = scale.reshape(-1, 1, d)
        shift = shift.reshape(-1, 1, d)
        mod_spec = pl.BlockSpec((None, 1, d), lambda i, j: (i // per, 0, 0))
    return pl.pallas_call(
        _in_proj_kernel,
        grid=(m // tm, N_PROJ // tn),
        in_specs=[
            pl.BlockSpec((tm, d), lambda i, j: (i, 0)),
            mod_spec, mod_spec,
            pl.BlockSpec((1, d), lambda i, j: (0, 0)),
            pl.BlockSpec((d, tn), lambda i, j: (0, j)),
        ],
        out_specs=pl.BlockSpec((tm, tn), lambda i, j: (i, j)),
        out_shape=jax.ShapeDtypeStruct((m, N_PROJ), F32),
        scratch_shapes=[pltpu.VMEM((tm, d), BF16)],
        compiler_params=pltpu.CompilerParams(
            dimension_semantics=("arbitrary", "arbitrary"), vmem_limit_bytes=VMEM_LIMIT),
        name="in_proj",
    )(x2d, scale, shift, norm_g.reshape(1, d), w_pad)


def _tile_rows(x, reps):
    return jnp.concatenate([x] * reps, axis=0)


def _head_mask(rows, row_bits, cols, col_bits):
    return (_iota((rows, cols), 0) >> row_bits) == (_iota((rows, cols), 1) >> col_bits)


def _block_sums(xs, gmat_b):
    t, width = xs[0].shape
    ng = width // GROUP
    parts = []
    for x in xs:
        xb = x.astype(BF16)
        parts += [xb[:, g * GROUP:(g + 1) * GROUP] for g in range(ng)]
    res = _mm(jnp.concatenate(parts, axis=0), gmat_b)
    return [jnp.concatenate([res[(i * ng + g) * t:(i * ng + g + 1) * t] for g in range(ng)], axis=1)
            for i in range(len(xs))]


def _cumsum_rows(x, tri_b):
    w = x.shape[1]
    res = _mm(tri_b, jnp.concatenate(_split3_bf16(x), axis=1))
    return res[:, 0:w] + (res[:, w:2 * w] + res[:, 2 * w:])


def _unit_lower_inverses(e_mats, eye, m_pp_b):
    L, pw_w = e_mats[0].shape
    steps = _bits(L)
    pws = [-e for e in e_mats]
    t_invs = [eye + pw for pw in pws]
    for s in range(steps):
        first, last = s == 0, s == steps - 1
        if first and last:
            break
        for c in range(len(e_mats)):
            p_hi, p_lo = _split_bf16(pws[c])
            bd_hi = _tile_rows(p_hi, PACK) * m_pp_b
            bd_lo = _tile_rows(p_lo, PACK) * m_pp_b
            rows = ([] if first else [_split_bf16(t_invs[c])]) + ([] if last else [(p_hi, p_lo)])
            x_hi = jnp.concatenate([hi for hi, _ in rows], axis=0)
            x_lo = jnp.concatenate([lo for _, lo in rows], axis=0)
            nr = x_hi.shape[0]
            hh = _mm(jnp.concatenate([x_hi, x_lo], axis=0), bd_hi)
            res = hh[0:nr] + (hh[nr:] + _mm(x_hi, bd_lo))
            if not first:
                t_invs[c] = t_invs[c] + res[0:L]
            if not last:
                pws[c] = res[nr - L:]
        yield
    return t_invs


def _gdn_body(q_ref, k_ref, v_ref, z_ref, ba_ref, cw_ref, prev_ref, s0_ref, alog_ref, dtb_ref, on_ref,
              o_ref, sfin_ref, ubuf, sb, *, nb, chunk, nsteps):
    L = chunk
    i = pl.program_id(1)
    l_bits = _bits(L)
    dk_bits = _bits(GDN_DK)
    pw_w = PACK * L
    hw = PACK * GDN_DK
    n_grp = GDN_HEADS // PACK
    hp = GROUP // GDN_DK
    dsl = lambda h: slice((h % hp) * GDN_DK, (h % hp + 1) * GDN_DK)

    m_pd_b = _head_mask(pw_w, l_bits, hw, dk_bits).astype(BF16)
    m_pp_b = _head_mask(pw_w, l_bits, pw_w, l_bits).astype(BF16)
    m_dd = _head_mask(GROUP, dk_bits, GROUP, dk_bits)
    m_dd_b = m_dd.astype(BF16)
    t_row = _iota((L, pw_w), 0)
    j_lane = _iota((L, pw_w), 1) & (L - 1)
    strict = j_lane < t_row
    incl = j_lane <= t_row
    eye = (j_lane == t_row).astype(F32)
    tri_b = (_iota((L, L), 0) >= _iota((L, L), 1)).astype(BF16)
    ones_b = jnp.ones((L, L), BF16)
    lane = _iota((L, LANE), 1)
    x_rows = _iota((LANE, GDN_HEADS * L + GDN_W), 0) & (GDN_HEADS - 1)
    x_cols = _iota((LANE, GDN_HEADS * L + GDN_W), 1)
    x_head = jnp.where(x_cols < GDN_HEADS * L, x_cols >> l_bits, (x_cols - GDN_HEADS * L) >> dk_bits)
    expand_b = ((x_rows == x_head) & (_iota((LANE, GDN_HEADS * L + GDN_W), 0) < 2 * GDN_HEADS)).astype(BF16)

    def bdiag(x):
        return _tile_rows(x.astype(BF16), PACK) * m_pd_b

    @pl.when(i == 0)
    def _():
        sb[...] = jnp.zeros(sb.shape, F32)
        for n in range(nb):
            ubuf[n, 0:SUBLANE, :] = prev_ref[n]
            for h in range(GDN_HEADS):
                sb[n, h // hp, dsl(h), dsl(h)] = s0_ref[n, h]

    base = SUBLANE - (CONV_W - 1)
    pre = []
    for n in range(nb):
        ubuf[n, SUBLANE:SUBLANE + L, 0:GDN_W] = q_ref[n]
        ubuf[n, SUBLANE:SUBLANE + L, GDN_W:2 * GDN_W] = k_ref[n]
        ubuf[n, SUBLANE:SUBLANE + L, 2 * GDN_W:3 * GDN_W] = v_ref[n]
        y = ubuf[n, base:base + L, :] * cw_ref[0:1, :]
        for j in range(1, CONV_W):
            y = y + ubuf[n, base + j:base + j + L, :] * cw_ref[j:j + 1, :]
        ubuf[n, 0:SUBLANE, :] = ubuf[n, L:L + SUBLANE, :]
        act = _silu(y)
        q = act[:, 0:GDN_W]
        k = act[:, GDN_W:2 * GDN_W]
        v = act[:, 2 * GDN_W:]
        ssq, ssk = _block_sums([q * q, k * k], m_dd_b)
        q = q * lax.rsqrt(ssq + EPS) * (GDN_DK ** -0.5)
        k = k * lax.rsqrt(ssk + EPS)

        ba = ba_ref[n]
        beta_c = jnp.where(lane < GDN_HEADS, _sigmoid(ba), 0.0)
        g_c = -jnp.exp(alog_ref[...]) * _softplus(ba + dtb_ref[...])
        gc_c = jnp.where((lane >= GDN_HEADS) & (lane < 2 * GDN_HEADS), _cumsum_rows(g_c, tri_b), 0.0)
        pieces = jnp.concatenate(list(_split_bf16(beta_c)) + list(_split3_bf16(gc_c)), axis=0)
        ex = _mm(pieces, expand_b)
        beta_x = ex[0:L] + ex[L:2 * L]
        gc_x = ex[2 * L:3 * L] + (ex[3 * L:4 * L] + ex[4 * L:])
        beta_p, beta_w = beta_x[:, 0:GDN_HEADS * L], beta_x[:, GDN_HEADS * L:]
        gc_p, gc_w = gc_x[:, 0:GDN_HEADS * L], gc_x[:, GDN_HEADS * L:]
        eye8 = jnp.concatenate([eye] * n_grp, axis=1)
        incl8 = jnp.concatenate([incl] * n_grp, axis=1)
        rw = _mm(ones_b, jnp.concatenate(_split3_bf16(gc_p * eye8), axis=1))
        pw8 = GDN_HEADS * L
        gc_row = rw[:, 0:pw8] + (rw[:, pw8:2 * pw8] + rw[:, 2 * pw8:])
        dm = jnp.where(incl8, jnp.exp(jnp.where(incl8, gc_p - gc_row, 0.0)), 0.0)
        gl_w = gc_w[L - 1:L, :]
        eg_w = jnp.exp(gc_w)
        qd = q * eg_w
        kd = k * jnp.exp(gl_w - gc_w)
        bv = beta_w * v
        bek = beta_w * eg_w * k
        pre.append(dict(q=q, k=k, beta_p=beta_p, dm=dm, qd=qd, kd=kd, bv=bv, bek=bek, egl_w=jnp.exp(gl_w)))
        yield

    chains = [(n, g) for g in range(n_grp) for n in range(nb)]
    wsl = lambda g: slice(g * hw, (g + 1) * hw)
    psl = lambda g: slice(g * pw_w, (g + 1) * pw_w)
    ppg = hw // GROUP
    a_mat, qkd = {}, {}
    for c in chains:
        p, sw, sp = pre[c[0]], wsl(c[1]), psl(c[1])
        prod = _mm_nt(jnp.concatenate([p["k"][:, sw], p["q"][:, sw]], axis=0).astype(BF16), bdiag(p["k"][:, sw]))
        a_mat[c] = jnp.where(strict, p["beta_p"][:, sp] * prod[0:L] * p["dm"][:, sp], 0.0)
        qkd[c] = prod[L:] * p["dm"][:, sp]
    yield
    t_inv = dict(zip(chains, (yield from _unit_lower_inverses([a_mat[c] for c in chains], eye, m_pp_b))))
    sol = {}
    for c in chains:
        p, sw = pre[c[0]], wsl(c[1])
        sol[c] = _mm(t_inv[c].astype(BF16), jnp.concatenate([bdiag(p["bv"][:, sw]), bdiag(p["bek"][:, sw])], axis=1))
    yield
    pairs = [(c, j) for c in chains for j in range(ppg)]
    s_p, r2, u_p = {}, {}, {}
    for c, j in pairs:
        pair = c[1] * ppg + j
        sg = slice(pair * GROUP, (pair + 1) * GROUP)
        s_p[c, j] = sb[c[0], pair]
        r2[c, j] = _mm(jnp.concatenate([sol[c][:, hw + j * GROUP:hw + (j + 1) * GROUP], pre[c[0]]["qd"][:, sg]],
                                       axis=0).astype(BF16), s_p[c, j].astype(BF16))
    yield
    for c, j in pairs:
        pair = c[1] * ppg + j
        sg = slice(pair * GROUP, (pair + 1) * GROUP)
        u_p[c, j] = sol[c][:, j * GROUP:(j + 1) * GROUP] - r2[c, j][0:L]
        upd = _mm_tn(pre[c[0]]["kd"][:, sg].astype(BF16), u_p[c, j].astype(BF16))
        sb[c[0], pair] = s_p[c, j] * pre[c[0]]["egl_w"][:, sg] + jnp.where(m_dd, upd, 0.0)
    yield
    o_g = {}
    for c in chains:
        u = jnp.concatenate([u_p[c, j] for j in range(ppg)], axis=1)
        qs = jnp.concatenate([r2[c, j][L:] for j in range(ppg)], axis=1)
        o_g[c] = qs + _mm(qkd[c].astype(BF16), bdiag(u))
    yield

    for n in range(nb):
        o = jnp.concatenate([o_g[(n, g)] for g in range(n_grp)], axis=1)
        sso, = _block_sums([o * o], m_dd_b)
        on = o * lax.rsqrt(sso * (1.0 / GDN_DV) + EPS) * on_ref[...]
        o_ref[n] = (on * _silu(z_ref[n])).astype(o_ref.dtype)

    @pl.when(i == nsteps - 1)
    def _():
        for n in range(nb):
            for h in range(GDN_HEADS):
                sfin_ref[n, h] = sb[n, h // hp, dsl(h), dsl(h)]


def _rwkv_body(r_ref, k_ref, v_ref, xw_ref, xa_ref, z_ref, pr_ref, pk_ref, pv_ref, pxw_ref, pxa_ref, s0_ref,
               mu_ref, muw_ref, mua_ref, w0_ref, wl_ref, a0_ref, al_ref, kk_ref, ka_ref, rk_ref,
               lnw_ref, lnb_ref,
               o_ref, sfin_ref,
               sbuf, lbuf, sbt, *, nb, chunk, nsteps):
    i = pl.program_id(1)
    L = chunk
    pw_w = PACK * L
    l_bits = _bits(L)
    hd_bits = _bits(RWKV_HD)

    m_pk_b = _head_mask(pw_w, l_bits, GROUP, hd_bits).astype(BF16)
    m_pp_b = _head_mask(pw_w, l_bits, pw_w, l_bits).astype(BF16)
    m_kk = _head_mask(GROUP, hd_bits, GROUP, hd_bits)
    m_kk_b = m_kk.astype(BF16)
    t_row = _iota((L, pw_w), 0)
    j_lane = _iota((L, pw_w), 1) & (L - 1)
    strict = j_lane < t_row
    incl = j_lane <= t_row
    eye = (j_lane == t_row).astype(F32)
    tri_b = (_iota((L, L), 0) >= _iota((L, L), 1)).astype(BF16)
    hsl = lambda h: slice((h % PACK) * RWKV_HD, (h % PACK + 1) * RWKV_HD)

    def bdiag(x):
        return _tile_rows(x.astype(BF16), PACK) * m_pk_b

    @pl.when(i == 0)
    def _():
        sbt[...] = jnp.zeros(sbt.shape, F32)
        for n in range(nb):
            for h in range(RWKV_HEADS):
                sbt[n, h // PACK, hsl(h), hsl(h)] = s0_ref[n, h]
            sbuf[n, SUBLANE - 1:SUBLANE, 0:SEG] = pr_ref[n]
            sbuf[n, SUBLANE - 1:SUBLANE, SEG:2 * SEG] = pk_ref[n]
            sbuf[n, SUBLANE - 1:SUBLANE, 2 * SEG:3 * SEG] = pv_ref[n]
            lbuf[n, SUBLANE - 1:SUBLANE, 0:LANE] = pxw_ref[n]
            lbuf[n, SUBLANE - 1:SUBLANE, LANE:2 * LANE] = pxa_ref[n]

    inv_hd = 1.0 / RWKV_HD
    pre = []
    for n in range(nb):
        sbuf[n, SUBLANE:SUBLANE + L, 0:SEG] = r_ref[n]
        sbuf[n, SUBLANE:SUBLANE + L, SEG:2 * SEG] = k_ref[n]
        sbuf[n, SUBLANE:SUBLANE + L, 2 * SEG:3 * SEG] = v_ref[n]
        lbuf[n, SUBLANE:SUBLANE + L, 0:LANE] = xw_ref[n]
        lbuf[n, SUBLANE:SUBLANE + L, LANE:2 * LANE] = xa_ref[n]
        cur = sbuf[n, SUBLANE:SUBLANE + L, :]
        prv = sbuf[n, SUBLANE - 1:SUBLANE - 1 + L, :]
        rkv = cur + (prv - cur) * mu_ref[...]
        curl = lbuf[n, SUBLANE:SUBLANE + L, :]
        prvl = lbuf[n, SUBLANE - 1:SUBLANE - 1 + L, :]
        xw = curl[:, 0:LANE] + (prvl[:, 0:LANE] - curl[:, 0:LANE]) * muw_ref[...]
        xa = curl[:, LANE:] + (prvl[:, LANE:] - curl[:, LANE:]) * mua_ref[...]
        sbuf[n, 0:SUBLANE, :] = sbuf[n, L:L + SUBLANE, :]
        lbuf[n, 0:SUBLANE, :] = lbuf[n, L:L + SUBLANE, :]
        r = rkv[:, 0:SEG]
        kr = rkv[:, SEG:2 * SEG]
        vr = rkv[:, 2 * SEG:3 * SEG]
        wl = w0_ref[...] + _mm(jnp.tanh(xw).astype(BF16), wl_ref[...])
        logw = -math.exp(-W_DECAY_OFFSET) * _sigmoid(wl)
        a = _sigmoid(a0_ref[...] + _mm(xa.astype(BF16), al_ref[...]))
        kkr = kr * kk_ref[...]
        kt = kr * (1.0 + (a - 1.0) * ka_ref[...])
        ss, bonus = _block_sums([kkr * kkr, r * kt * rk_ref[...]], m_kk_b)
        kk = kkr * lax.rsqrt(ss + EPS)
        ah = a * kk
        cum = _cumsum_rows(logw, tri_b)
        c_last = cum[L - 1:L, :]
        e_neg = jnp.exp(-cum)
        e_last = jnp.exp(c_last - cum)
        kx = kk * jnp.exp(cum - logw)
        rx = r * jnp.exp(cum)
        kb = kt * e_neg
        ab = ah * e_neg
        kh = kt * e_last
        ahh = ah * e_last
        pre.append(dict(vr=vr, bonus=bonus, kx=kx, rx=rx, kb=kb, ab=ab, kh=kh, ahh=ahh, g_last=jnp.exp(c_last)))
        yield

    chains = [(n, g) for g in range(RWKV_GROUPS) for n in range(nb)]
    gsl = lambda g: slice(g * GROUP, (g + 1) * GROUP)
    lhs, c_mat, e_mat, rk_mat, ra_mat = {}, {}, {}, {}, {}
    for c in chains:
        p, sl = pre[c[0]], gsl(c[1])
        lhs[c] = jnp.concatenate([p["kx"][:, sl], p["rx"][:, sl]], axis=0).astype(BF16)
        rhs_nt = jnp.concatenate([bdiag(p["kb"][:, sl]), bdiag(p["ab"][:, sl])], axis=0)
        prod = _mm_nt(lhs[c], rhs_nt)
        c_mat[c] = jnp.where(strict, prod[0:L, 0:pw_w], 0.0)
        e_mat[c] = jnp.where(strict, prod[0:L, pw_w:], 0.0)
        rk_mat[c] = jnp.where(incl, prod[L:, 0:pw_w], 0.0)
        ra_mat[c] = jnp.where(incl, prod[L:, pw_w:], 0.0)
    yield
    t_inv = dict(zip(chains, (yield from _unit_lower_inverses([e_mat[c] for c in chains], eye, m_pp_b))))
    s_g, ks_rs, cv, u = {}, {}, {}, {}
    for c in chains:
        s_g[c] = sbt[c[0], c[1]]
        ks_rs[c] = _mm_nt(lhs[c], s_g[c].astype(BF16))
        cv[c] = _mm(jnp.concatenate([c_mat[c], rk_mat[c]], axis=0).astype(BF16),
                    bdiag(pre[c[0]]["vr"][:, gsl(c[1])]))
    yield
    for c in chains:
        u[c] = _mm(t_inv[c].astype(BF16), bdiag(ks_rs[c][0:L] + cv[c][0:L]))
    yield
    y_g = {}
    for c in chains:
        p, sl = pre[c[0]], gsl(c[1])
        y_g[c] = ks_rs[c][L:] + cv[c][L:] - _mm(ra_mat[c].astype(BF16), bdiag(u[c]))
        upd = _mm_tn(jnp.concatenate([p["vr"][:, sl], -u[c]], axis=0).astype(BF16),
                     jnp.concatenate([p["kh"][:, sl], p["ahh"][:, sl]], axis=0).astype(BF16))
        sbt[c[0], c[1]] = s_g[c] * p["g_last"][:, sl] + jnp.where(m_kk, upd, 0.0)
    yield

    for n in range(nb):
        y = jnp.concatenate([y_g[(n, g)] for g in range(RWKV_GROUPS)], axis=1)
        dlt = y - _block_sums([y], m_kk_b)[0] * inv_hd
        var = _block_sums([dlt * dlt], m_kk_b)[0] * inv_hd
        yn = dlt * lax.rsqrt(var + GN_EPS) * lnw_ref[...] + lnb_ref[...]
        o_ref[n] = ((yn + pre[n]["bonus"] * pre[n]["vr"]) * _silu(z_ref[n])).astype(o_ref.dtype)

    @pl.when(i == nsteps - 1)
    def _():
        for n in range(nb):
            for h in range(RWKV_HEADS):
                sfin_ref[n, h] = sbt[n, h // PACK, hsl(h), hsl(h)]


N_GDN_IN, N_RWKV_IN = 11, 24


def _scan_kernel(*refs, nb, chunk, nsteps):
    g_in = refs[0:N_GDN_IN]
    r_in = refs[N_GDN_IN:N_GDN_IN + N_RWKV_IN]
    og_ref, sg_ref, or_ref, sr_ref, ubuf, sb, sbuf, lbuf, sbt = refs[N_GDN_IN + N_RWKV_IN:]
    kw = dict(nb=nb, chunk=chunk, nsteps=nsteps)
    gens = [_gdn_body(*g_in, og_ref, sg_ref, ubuf, sb, **kw), _rwkv_body(*r_in, or_ref, sr_ref, sbuf, lbuf, sbt, **kw)]
    while gens:
        for gen in list(gens):
            try:
                next(gen)
            except StopIteration:
                gens.remove(gen)


def _scans(proj3, conv_prev, s_gdn, shift_prev, s_rwkv, p, nb):
    b, t, _ = proj3.shape
    L = min(GDN_CHUNK, t)
    assert b % nb == 0 and t % L == 0 and L % SUBLANE == 0
    nsteps = t // L
    blk = lambda c, w: pl.BlockSpec((nb, L, w), lambda g, i, c=c: (g, i, c))
    const = lambda shape: pl.BlockSpec(shape, lambda g, i: (0,) * len(shape))
    per_seq = lambda *shape: pl.BlockSpec((nb,) + shape, lambda g, i: (g,) + (0,) * len(shape))
    prev_pad = jnp.pad(conv_prev, ((0, 0), (SUBLANE - (CONV_W - 1), 0), (0, 0)))
    lane_pad = (GDN_HEADS, LANE - 2 * GDN_HEADS)
    alog_row = jnp.pad(p["a_log"], lane_pad).reshape(1, LANE)
    dtb_row = jnp.pad(p["dt_bias"], lane_pad).reshape(1, LANE)
    on_row = jnp.tile(p["onorm_g"], GDN_HEADS).reshape(1, GDN_W)
    gdn_state = per_seq(GDN_HEADS, GDN_DK, GDN_DV)
    gdn_specs = [
        blk(COL_Q, SEG), blk(COL_K, SEG), blk(COL_V, SEG), blk(COL_ZG, SEG), blk(COL_BA // LANE, LANE),
        const((CONV_W, 3 * GDN_W)), per_seq(SUBLANE, 3 * GDN_W), gdn_state,
        const((1, LANE)), const((1, LANE)), const((1, GDN_W)),
    ]
    gdn_args = [proj3] * 5 + [p["conv_w"], prev_pad, s_gdn, alog_row, dtb_row, on_row]
    w3 = 3 * RWKV_W
    padl = lambda x, n: jnp.pad(x, [(0, 0)] * (x.ndim - 1) + [(0, LANE - n)])
    p_r, p_k, p_v = shift_prev[..., 0:RWKV_W], shift_prev[..., RWKV_W:2 * RWKV_W], shift_prev[..., 2 * RWKV_W:w3]
    p_xw = padl(shift_prev[..., w3:w3 + W_LORA], W_LORA)
    p_xa = padl(shift_prev[..., w3 + W_LORA:], A_LORA)
    mu = p["mu"]
    mu_rkv = mu[0:w3].reshape(1, w3)
    mu_w = padl(mu[w3:w3 + W_LORA], W_LORA).reshape(1, LANE)
    mu_a = padl(mu[w3 + W_LORA:], A_LORA).reshape(1, LANE)
    wl_pad = jnp.pad(p["w_lora"], ((0, LANE - W_LORA), (0, 0))).astype(BF16)
    al_pad = jnp.pad(p["a_lora"], ((0, LANE - A_LORA), (0, 0))).astype(BF16)
    s0_t = s_rwkv.transpose(0, 1, 3, 2)
    row1 = lambda x: x.reshape(1, RWKV_W)
    rwkv_state = per_seq(RWKV_HEADS, RWKV_HD, RWKV_HD)
    rwkv_specs = [
        blk(COL_R, SEG), blk(COL_RK, SEG), blk(COL_RV, SEG),
        blk(COL_XW // LANE, LANE), blk(COL_XA // LANE, LANE), blk(COL_ZR, SEG),
        per_seq(1, SEG), per_seq(1, SEG), per_seq(1, SEG), per_seq(1, LANE), per_seq(1, LANE),
        rwkv_state,
        const((1, w3)), const((1, LANE)), const((1, LANE)),
        const((1, RWKV_W)), const((LANE, RWKV_W)), const((1, RWKV_W)), const((LANE, RWKV_W)),
        const((1, RWKV_W)), const((1, RWKV_W)), const((1, RWKV_W)), const((1, RWKV_W)), const((1, RWKV_W)),
    ]
    rwkv_args = [proj3] * 6 + [p_r, p_k, p_v, p_xw, p_xa, s0_t, mu_rkv, mu_w, mu_a, row1(p["w0"]), wl_pad,
                               row1(p["a0"]), al_pad, row1(p["k_k"]), row1(p["k_a"]), row1(p["r_k"]),
                               row1(p["ln_w"]), row1(p["ln_b"])]
    assert len(gdn_specs) == N_GDN_IN and len(rwkv_specs) == N_RWKV_IN
    o_g, sg, o_r, sr = pl.pallas_call(
        functools.partial(_scan_kernel, nb=nb, chunk=L, nsteps=nsteps),
        grid=(b // nb, nsteps),
        in_specs=gdn_specs + rwkv_specs,
        out_specs=[
            pl.BlockSpec((nb, L, GDN_W), lambda g, i: (g, i, 0)), gdn_state,
            pl.BlockSpec((nb, L, RWKV_W), lambda g, i: (g, i, 0)), rwkv_state,
        ],
        out_shape=[
            jax.ShapeDtypeStruct((b, t, GDN_W), BF16),
            jax.ShapeDtypeStruct((b, GDN_HEADS, GDN_DK, GDN_DV), F32),
            jax.ShapeDtypeStruct((b, t, RWKV_W), BF16),
            jax.ShapeDtypeStruct((b, RWKV_HEADS, RWKV_HD, RWKV_HD), F32),
        ],
        scratch_shapes=[
            pltpu.VMEM((nb, L + SUBLANE, 3 * GDN_W), F32),
            pltpu.VMEM((nb, GDN_PAIRS, GROUP, GROUP), F32),
            pltpu.VMEM((nb, L + SUBLANE, w3), F32),
            pltpu.VMEM((nb, L + SUBLANE, 2 * LANE), F32),
            pltpu.VMEM((nb, RWKV_GROUPS, GROUP, GROUP), F32),
        ],
        compiler_params=pltpu.CompilerParams(
            dimension_semantics=("arbitrary", "arbitrary"), vmem_limit_bytes=VMEM_LIMIT),
        name="scan",
    )(*gdn_args, *rwkv_args)
    return o_g.reshape(b * t, GDN_W), sg, o_r.reshape(b * t, RWKV_W), sr.transpose(0, 1, 3, 2)


def _merge_kernel(og_ref, or_ref, gg_ref, gr_ref, x_ref, gate_ref, wog_ref, wor_ref, wout_ref, fg_ref, o_ref,
                  *, final_norm):
    m = _sigmoid(gg_ref[...]) * _mm(og_ref[...], wog_ref[...]) \
        + _sigmoid(gr_ref[...]) * _mm(or_ref[...], wor_ref[...])
    out = _mm(m.astype(BF16), wout_ref[...])
    xn = x_ref[...] + gate_ref[...] * out
    if final_norm:
        xn = xn * lax.rsqrt(jnp.mean(xn * xn, axis=-1, keepdims=True) + EPS) * fg_ref[...]
    o_ref[...] = xn


def _merge(o_g, o_r, proj, x2d, gate, w_og, w_or, w_out, final_g, rows_per_mod, final_norm):
    m, d = x2d.shape
    tm = min(m, 256) if rows_per_mod == 1 else min(rows_per_mod, 256)
    if rows_per_mod == 1:
        gate_spec = pl.BlockSpec((tm, d), lambda i: (i, 0))
    else:
        assert rows_per_mod % tm == 0
        per = rows_per_mod // tm
        gate = gate.reshape(-1, 1, d)
        gate_spec = pl.BlockSpec((None, 1, d), lambda i: (i // per, 0, 0))
    whole = lambda shape: pl.BlockSpec(shape, lambda i: (0, 0))
    gcol = COL_GATES // d
    return pl.pallas_call(
        functools.partial(_merge_kernel, final_norm=final_norm),
        grid=(m // tm,),
        in_specs=[
            pl.BlockSpec((tm, GDN_W), lambda i: (i, 0)),
            pl.BlockSpec((tm, RWKV_W), lambda i: (i, 0)),
            pl.BlockSpec((tm, d), lambda i: (i, gcol)),
            pl.BlockSpec((tm, d), lambda i: (i, gcol + 1)),
            pl.BlockSpec((tm, d), lambda i: (i, 0)),
            gate_spec,
            whole((GDN_W, d)), whole((RWKV_W, d)), whole((d, d)), whole((1, d)),
        ],
        out_specs=pl.BlockSpec((tm, d), lambda i: (i, 0)),
        out_shape=jax.ShapeDtypeStruct((m, d), F32),
        compiler_params=pltpu.CompilerParams(
            dimension_semantics=("arbitrary",), vmem_limit_bytes=VMEM_LIMIT),
        name="merge_out",
    )(o_g, o_r, proj, proj, x2d, gate, w_og, w_or, w_out, final_g.reshape(1, d))


def _pad_in_weight(w_in):
    d = w_in.shape[0]
    o_zg = 3 * GDN_W
    o_b = o_zg + GDN_W
    o_rw = o_b + 2 * GDN_HEADS
    o_xw = o_rw + 3 * RWKV_W
    o_xa = o_xw + W_LORA
    o_zr = o_xa + A_LORA
    o_br = o_zr + RWKV_W
    z = lambda n: jnp.zeros((d, n), w_in.dtype)
    cols = [
        w_in[:, 0:o_zg], w_in[:, o_zg:o_b],
        w_in[:, o_rw:o_xw], w_in[:, o_zr:o_br],
        w_in[:, o_br:o_br + 2 * D_MODEL],
        w_in[:, o_b:o_rw], z(LANE - 2 * GDN_HEADS),
        w_in[:, o_xw:o_xa], z(LANE - W_LORA),
        w_in[:, o_xa:o_zr], z(LANE - A_LORA),
    ]
    used = COL_XA + LANE
    cols.append(z(N_PROJ - used))
    return jnp.concatenate(cols, axis=1).astype(BF16)


def _layer(x2d, batch, seq, scale, shift, gate, rows_per_mod, conv_prev, s_gdn, shift_prev, s_rwkv, p, final_g,
           final_norm, scan_nb):
    proj = _in_proj(x2d, scale, shift, p["norm_g"], p["w_in_pad"], rows_per_mod)
    proj3 = proj.reshape(batch, seq, N_PROJ)
    o_g, s_gdn_new, o_r, s_rwkv_new = _scans(proj3, conv_prev, s_gdn, shift_prev, s_rwkv, p, scan_nb)
    x_new = _merge(o_g, o_r, proj, x2d, gate, p["w_o_gdn"], p["w_o_rwkv"], p["w_out"], final_g, rows_per_mod,
                   final_norm)
    conv_new = proj3[:, seq - (CONV_W - 1):, 0:3 * GDN_W]
    last = proj3[:, seq - 1:, :]
    shift_new = jnp.concatenate(
        [last[..., COL_R * SEG:COL_R * SEG + 3 * RWKV_W], last[..., COL_XW:COL_XW + W_LORA],
         last[..., COL_XA:COL_XA + A_LORA]], axis=-1)
    return x_new, conv_new, s_gdn_new, shift_new, s_rwkv_new


def _forward(x_prompt, x_sample, c_prompt, c_sample, cache_gdn_conv, state_gdn, cache_rwkv_shift, state_rwkv,
             ada_w, ada_b, norm_g, w_in, gdn_conv_w, gdn_a_log, gdn_dt_bias, gdn_out_norm_g,
             rwkv_mu, rwkv_w0, rwkv_w_lora, rwkv_a0, rwkv_a_lora, rwkv_k_k, rwkv_k_a, rwkv_r_k,
             rwkv_ln_w, rwkv_ln_b, w_o_gdn, w_o_rwkv, w_out, final_norm_g):
    depth = ada_w.shape[0]
    bp, tp, d = x_prompt.shape
    bs, ts, _ = x_sample.shape
    assert CONV_W - 1 <= min(tp, ts)
    c_all = jnp.concatenate([c_prompt, c_sample], axis=0)
    rows = -(-(bp + bs) // SUBLANE) * SUBLANE
    c_all = jnp.pad(c_all, ((0, rows - (bp + bs)), (0, 0)))
    mod = _ada_mod(c_all, ada_w, ada_b)
    xp = x_prompt.reshape(bp * tp, d)
    xs = x_sample.reshape(bs * ts, d)
    outs = [[] for _ in range(8)]
    for l in range(depth):
        p = dict(norm_g=norm_g[l], w_in_pad=_pad_in_weight(w_in[l]), conv_w=gdn_conv_w[l], a_log=gdn_a_log[l],
                 dt_bias=gdn_dt_bias[l], onorm_g=gdn_out_norm_g[l], mu=rwkv_mu[l], w0=rwkv_w0[l],
                 w_lora=rwkv_w_lora[l], a0=rwkv_a0[l], a_lora=rwkv_a_lora[l], k_k=rwkv_k_k[l], k_a=rwkv_k_a[l],
                 r_k=rwkv_r_k[l].reshape(-1), ln_w=rwkv_ln_w[l], ln_b=rwkv_ln_b[l],
                 w_o_gdn=w_o_gdn[l].astype(BF16), w_o_rwkv=w_o_rwkv[l].astype(BF16), w_out=w_out[l].astype(BF16))
        last = l == depth - 1
        m_p = mod[l, 0:bp]
        m_s = jnp.repeat(mod[l, bp:bp + bs], ts, axis=0)
        sh_p, sc_p, gt_p = m_p[:, 0:d], m_p[:, d:2 * d], m_p[:, 2 * d:]
        sh_s, sc_s, gt_s = m_s[:, 0:d], m_s[:, d:2 * d], m_s[:, 2 * d:]
        zeros = lambda *s: jnp.zeros(s, F32)
        xp, c1, g1, h1, r1 = _layer(
            xp, bp, tp, sc_p, sh_p, gt_p, tp,
            zeros(bp, CONV_W - 1, 3 * GDN_W), zeros(bp, GDN_HEADS, GDN_DK, GDN_DV),
            zeros(bp, 1, 3 * RWKV_W + W_LORA + A_LORA), zeros(bp, RWKV_HEADS, RWKV_HD, RWKV_HD),
            p, final_norm_g, last, scan_nb=bp)
        xs, c2, g2, h2, r2 = _layer(
            xs, bs, ts, sc_s, sh_s, gt_s, 1,
            cache_gdn_conv[l], state_gdn[l], cache_rwkv_shift[l], state_rwkv[l],
            p, final_norm_g, last, scan_nb=min(bs, 2))
        for lst, val in zip(outs, (c1, g1, h1, r1, c2, g2, h2, r2)):
            lst.append(val)
    stk = [jnp.stack(v) for v in outs]
    return (xp.reshape(bp, tp, d), xs.reshape(bs, ts, d), *stk)


def kernel(x_prompt, x_sample, c_prompt, c_sample, cache_gdn_conv, state_gdn, cache_rwkv_shift, state_rwkv, ada_w, ada_b, norm_g, w_in, gdn_conv_w, gdn_a_log, gdn_dt_bias, gdn_out_norm_g, rwkv_mu, rwkv_w0, rwkv_w_lora, rwkv_a0, rwkv_a_lora, rwkv_k_k, rwkv_k_a, rwkv_r_k, rwkv_ln_w, rwkv_ln_b, w_o_gdn, w_o_rwkv, w_out, final_norm_g):
    return _forward(x_prompt, x_sample, c_prompt, c_sample, cache_gdn_conv, state_gdn, cache_rwkv_shift,
                    state_rwkv, ada_w, ada_b, norm_g, w_in, gdn_conv_w, gdn_a_log, gdn_dt_bias, gdn_out_norm_g,
                    rwkv_mu, rwkv_w0, rwkv_w_lora, rwkv_a0, rwkv_a_lora, rwkv_k_k, rwkv_k_a, rwkv_r_k,
                    rwkv_ln_w, rwkv_ln_b, w_o_gdn, w_o_rwkv, w_out, final_norm_g)
```

```python
import functools
import math

import jax
import jax.numpy as jnp
from jax import lax
from jax.experimental import pallas as pl
from jax.experimental.pallas import tpu as pltpu

F32 = jnp.float32
BF16 = jnp.bfloat16

D_MODEL = 2048
GDN_HEADS = 8
GDN_DK = 128
GDN_DV = 128
GDN_W = GDN_HEADS * GDN_DK
GDN_CHUNK = 64
CONV_W = 4
RWKV_HEADS = 16
RWKV_HD = 64
RWKV_W = RWKV_HEADS * RWKV_HD
W_LORA = 96
A_LORA = 96
EPS = 1e-6
GN_EPS = 64e-5
W_DECAY_OFFSET = 0.5

LANE = 128
SUBLANE = 8
SEG = 1024
COL_Q, COL_K, COL_V, COL_ZG, COL_R, COL_RK, COL_RV, COL_ZR = 0, 1, 2, 3, 4, 5, 6, 7
COL_GATES = 8 * SEG
COL_BA = 12 * SEG
COL_XW = COL_BA + LANE
COL_XA = COL_XW + LANE
GROUP = 256
PROJ_TILE = 5 * GROUP
N_PROJ = 10 * PROJ_TILE
PACK = 4
RWKV_GROUPS = RWKV_W // GROUP
GDN_PAIRS = GDN_W // GROUP
VMEM_LIMIT = 56 * 1024 * 1024


def _mm(a, b):
    return jnp.dot(a, b, preferred_element_type=F32)


def _mm_nt(a, b):
    return lax.dot_general(a, b, (((1,), (1,)), ((), ())), preferred_element_type=F32)


def _mm_tn(a, b):
    return lax.dot_general(a, b, (((0,), (0,)), ((), ())), preferred_element_type=F32)


def _split_bf16(x):
    hi = x.astype(BF16)
    return hi, (x - hi.astype(F32)).astype(BF16)


def _split3_bf16(x):
    hi = x.astype(BF16)
    mid, lo = _split_bf16(x - hi.astype(F32))
    return hi, mid, lo


def _sigmoid(x):
    return 1.0 / (1.0 + jnp.exp(-x))


def _silu(x):
    return x * _sigmoid(x)


def _softplus(x):
    return jnp.maximum(x, 0.0) + jnp.log1p(jnp.exp(-jnp.abs(x)))


def _iota(shape, dim):
    return lax.broadcasted_iota(jnp.int32, shape, dim)


def _bits(n):
    assert n & (n - 1) == 0
    return n.bit_length() - 1


def _mod_kernel(c_ref, w_ref, b_ref, o_ref):
    o_ref[...] = _mm(c_ref[...], w_ref[...]) + b_ref[...]


def _ada_mod(c_all, ada_w, ada_b):
    depth, d, n3 = ada_w.shape
    rows = c_all.shape[0]
    tn = 512
    return pl.pallas_call(
        _mod_kernel,
        grid=(depth, n3 // tn),
        in_specs=[
            pl.BlockSpec((rows, d), lambda l, j: (0, 0)),
            pl.BlockSpec((None, d, tn), lambda l, j: (l, 0, j)),
            pl.BlockSpec((None, 1, tn), lambda l, j: (l, 0, j)),
        ],
        out_specs=pl.BlockSpec((None, rows, tn), lambda l, j: (l, 0, j)),
        out_shape=jax.ShapeDtypeStruct((depth, rows, n3), F32),
        compiler_params=pltpu.CompilerParams(
            dimension_semantics=("arbitrary", "arbitrary"), vmem_limit_bytes=VMEM_LIMIT),
        name="ada_mod",
    )(c_all, ada_w, ada_b.reshape(depth, 1, n3))


def _in_proj_kernel(x_ref, sc_ref, sh_ref, g_ref, w_ref, o_ref, h_ref):
    @pl.when(pl.program_id(1) == 0)
    def _():
        x = x_ref[...]
        ms = jnp.mean(x * x, axis=-1, keepdims=True)
        h = x * lax.rsqrt(ms + EPS) * g_ref[...] * (1.0 + sc_ref[...]) + sh_ref[...]
        h_ref[...] = h.astype(BF16)

    o_ref[...] = _mm(h_ref[...], w_ref[...])


def _in_proj(x2d, scale, shift, norm_g, w_pad, rows_per_mod):
    m, d = x2d.shape
    tm = min(m, 512) if rows_per_mod == 1 else min(rows_per_mod, 1024)
    tn = PROJ_TILE
    if rows_per_mod == 1:
        mod_spec = pl.BlockSpec((tm, d), lambda i, j: (i, 0))
    else:
        assert rows_per_mod % tm == 0
        per = rows_per_mod // tm
        scale = scale.reshape(-1, 1, d)
        shift = shift.reshape(-1, 1, d)
        mod_spec = pl.BlockSpec((None, 1, d), lambda i, j: (i // per, 0, 0))
    return pl.pallas_call(
        _in_proj_kernel,
        grid=(m // tm, N_PROJ // tn),
        in_specs=[
            pl.BlockSpec((tm, d), lambda i, j: (i, 0)),
            mod_spec, mod_spec,
            pl.BlockSpec((1, d), lambda i, j: (0, 0)),
            pl.BlockSpec((d, tn), lambda i, j: (0, j)),
        ],
        out_specs=pl.BlockSpec((tm, tn), lambda i, j: (i, j)),
        out_shape=jax.ShapeDtypeStruct((m, N_PROJ), F32),
        scratch_shapes=[pltpu.VMEM((tm, d), BF16)],
        compiler_params=pltpu.CompilerParams(
            dimension_semantics=("arbitrary", "arbitrary"), vmem_limit_bytes=VMEM_LIMIT),
        name="in_proj",
    )(x2d, scale, shift, norm_g.reshape(1, d), w_pad)


def _lane_masks(rows, head_lanes):
    if head_lanes >= LANE:
        return None
    lane = _iota((rows, LANE), 1)
    return [((lane >= o) & (lane < o + head_lanes)).astype(BF16) for o in range(0, LANE, head_lanes)]


def _block_diag(x, n_heads, lane_masks):
    L, w = x.shape
    hl = w // n_heads
    span = max(hl, LANE)
    rows = []
    for h in range(n_heads):
        c0 = (h * hl) // span * span
        piece = x[:, c0:c0 + span]
        if hl < LANE:
            piece = piece * lane_masks[(h * hl - c0) // hl]
        parts = [piece]
        if c0:
            parts.insert(0, jnp.zeros((L, c0), x.dtype))
        if w - c0 - span:
            parts.append(jnp.zeros((L, w - c0 - span), x.dtype))
        rows.append(jnp.concatenate(parts, axis=1) if len(parts) > 1 else piece)
    return jnp.concatenate(rows, axis=0)


def _head_mask(rows, row_bits, cols, col_bits):
    return (_iota((rows, cols), 0) >> row_bits) == (_iota((rows, cols), 1) >> col_bits)


def _block_sums(xs, gmat_b):
    t, width = xs[0].shape
    ng = width // GROUP
    parts = []
    for x in xs:
        xb = x.astype(BF16)
        parts += [xb[:, g * GROUP:(g + 1) * GROUP] for g in range(ng)]
    res = _mm(jnp.concatenate(parts, axis=0), gmat_b)
    return [jnp.concatenate([res[(i * ng + g) * t:(i * ng + g + 1) * t] for g in range(ng)], axis=1)
            for i in range(len(xs))]


def _cumsum_rows(x, tri_b):
    w = x.shape[1]
    res = _mm(tri_b, jnp.concatenate(_split3_bf16(x), axis=1))
    return res[:, 0:w] + (res[:, w:2 * w] + res[:, 2 * w:])


def _unit_lower_inverses(e_mats, eye, lane_masks):
    L, pw_w = e_mats[0].shape
    steps = _bits(L)
    pws = [-e for e in e_mats]
    t_invs = [eye + pw for pw in pws]
    for s in range(steps):
        first, last = s == 0, s == steps - 1
        if first and last:
            break
        for c in range(len(e_mats)):
            p_hi, p_lo = _split_bf16(pws[c])
            bd_hi = _block_diag(p_hi, PACK, lane_masks)
            bd_lo = _block_diag(p_lo, PACK, lane_masks)
            rows = ([] if first else [_split_bf16(t_invs[c])]) + ([] if last else [(p_hi, p_lo)])
            x_hi = jnp.concatenate([hi for hi, _ in rows], axis=0)
            x_lo = jnp.concatenate([lo for _, lo in rows], axis=0)
            nr = x_hi.shape[0]
            hh = _mm(jnp.concatenate([x_hi, x_lo], axis=0), bd_hi)
            res = hh[0:nr] + (hh[nr:] + _mm(x_hi, bd_lo))
            if not first:
                t_invs[c] = t_invs[c] + res[0:L]
            if not last:
                pws[c] = res[nr - L:]
        yield
    return t_invs


def _gdn_body(q_ref, k_ref, v_ref, z_ref, ba_ref, cw_ref, prev_ref, s0_ref, alog_ref, dtb_ref, on_ref,
              o_ref, sfin_ref, ubuf, sb, *, nb, chunk, nsteps):
    L = chunk
    i = pl.program_id(1)
    l_bits = _bits(L)
    dk_bits = _bits(GDN_DK)
    pw_w = PACK * L
    hw = PACK * GDN_DK
    n_grp = GDN_HEADS // PACK
    hp = GROUP // GDN_DK
    dsl = lambda h: slice((h % hp) * GDN_DK, (h % hp + 1) * GDN_DK)

    lm_p = _lane_masks(L, L)
    m_dd = _head_mask(GROUP, dk_bits, GROUP, dk_bits)
    m_dd_b = m_dd.astype(BF16)
    t_row = _iota((L, pw_w), 0)
    j_lane = _iota((L, pw_w), 1) & (L - 1)
    strict = j_lane < t_row
    incl = j_lane <= t_row
    eye = (j_lane == t_row).astype(F32)
    tri_b = (_iota((L, L), 0) >= _iota((L, L), 1)).astype(BF16)
    ones_b = jnp.ones((L, L), BF16)
    lane = _iota((L, LANE), 1)
    x_rows = _iota((LANE, GDN_HEADS * L + GDN_W), 0) & (GDN_HEADS - 1)
    x_cols = _iota((LANE, GDN_HEADS * L + GDN_W), 1)
    x_head = jnp.where(x_cols < GDN_HEADS * L, x_cols >> l_bits, (x_cols - GDN_HEADS * L) >> dk_bits)
    expand_b = ((x_rows == x_head) & (_iota((LANE, GDN_HEADS * L + GDN_W), 0) < 2 * GDN_HEADS)).astype(BF16)

    def bdiag(x):
        return _block_diag(x.astype(BF16), PACK, None)

    @pl.when(i == 0)
    def _():
        sb[...] = jnp.zeros(sb.shape, F32)
        for n in range(nb):
            ubuf[n, 0:SUBLANE, :] = prev_ref[n]
            for h in range(GDN_HEADS):
                sb[n, h // hp, dsl(h), dsl(h)] = s0_ref[n, h]

    base = SUBLANE - (CONV_W - 1)
    pre = []
    for n in range(nb):
        ubuf[n, SUBLANE:SUBLANE + L, 0:GDN_W] = q_ref[n]
        ubuf[n, SUBLANE:SUBLANE + L, GDN_W:2 * GDN_W] = k_ref[n]
        ubuf[n, SUBLANE:SUBLANE + L, 2 * GDN_W:3 * GDN_W] = v_ref[n]
        y = ubuf[n, base:base + L, :] * cw_ref[0:1, :]
        for j in range(1, CONV_W):
            y = y + ubuf[n, base + j:base + j + L, :] * cw_ref[j:j + 1, :]
        ubuf[n, 0:SUBLANE, :] = ubuf[n, L:L + SUBLANE, :]
        act = _silu(y)
        q = act[:, 0:GDN_W]
        k = act[:, GDN_W:2 * GDN_W]
        v = act[:, 2 * GDN_W:]
        ssq, ssk = _block_sums([q * q, k * k], m_dd_b)
        q = q * lax.rsqrt(ssq + EPS) * (GDN_DK ** -0.5)
        k = k * lax.rsqrt(ssk + EPS)

        ba = ba_ref[n]
        beta_c = jnp.where(lane < GDN_HEADS, _sigmoid(ba), 0.0)
        g_c = -jnp.exp(alog_ref[...]) * _softplus(ba + dtb_ref[...])
        gc_c = jnp.where((lane >= GDN_HEADS) & (lane < 2 * GDN_HEADS), _cumsum_rows(g_c, tri_b), 0.0)
        pieces = jnp.concatenate(list(_split_bf16(beta_c)) + list(_split3_bf16(gc_c)), axis=0)
        ex = _mm(pieces, expand_b)
        beta_x = ex[0:L] + ex[L:2 * L]
        gc_x = ex[2 * L:3 * L] + (ex[3 * L:4 * L] + ex[4 * L:])
        beta_p, beta_w = beta_x[:, 0:GDN_HEADS * L], beta_x[:, GDN_HEADS * L:]
        gc_p, gc_w = gc_x[:, 0:GDN_HEADS * L], gc_x[:, GDN_HEADS * L:]
        eye8 = jnp.concatenate([eye] * n_grp, axis=1)
        incl8 = jnp.concatenate([incl] * n_grp, axis=1)
        rw = _mm(ones_b, jnp.concatenate(_split3_bf16(gc_p * eye8), axis=1))
        pw8 = GDN_HEADS * L
        gc_row = rw[:, 0:pw8] + (rw[:, pw8:2 * pw8] + rw[:, 2 * pw8:])
        dm = jnp.where(incl8, jnp.exp(jnp.where(incl8, gc_p - gc_row, 0.0)), 0.0)
        gl_w = gc_w[L - 1:L, :]
        eg_w = jnp.exp(gc_w)
        qd = q * eg_w
        kd = k * jnp.exp(gl_w - gc_w)
        bv = beta_w * v
        bek = beta_w * eg_w * k
        pre.append(dict(q=q, k=k, beta_p=beta_p, dm=dm, qd=qd, kd=kd, bv=bv, bek=bek, egl_w=jnp.exp(gl_w)))
        yield

    chains = [(n, g) for g in range(n_grp) for n in range(nb)]
    wsl = lambda g: slice(g * hw, (g + 1) * hw)
    psl = lambda g: slice(g * pw_w, (g + 1) * pw_w)
    ppg = hw // GROUP
    a_mat, qkd = {}, {}
    for c in chains:
        p, sw, sp = pre[c[0]], wsl(c[1]), psl(c[1])
        prod = _mm_nt(jnp.concatenate([p["k"][:, sw], p["q"][:, sw]], axis=0).astype(BF16), bdiag(p["k"][:, sw]))
        a_mat[c] = jnp.where(strict, p["beta_p"][:, sp] * prod[0:L] * p["dm"][:, sp], 0.0)
        qkd[c] = prod[L:] * p["dm"][:, sp]
    yield
    t_inv = dict(zip(chains, (yield from _unit_lower_inverses([a_mat[c] for c in chains], eye, lm_p))))
    sol = {}
    for c in chains:
        p, sw = pre[c[0]], wsl(c[1])
        sol[c] = _mm(t_inv[c].astype(BF16), jnp.concatenate([bdiag(p["bv"][:, sw]), bdiag(p["bek"][:, sw])], axis=1))
    yield
    pairs = [(c, j) for c in chains for j in range(ppg)]
    s_p, r2, u_p = {}, {}, {}
    for c, j in pairs:
        pair = c[1] * ppg + j
        sg = slice(pair * GROUP, (pair + 1) * GROUP)
        s_p[c, j] = sb[c[0], pair]
        r2[c, j] = _mm(jnp.concatenate([sol[c][:, hw + j * GROUP:hw + (j + 1) * GROUP], pre[c[0]]["qd"][:, sg]],
                                       axis=0).astype(BF16), s_p[c, j].astype(BF16))
    yield
    for c, j in pairs:
        pair = c[1] * ppg + j
        sg = slice(pair * GROUP, (pair + 1) * GROUP)
        u_p[c, j] = sol[c][:, j * GROUP:(j + 1) * GROUP] - r2[c, j][0:L]
        upd = _mm_tn(pre[c[0]]["kd"][:, sg].astype(BF16), u_p[c, j].astype(BF16))
        sb[c[0], pair] = s_p[c, j] * pre[c[0]]["egl_w"][:, sg] + jnp.where(m_dd, upd, 0.0)
    yield
    o_g = {}
    for c in chains:
        u = jnp.concatenate([u_p[c, j] for j in range(ppg)], axis=1)
        qs = jnp.concatenate([r2[c, j][L:] for j in range(ppg)], axis=1)
        o_g[c] = qs + _mm(qkd[c].astype(BF16), bdiag(u))
    yield

    for n in range(nb):
        o = jnp.concatenate([o_g[(n, g)] for g in range(n_grp)], axis=1)
        sso, = _block_sums([o * o], m_dd_b)
        on = o * lax.rsqrt(sso * (1.0 / GDN_DV) + EPS) * on_ref[...]
        o_ref[n] = (on * _silu(z_ref[n])).astype(o_ref.dtype)

    @pl.when(i == nsteps - 1)
    def _():
        for n in range(nb):
            for h in range(GDN_HEADS):
                sfin_ref[n, h] = sb[n, h // hp, dsl(h), dsl(h)]


def _rwkv_body(r_ref, k_ref, v_ref, xw_ref, xa_ref, z_ref, pr_ref, pk_ref, pv_ref, pxw_ref, pxa_ref, s0_ref,
               mu_ref, muw_ref, mua_ref, w0_ref, wl_ref, a0_ref, al_ref, kk_ref, ka_ref, rk_ref,
               lnw_ref, lnb_ref,
               o_ref, sfin_ref,
               sbuf, lbuf, sbt, *, nb, chunk, nsteps):
    i = pl.program_id(1)
    L = chunk
    pw_w = PACK * L
    hd_bits = _bits(RWKV_HD)

    lm_p = _lane_masks(L, L)
    lm_k = _lane_masks(L, RWKV_HD)
    m_kk = _head_mask(GROUP, hd_bits, GROUP, hd_bits)
    m_kk_b = m_kk.astype(BF16)
    t_row = _iota((L, pw_w), 0)
    j_lane = _iota((L, pw_w), 1) & (L - 1)
    strict = j_lane < t_row
    incl = j_lane <= t_row
    eye = (j_lane == t_row).astype(F32)
    tri_b = (_iota((L, L), 0) >= _iota((L, L), 1)).astype(BF16)
    hsl = lambda h: slice((h % PACK) * RWKV_HD, (h % PACK + 1) * RWKV_HD)

    def bdiag(x):
        return _block_diag(x.astype(BF16), PACK, lm_k)

    @pl.when(i == 0)
    def _():
        sbt[...] = jnp.zeros(sbt.shape, F32)
        for n in range(nb):
            for h in range(RWKV_HEADS):
                sbt[n, h // PACK, hsl(h), hsl(h)] = s0_ref[n, h]
            sbuf[n, SUBLANE - 1:SUBLANE, 0:SEG] = pr_ref[n]
            sbuf[n, SUBLANE - 1:SUBLANE, SEG:2 * SEG] = pk_ref[n]
            sbuf[n, SUBLANE - 1:SUBLANE, 2 * SEG:3 * SEG] = pv_ref[n]
            lbuf[n, SUBLANE - 1:SUBLANE, 0:LANE] = pxw_ref[n]
            lbuf[n, SUBLANE - 1:SUBLANE, LANE:2 * LANE] = pxa_ref[n]

    inv_hd = 1.0 / RWKV_HD
    pre = []
    for n in range(nb):
        sbuf[n, SUBLANE:SUBLANE + L, 0:SEG] = r_ref[n]
        sbuf[n, SUBLANE:SUBLANE + L, SEG:2 * SEG] = k_ref[n]
        sbuf[n, SUBLANE:SUBLANE + L, 2 * SEG:3 * SEG] = v_ref[n]
        lbuf[n, SUBLANE:SUBLANE + L, 0:LANE] = xw_ref[n]
        lbuf[n, SUBLANE:SUBLANE + L, LANE:2 * LANE] = xa_ref[n]
        cur = sbuf[n, SUBLANE:SUBLANE + L, :]
        prv = sbuf[n, SUBLANE - 1:SUBLANE - 1 + L, :]
        rkv = cur + (prv - cur) * mu_ref[...]
        curl = lbuf[n, SUBLANE:SUBLANE + L, :]
        prvl = lbuf[n, SUBLANE - 1:SUBLANE - 1 + L, :]
        xw = curl[:, 0:LANE] + (prvl[:, 0:LANE] - curl[:, 0:LANE]) * muw_ref[...]
        xa = curl[:, LANE:] + (prvl[:, LANE:] - curl[:, LANE:]) * mua_ref[...]
        sbuf[n, 0:SUBLANE, :] = sbuf[n, L:L + SUBLANE, :]
        lbuf[n, 0:SUBLANE, :] = lbuf[n, L:L + SUBLANE, :]
        r = rkv[:, 0:SEG]
        kr = rkv[:, SEG:2 * SEG]
        vr = rkv[:, 2 * SEG:3 * SEG]
        wl = w0_ref[...] + _mm(jnp.tanh(xw).astype(BF16), wl_ref[...])
        logw = -math.exp(-W_DECAY_OFFSET) * _sigmoid(wl)
        a = _sigmoid(a0_ref[...] + _mm(xa.astype(BF16), al_ref[...]))
        kkr = kr * kk_ref[...]
        kt = kr * (1.0 + (a - 1.0) * ka_ref[...])
        ss, bonus = _block_sums([kkr * kkr, r * kt * rk_ref[...]], m_kk_b)
        kk = kkr * lax.rsqrt(ss + EPS)
        ah = a * kk
        cum = _cumsum_rows(logw, tri_b)
        c_last = cum[L - 1:L, :]
        e_neg = jnp.exp(-cum)
        e_last = jnp.exp(c_last - cum)
        kx = kk * jnp.exp(cum - logw)
        rx = r * jnp.exp(cum)
        kb = kt * e_neg
        ab = ah * e_neg
        kh = kt * e_last
        ahh = ah * e_last
        pre.append(dict(vr=vr, bonus=bonus, kx=kx, rx=rx, kb=kb, ab=ab, kh=kh, ahh=ahh, g_last=jnp.exp(c_last)))
        yield

    chains = [(n, g) for g in range(RWKV_GROUPS) for n in range(nb)]
    gsl = lambda g: slice(g * GROUP, (g + 1) * GROUP)
    lhs, c_mat, e_mat, rk_mat, ra_mat = {}, {}, {}, {}, {}
    for c in chains:
        p, sl = pre[c[0]], gsl(c[1])
        lhs[c] = jnp.concatenate([p["kx"][:, sl], p["rx"][:, sl]], axis=0).astype(BF16)
        rhs_nt = jnp.concatenate([bdiag(p["kb"][:, sl]), bdiag(p["ab"][:, sl])], axis=0)
        prod = _mm_nt(lhs[c], rhs_nt)
        c_mat[c] = jnp.where(strict, prod[0:L, 0:pw_w], 0.0)
        e_mat[c] = jnp.where(strict, prod[0:L, pw_w:], 0.0)
        rk_mat[c] = jnp.where(incl, prod[L:, 0:pw_w], 0.0)
        ra_mat[c] = jnp.where(incl, prod[L:, pw_w:], 0.0)
    yield
    t_inv = dict(zip(chains, (yield from _unit_lower_inverses([e_mat[c] for c in chains], eye, lm_p))))
    s_g, ks_rs, cv, u = {}, {}, {}, {}
    for c in chains:
        s_g[c] = sbt[c[0], c[1]]
        ks_rs[c] = _mm_nt(lhs[c], s_g[c].astype(BF16))
        cv[c] = _mm(jnp.concatenate([c_mat[c], rk_mat[c]], axis=0).astype(BF16),
                    bdiag(pre[c[0]]["vr"][:, gsl(c[1])]))
    yield
    for c in chains:
        u[c] = _mm(t_inv[c].astype(BF16), bdiag(ks_rs[c][0:L] + cv[c][0:L]))
    yield
    y_g = {}
    for c in chains:
        p, sl = pre[c[0]], gsl(c[1])
        y_g[c] = ks_rs[c][L:] + cv[c][L:] - _mm(ra_mat[c].astype(BF16), bdiag(u[c]))
        upd = _mm_tn(jnp.concatenate([p["vr"][:, sl], -u[c]], axis=0).astype(BF16),
                     jnp.concatenate([p["kh"][:, sl], p["ahh"][:, sl]], axis=0).astype(BF16))
        sbt[c[0], c[1]] = s_g[c] * p["g_last"][:, sl] + jnp.where(m_kk, upd, 0.0)
    yield

    for n in range(nb):
        y = jnp.concatenate([y_g[(n, g)] for g in range(RWKV_GROUPS)], axis=1)
        dlt = y - _block_sums([y], m_kk_b)[0] * inv_hd
        var = _block_sums([dlt * dlt], m_kk_b)[0] * inv_hd
        yn = dlt * lax.rsqrt(var + GN_EPS) * lnw_ref[...] + lnb_ref[...]
        o_ref[n] = ((yn + pre[n]["bonus"] * pre[n]["vr"]) * _silu(z_ref[n])).astype(o_ref.dtype)

    @pl.when(i == nsteps - 1)
    def _():
        for n in range(nb):
            for h in range(RWKV_HEADS):
                sfin_ref[n, h] = sbt[n, h // PACK, hsl(h), hsl(h)]


N_GDN_IN, N_RWKV_IN = 11, 24


def _scan_kernel(*refs, nb, chunk, nsteps):
    g_in = refs[0:N_GDN_IN]
    r_in = refs[N_GDN_IN:N_GDN_IN + N_RWKV_IN]
    og_ref, sg_ref, or_ref, sr_ref, ubuf, sb, sbuf, lbuf, sbt = refs[N_GDN_IN + N_RWKV_IN:]
    kw = dict(nb=nb, chunk=chunk, nsteps=nsteps)
    gens = [_gdn_body(*g_in, og_ref, sg_ref, ubuf, sb, **kw), _rwkv_body(*r_in, or_ref, sr_ref, sbuf, lbuf, sbt, **kw)]
    while gens:
        for gen in list(gens):
            try:
                next(gen)
            except StopIteration:
                gens.remove(gen)


def _scans(proj3, conv_prev, s_gdn, shift_prev, s_rwkv, p, nb):
    b, t, _ = proj3.shape
    L = min(GDN_CHUNK, t)
    assert b % nb == 0 and t % L == 0 and L % SUBLANE == 0
    nsteps = t // L
    blk = lambda c, w: pl.BlockSpec((nb, L, w), lambda g, i, c=c: (g, i, c))
    const = lambda shape: pl.BlockSpec(shape, lambda g, i: (0,) * len(shape))
    per_seq = lambda *shape: pl.BlockSpec((nb,) + shape, lambda g, i: (g,) + (0,) * len(shape))
    prev_pad = jnp.pad(conv_prev, ((0, 0), (SUBLANE - (CONV_W - 1), 0), (0, 0)))
    lane_pad = (GDN_HEADS, LANE - 2 * GDN_HEADS)
    alog_row = jnp.pad(p["a_log"], lane_pad).reshape(1, LANE)
    dtb_row = jnp.pad(p["dt_bias"], lane_pad).reshape(1, LANE)
    on_row = jnp.tile(p["onorm_g"], GDN_HEADS).reshape(1, GDN_W)
    gdn_state = per_seq(GDN_HEADS, GDN_DK, GDN_DV)
    gdn_specs = [
        blk(COL_Q, SEG), blk(COL_K, SEG), blk(COL_V, SEG), blk(COL_ZG, SEG), blk(COL_BA // LANE, LANE),
        const((CONV_W, 3 * GDN_W)), per_seq(SUBLANE, 3 * GDN_W), gdn_state,
        const((1, LANE)), const((1, LANE)), const((1, GDN_W)),
    ]
    gdn_args = [proj3] * 5 + [p["conv_w"], prev_pad, s_gdn, alog_row, dtb_row, on_row]
    w3 = 3 * RWKV_W
    padl = lambda x, n: jnp.pad(x, [(0, 0)] * (x.ndim - 1) + [(0, LANE - n)])
    p_r, p_k, p_v = shift_prev[..., 0:RWKV_W], shift_prev[..., RWKV_W:2 * RWKV_W], shift_prev[..., 2 * RWKV_W:w3]
    p_xw = padl(shift_prev[..., w3:w3 + W_LORA], W_LORA)
    p_xa = padl(shift_prev[..., w3 + W_LORA:], A_LORA)
    mu = p["mu"]
    mu_rkv = mu[0:w3].reshape(1, w3)
    mu_w = padl(mu[w3:w3 + W_LORA], W_LORA).reshape(1, LANE)
    mu_a = padl(mu[w3 + W_LORA:], A_LORA).reshape(1, LANE)
    wl_pad = jnp.pad(p["w_lora"], ((0, LANE - W_LORA), (0, 0))).astype(BF16)
    al_pad = jnp.pad(p["a_lora"], ((0, LANE - A_LORA), (0, 0))).astype(BF16)
    s0_t = s_rwkv.transpose(0, 1, 3, 2)
    row1 = lambda x: x.reshape(1, RWKV_W)
    rwkv_state = per_seq(RWKV_HEADS, RWKV_HD, RWKV_HD)
    rwkv_specs = [
        blk(COL_R, SEG), blk(COL_RK, SEG), blk(COL_RV, SEG),
        blk(COL_XW // LANE, LANE), blk(COL_XA // LANE, LANE), blk(COL_ZR, SEG),
        per_seq(1, SEG), per_seq(1, SEG), per_seq(1, SEG), per_seq(1, LANE), per_seq(1, LANE),
        rwkv_state,
        const((1, w3)), const((1, LANE)), const((1, LANE)),
        const((1, RWKV_W)), const((LANE, RWKV_W)), const((1, RWKV_W)), const((LANE, RWKV_W)),
        const((1, RWKV_W)), const((1, RWKV_W)), const((1, RWKV_W)), const((1, RWKV_W)), const((1, RWKV_W)),
    ]
    rwkv_args = [proj3] * 6 + [p_r, p_k, p_v, p_xw, p_xa, s0_t, mu_rkv, mu_w, mu_a, row1(p["w0"]), wl_pad,
                               row1(p["a0"]), al_pad, row1(p["k_k"]), row1(p["k_a"]), row1(p["r_k"]),
                               row1(p["ln_w"]), row1(p["ln_b"])]
    assert len(gdn_specs) == N_GDN_IN and len(rwkv_specs) == N_RWKV_IN
    o_g, sg, o_r, sr = pl.pallas_call(
        functools.partial(_scan_kernel, nb=nb, chunk=L, nsteps=nsteps),
        grid=(b // nb, nsteps),
        in_specs=gdn_specs + rwkv_specs,
        out_specs=[
            pl.BlockSpec((nb, L, GDN_W), lambda g, i: (g, i, 0)), gdn_state,
            pl.BlockSpec((nb, L, RWKV_W), lambda g, i: (g, i, 0)), rwkv_state,
        ],
        out_shape=[
            jax.ShapeDtypeStruct((b, t, GDN_W), BF16),
            jax.ShapeDtypeStruct((b, GDN_HEADS, GDN_DK, GDN_DV), F32),
            jax.ShapeDtypeStruct((b, t, RWKV_W), BF16),
            jax.ShapeDtypeStruct((b, RWKV_HEADS, RWKV_HD, RWKV_HD), F32),
        ],
        scratch_shapes=[
            pltpu.VMEM((nb, L + SUBLANE, 3 * GDN_W), F32),
            pltpu.VMEM((nb, GDN_PAIRS, GROUP, GROUP), F32),
            pltpu.VMEM((nb, L + SUBLANE, w3), F32),
            pltpu.VMEM((nb, L + SUBLANE, 2 * LANE), F32),
            pltpu.VMEM((nb, RWKV_GROUPS, GROUP, GROUP), F32),
        ],
        compiler_params=pltpu.CompilerParams(
            dimension_semantics=("arbitrary", "arbitrary"), vmem_limit_bytes=VMEM_LIMIT),
        name="scan",
    )(*gdn_args, *rwkv_args)
    return o_g.reshape(b * t, GDN_W), sg, o_r.reshape(b * t, RWKV_W), sr.transpose(0, 1, 3, 2)


def _merge_kernel(og_ref, or_ref, gg_ref, gr_ref, x_ref, gate_ref, wog_ref, wor_ref, wout_ref, fg_ref, o_ref,
                  *, final_norm):
    m = _sigmoid(gg_ref[...]) * _mm(og_ref[...], wog_ref[...]) \
        + _sigmoid(gr_ref[...]) * _mm(or_ref[...], wor_ref[...])
    out = _mm(m.astype(BF16), wout_ref[...])
    xn = x_ref[...] + gate_ref[...] * out
    if final_norm:
        xn = xn * lax.rsqrt(jnp.mean(xn * xn, axis=-1, keepdims=True) + EPS) * fg_ref[...]
    o_ref[...] = xn


def _merge(o_g, o_r, proj, x2d, gate, w_og, w_or, w_out, final_g, rows_per_mod, final_norm):
    m, d = x2d.shape
    tm = min(m, 256) if rows_per_mod == 1 else min(rows_per_mod, 256)
    if rows_per_mod == 1:
        gate_spec = pl.BlockSpec((tm, d), lambda i: (i, 0))
    else:
        assert rows_per_mod % tm == 0
        per = rows_per_mod // tm
        gate = gate.reshape(-1, 1, d)
        gate_spec = pl.BlockSpec((None, 1, d), lambda i: (i // per, 0, 0))
    whole = lambda shape: pl.BlockSpec(shape, lambda i: (0, 0))
    gcol = COL_GATES // d
    return pl.pallas_call(
        functools.partial(_merge_kernel, final_norm=final_norm),
        grid=(m // tm,),
        in_specs=[
            pl.BlockSpec((tm, GDN_W), lambda i: (i, 0)),
            pl.BlockSpec((tm, RWKV_W), lambda i: (i, 0)),
            pl.BlockSpec((tm, d), lambda i: (i, gcol)),
            pl.BlockSpec((tm, d), lambda i: (i, gcol + 1)),
            pl.BlockSpec((tm, d), lambda i: (i, 0)),
            gate_spec,
            whole((GDN_W, d)), whole((RWKV_W, d)), whole((d, d)), whole((1, d)),
        ],
        out_specs=pl.BlockSpec((tm, d), lambda i: (i, 0)),
        out_shape=jax.ShapeDtypeStruct((m, d), F32),
        compiler_params=pltpu.CompilerParams(
            dimension_semantics=("arbitrary",), vmem_limit_bytes=VMEM_LIMIT),
        name="merge_out",
    )(o_g, o_r, proj, proj, x2d, gate, w_og, w_or, w_out, final_g.reshape(1, d))


def _pad_in_weight(w_in):
    d = w_in.shape[0]
    o_zg = 3 * GDN_W
    o_b = o_zg + GDN_W
    o_rw = o_b + 2 * GDN_HEADS
    o_xw = o_rw + 3 * RWKV_W
    o_xa = o_xw + W_LORA
    o_zr = o_xa + A_LORA
    o_br = o_zr + RWKV_W
    z = lambda n: jnp.zeros((d, n), w_in.dtype)
    cols = [
        w_in[:, 0:o_zg], w_in[:, o_zg:o_b],
        w_in[:, o_rw:o_xw], w_in[:, o_zr:o_br],
        w_in[:, o_br:o_br + 2 * D_MODEL],
        w_in[:, o_b:o_rw], z(LANE - 2 * GDN_HEADS),
        w_in[:, o_xw:o_xa], z(LANE - W_LORA),
        w_in[:, o_xa:o_zr], z(LANE - A_LORA),
    ]
    used = COL_XA + LANE
    cols.append(z(N_PROJ - used))
    return jnp.concatenate(cols, axis=1).astype(BF16)


def _layer(x2d, batch, seq, scale, shift, gate, rows_per_mod, conv_prev, s_gdn, shift_prev, s_rwkv, p, final_g,
           final_norm, scan_nb):
    proj = _in_proj(x2d, scale, shift, p["norm_g"], p["w_in_pad"], rows_per_mod)
    proj3 = proj.reshape(batch, seq, N_PROJ)
    o_g, s_gdn_new, o_r, s_rwkv_new = _scans(proj3, conv_prev, s_gdn, shift_prev, s_rwkv, p, scan_nb)
    x_new = _merge(o_g, o_r, proj, x2d, gate, p["w_o_gdn"], p["w_o_rwkv"], p["w_out"], final_g, rows_per_mod,
                   final_norm)
    conv_new = proj3[:, seq - (CONV_W - 1):, 0:3 * GDN_W]
    last = proj3[:, seq - 1:, :]
    shift_new = jnp.concatenate(
        [last[..., COL_R * SEG:COL_R * SEG + 3 * RWKV_W], last[..., COL_XW:COL_XW + W_LORA],
         last[..., COL_XA:COL_XA + A_LORA]], axis=-1)
    return x_new, conv_new, s_gdn_new, shift_new, s_rwkv_new


def _forward(x_prompt, x_sample, c_prompt, c_sample, cache_gdn_conv, state_gdn, cache_rwkv_shift, state_rwkv,
             ada_w, ada_b, norm_g, w_in, gdn_conv_w, gdn_a_log, gdn_dt_bias, gdn_out_norm_g,
             rwkv_mu, rwkv_w0, rwkv_w_lora, rwkv_a0, rwkv_a_lora, rwkv_k_k, rwkv_k_a, rwkv_r_k,
             rwkv_ln_w, rwkv_ln_b, w_o_gdn, w_o_rwkv, w_out, final_norm_g):
    depth = ada_w.shape[0]
    bp, tp, d = x_prompt.shape
    bs, ts, _ = x_sample.shape
    assert CONV_W - 1 <= min(tp, ts)
    c_all = jnp.concatenate([c_prompt, c_sample], axis=0)
    rows = -(-(bp + bs) // SUBLANE) * SUBLANE
    c_all = jnp.pad(c_all, ((0, rows - (bp + bs)), (0, 0)))
    mod = _ada_mod(c_all, ada_w, ada_b)
    xp = x_prompt.reshape(bp * tp, d)
    xs = x_sample.reshape(bs * ts, d)
    outs = [[] for _ in range(8)]
    for l in range(depth):
        p = dict(norm_g=norm_g[l], w_in_pad=_pad_in_weight(w_in[l]), conv_w=gdn_conv_w[l], a_log=gdn_a_log[l],
                 dt_bias=gdn_dt_bias[l], onorm_g=gdn_out_norm_g[l], mu=rwkv_mu[l], w0=rwkv_w0[l],
                 w_lora=rwkv_w_lora[l], a0=rwkv_a0[l], a_lora=rwkv_a_lora[l], k_k=rwkv_k_k[l], k_a=rwkv_k_a[l],
                 r_k=rwkv_r_k[l].reshape(-1), ln_w=rwkv_ln_w[l], ln_b=rwkv_ln_b[l],
                 w_o_gdn=w_o_gdn[l].astype(BF16), w_o_rwkv=w_o_rwkv[l].astype(BF16), w_out=w_out[l].astype(BF16))
        last = l == depth - 1
        m_p = mod[l, 0:bp]
        m_s = jnp.repeat(mod[l, bp:bp + bs], ts, axis=0)
        sh_p, sc_p, gt_p = m_p[:, 0:d], m_p[:, d:2 * d], m_p[:, 2 * d:]
        sh_s, sc_s, gt_s = m_s[:, 0:d], m_s[:, d:2 * d], m_s[:, 2 * d:]
        zeros = lambda *s: jnp.zeros(s, F32)
        xp, c1, g1, h1, r1 = _layer(
            xp, bp, tp, sc_p, sh_p, gt_p, tp,
            zeros(bp, CONV_W - 1, 3 * GDN_W), zeros(bp, GDN_HEADS, GDN_DK, GDN_DV),
            zeros(bp, 1, 3 * RWKV_W + W_LORA + A_LORA), zeros(bp, RWKV_HEADS, RWKV_HD, RWKV_HD),
            p, final_norm_g, last, scan_nb=bp)
        xs, c2, g2, h2, r2 = _layer(
            xs, bs, ts, sc_s, sh_s, gt_s, 1,
            cache_gdn_conv[l], state_gdn[l], cache_rwkv_shift[l], state_rwkv[l],
            p, final_norm_g, last, scan_nb=min(bs, 2))
        for lst, val in zip(outs, (c1, g1, h1, r1, c2, g2, h2, r2)):
            lst.append(val)
    stk = [jnp.stack(v) for v in outs]
    return (xp.reshape(bp, tp, d), xs.reshape(bs, ts, d), *stk)


def kernel(x_prompt, x_sample, c_prompt, c_sample, cache_gdn_conv, state_gdn, cache_rwkv_shift, state_rwkv, ada_w, ada_b, norm_g, w_in, gdn_conv_w, gdn_a_log, gdn_dt_bias, gdn_out_norm_g, rwkv_mu, rwkv_w0, rwkv_w_lora, rwkv_a0, rwkv_a_lora, rwkv_k_k, rwkv_k_a, rwkv_r_k, rwkv_ln_w, rwkv_ln_b, w_o_gdn, w_o_rwkv, w_out, final_norm_g):
    return _forward(x_prompt, x_sample, c_prompt, c_sample, cache_gdn_conv, state_gdn, cache_rwkv_shift,
                    state_rwkv, ada_w, ada_b, norm_g, w_in, gdn_conv_w, gdn_a_log, gdn_dt_bias, gdn_out_norm_g,
                    rwkv_mu, rwkv_w0, rwkv_w_lora, rwkv_a0, rwkv_a_lora, rwkv_k_k, rwkv_k_a, rwkv_r_k,
                    rwkv_ln_w, rwkv_ln_b, w_o_gdn, w_o_rwkv, w_out, final_norm_g)
```

```python
import functools
import math

import jax
import jax.numpy as jnp
from jax import lax
from jax.experimental import pallas as pl
from jax.experimental.pallas import tpu as pltpu

F32 = jnp.float32
BF16 = jnp.bfloat16

D_MODEL = 2048
GDN_HEADS = 8
GDN_DK = 128
GDN_DV = 128
GDN_W = GDN_HEADS * GDN_DK
GDN_CHUNK = 64
CONV_W = 4
RWKV_HEADS = 16
RWKV_HD = 64
RWKV_W = RWKV_HEADS * RWKV_HD
W_LORA = 96
A_LORA = 96
EPS = 1e-6
GN_EPS = 64e-5
W_DECAY_OFFSET = 0.5

LANE = 128
SUBLANE = 8
SEG = 1024
COL_Q, COL_K, COL_V, COL_ZG, COL_R, COL_RK, COL_RV, COL_ZR = 0, 1, 2, 3, 4, 5, 6, 7
COL_GATES = 8 * SEG
COL_BA = 12 * SEG
COL_XW = COL_BA + LANE
COL_XA = COL_XW + LANE
GROUP = 256
PROJ_TILE = 5 * GROUP
N_PROJ = 10 * PROJ_TILE
PACK = 4
RWKV_GROUPS = RWKV_W // GROUP
GDN_PAIRS = GDN_W // GROUP
VMEM_LIMIT = 56 * 1024 * 1024


def _mm(a, b):
    return jnp.dot(a, b, preferred_element_type=F32)


def _mm_nt(a, b):
    return lax.dot_general(a, b, (((1,), (1,)), ((), ())), preferred_element_type=F32)


def _mm_tn(a, b):
    return lax.dot_general(a, b, (((0,), (0,)), ((), ())), preferred_element_type=F32)


def _split_bf16(x):
    hi = x.astype(BF16)
    return hi, (x - hi.astype(F32)).astype(BF16)


def _split3_bf16(x):
    hi = x.astype(BF16)
    mid, lo = _split_bf16(x - hi.astype(F32))
    return hi, mid, lo


def _sigmoid(x):
    return 1.0 / (1.0 + jnp.exp(-x))


def _silu(x):
    return x * _sigmoid(x)


def _softplus(x):
    return jnp.maximum(x, 0.0) + jnp.log1p(jnp.exp(-jnp.abs(x)))


def _iota(shape, dim):
    return lax.broadcasted_iota(jnp.int32, shape, dim)


def _bits(n):
    assert n & (n - 1) == 0
    return n.bit_length() - 1


def _mod_kernel(c_ref, w_ref, b_ref, o_ref):
    o_ref[...] = _mm(c_ref[...], w_ref[...]) + b_ref[...]


def _ada_mod(c_all, ada_w, ada_b):
    depth, d, n3 = ada_w.shape
    rows = c_all.shape[0]
    tn = 512
    return pl.pallas_call(
        _mod_kernel,
        grid=(depth, n3 // tn),
        in_specs=[
            pl.BlockSpec((rows, d), lambda l, j: (0, 0)),
            pl.BlockSpec((None, d, tn), lambda l, j: (l, 0, j)),
            pl.BlockSpec((None, 1, tn), lambda l, j: (l, 0, j)),
        ],
        out_specs=pl.BlockSpec((None, rows, tn), lambda l, j: (l, 0, j)),
        out_shape=jax.ShapeDtypeStruct((depth, rows, n3), F32),
        compiler_params=pltpu.CompilerParams(
            dimension_semantics=("arbitrary", "arbitrary"), vmem_limit_bytes=VMEM_LIMIT),
        name="ada_mod",
    )(c_all, ada_w, ada_b.reshape(depth, 1, n3))


def _in_proj_kernel(x_ref, sc_ref, sh_ref, g_ref, w_ref, o_ref, h_ref):
    @pl.when(pl.program_id(1) == 0)
    def _():
        x = x_ref[...]
        ms = jnp.mean(x * x, axis=-1, keepdims=True)
        h = x * lax.rsqrt(ms + EPS) * g_ref[...] * (1.0 + sc_ref[...]) + sh_ref[...]
        h_ref[...] = h.astype(BF16)

    o_ref[...] = _mm(h_ref[...], w_ref[...])


def _in_proj(x2d, scale, shift, norm_g, w_pad, rows_per_mod):
    m, d = x2d.shape
    tm = min(m, 512) if rows_per_mod == 1 else min(rows_per_mod, 1024)
    tn = PROJ_TILE
    if rows_per_mod == 1:
        mod_spec = pl.BlockSpec((tm, d), lambda i, j: (i, 0))
    else:
        assert rows_per_mod % tm == 0
        per = rows_per_mod // tm
        scale = scale.reshape(-1, 1, d)
        shift = shift.reshape(-1, 1, d)
        mod_spec = pl.BlockSpec((None, 1, d), lambda i, j: (i // per, 0, 0))
    return pl.pallas_call(
        _in_proj_kernel,
        grid=(m // tm, N_PROJ // tn),
        in_specs=[
            pl.BlockSpec((tm, d), lambda i, j: (i, 0)),
            mod_spec, mod_spec,
            pl.BlockSpec((1, d), lambda i, j: (0, 0)),
            pl.BlockSpec((d, tn), lambda i, j: (0, j)),
        ],
        out_specs=pl.BlockSpec((tm, tn), lambda i, j: (i, j)),
        out_shape=jax.ShapeDtypeStruct((m, N_PROJ), F32),
        scratch_shapes=[pltpu.VMEM((tm, d), BF16)],
        compiler_params=pltpu.CompilerParams(
            dimension_semantics=("arbitrary", "arbitrary"), vmem_limit_bytes=VMEM_LIMIT),
        name="in_proj",
    )(x2d, scale, shift, norm_g.reshape(1, d), w_pad)


def _round_robin(gens):
    gens = list(gens)
    while gens:
        for gen in list(gens):
            try:
                next(gen)
            except StopIteration:
                gens.remove(gen)
        yield


def _lane_masks(rows, head_lanes):
    if head_lanes >= LANE:
        return None
    lane = _iota((rows, LANE), 1)
    return [((lane >= o) & (lane < o + head_lanes)).astype(BF16) for o in range(0, LANE, head_lanes)]


def _block_diag(x, n_heads, lane_masks):
    L, w = x.shape
    hl = w // n_heads
    span = max(hl, LANE)
    rows = []
    for h in range(n_heads):
        c0 = (h * hl) // span * span
        piece = x[:, c0:c0 + span]
        if hl < LANE:
            piece = piece * lane_masks[(h * hl - c0) // hl]
        parts = [piece]
        if c0:
            parts.insert(0, jnp.zeros((L, c0), x.dtype))
        if w - c0 - span:
            parts.append(jnp.zeros((L, w - c0 - span), x.dtype))
        rows.append(jnp.concatenate(parts, axis=1) if len(parts) > 1 else piece)
    return jnp.concatenate(rows, axis=0)


def _head_mask(rows, row_bits, cols, col_bits):
    return (_iota((rows, cols), 0) >> row_bits) == (_iota((rows, cols), 1) >> col_bits)


def _block_sums(xs, gmat_b):
    t, width = xs[0].shape
    ng = width // GROUP
    parts = []
    for x in xs:
        xb = x.astype(BF16)
        parts += [xb[:, g * GROUP:(g + 1) * GROUP] for g in range(ng)]
    res = _mm(jnp.concatenate(parts, axis=0), gmat_b)
    return [jnp.concatenate([res[(i * ng + g) * t:(i * ng + g + 1) * t] for g in range(ng)], axis=1)
            for i in range(len(xs))]


def _cumsum_rows(x, tri_b):
    w = x.shape[1]
    res = _mm(tri_b, jnp.concatenate(_split3_bf16(x), axis=1))
    return res[:, 0:w] + (res[:, w:2 * w] + res[:, 2 * w:])


def _unit_lower_inverses(e_mats, eye, lane_masks):
    L, pw_w = e_mats[0].shape
    steps = _bits(L)
    pws = [-e for e in e_mats]
    t_invs = [eye + pw for pw in pws]
    for s in range(steps):
        first, last = s == 0, s == steps - 1
        if first and last:
            break
        for c in range(len(e_mats)):
            p_hi, p_lo = _split_bf16(pws[c])
            bd_hi = _block_diag(p_hi, PACK, lane_masks)
            bd_lo = _block_diag(p_lo, PACK, lane_masks)
            rows = ([] if first else [_split_bf16(t_invs[c])]) + ([] if last else [(p_hi, p_lo)])
            x_hi = jnp.concatenate([hi for hi, _ in rows], axis=0)
            x_lo = jnp.concatenate([lo for _, lo in rows], axis=0)
            nr = x_hi.shape[0]
            hh = _mm(jnp.concatenate([x_hi, x_lo], axis=0), bd_hi)
            res = hh[0:nr] + (hh[nr:] + _mm(x_hi, bd_lo))
            if not first:
                t_invs[c] = t_invs[c] + res[0:L]
            if not last:
                pws[c] = res[nr - L:]
        yield
    return t_invs


MAIN_DONE = "main stages done"
GDN_STAGE = ("q", "k", "qd", "kd", "bv", "bek", "beta_p", "dm", "egl")
RWKV_STAGE = ("kx", "rx", "kb", "ab", "kh", "ahh", "vr", "bonus", "g_last")


def _stage_shapes(nb, L):
    wide_b = lambda: pltpu.VMEM((2, nb, L, SEG), BF16)
    wide_f = lambda: pltpu.VMEM((2, nb, L, SEG), F32)
    row_f = lambda: pltpu.VMEM((2, nb, SUBLANE, SEG), F32)
    packed_f = lambda: pltpu.VMEM((2, nb, L, GDN_HEADS * L), F32)
    gdn = [wide_b() for _ in range(6)] + [packed_f(), packed_f(), row_f()]
    rwkv = [wide_b() for _ in range(6)] + [wide_f(), wide_f(), row_f()]
    return gdn, rwkv


def _zero_slot0(refs):
    for r in refs:
        r[0] = jnp.zeros(r.shape[1:], r.dtype)


def _commit_stage(refs):
    for r in refs:
        r[0] = r[1]


def _gdn_body(q_ref, k_ref, v_ref, z_ref, ba_ref, cw_ref, prev_ref, s0_ref, alog_ref, dtb_ref, on_ref,
              o_ref, sfin_ref, ubuf, sb, *stage_refs, nb, chunk, nsteps, pipelined):
    L = chunk
    i = pl.program_id(1)
    stage = dict(zip(GDN_STAGE, stage_refs))
    l_bits = _bits(L)
    dk_bits = _bits(GDN_DK)
    pw_w = PACK * L
    hw = PACK * GDN_DK
    n_grp = GDN_HEADS // PACK
    hp = GROUP // GDN_DK
    dsl = lambda h: slice((h % hp) * GDN_DK, (h % hp + 1) * GDN_DK)

    lm_p = _lane_masks(L, L)
    m_dd_b = _head_mask(GROUP, dk_bits, GROUP, dk_bits).astype(BF16)
    t_row = _iota((L, pw_w), 0)
    j_lane = _iota((L, pw_w), 1) & (L - 1)
    strict = j_lane < t_row
    incl = j_lane <= t_row
    eye = (j_lane == t_row).astype(F32)
    tri_b = (_iota((L, L), 0) >= _iota((L, L), 1)).astype(BF16)
    ones_b = jnp.ones((L, L), BF16)
    lane = _iota((L, LANE), 1)
    x_rows = _iota((LANE, GDN_HEADS * L + GDN_W), 0) & (GDN_HEADS - 1)
    x_cols = _iota((LANE, GDN_HEADS * L + GDN_W), 1)
    x_head = jnp.where(x_cols < GDN_HEADS * L, x_cols >> l_bits, (x_cols - GDN_HEADS * L) >> dk_bits)
    expand_b = ((x_rows == x_head) & (_iota((LANE, GDN_HEADS * L + GDN_W), 0) < 2 * GDN_HEADS)).astype(BF16)

    def bdiag(x):
        return _block_diag(x, PACK, None)

    @pl.when(i == 0)
    def _():
        sb[...] = jnp.zeros(sb.shape, F32)
        _zero_slot0(stage_refs)
        for n in range(nb):
            ubuf[n, 0:SUBLANE, :] = prev_ref[n]

    @pl.when(i == int(pipelined))
    def _():
        sb[...] = jnp.zeros(sb.shape, F32)
        for n in range(nb):
            for h in range(GDN_HEADS):
                sb[n, h // hp, dsl(h), dsl(h)] = s0_ref[n, h]

    yield

    def prepare():
        base = SUBLANE - (CONV_W - 1)
        for n in range(nb):
            ubuf[n, SUBLANE:SUBLANE + L, 0:GDN_W] = q_ref[n]
            ubuf[n, SUBLANE:SUBLANE + L, GDN_W:2 * GDN_W] = k_ref[n]
            ubuf[n, SUBLANE:SUBLANE + L, 2 * GDN_W:3 * GDN_W] = v_ref[n]
            y = ubuf[n, base:base + L, :] * cw_ref[0:1, :]
            for j in range(1, CONV_W):
                y = y + ubuf[n, base + j:base + j + L, :] * cw_ref[j:j + 1, :]
            ubuf[n, 0:SUBLANE, :] = ubuf[n, L:L + SUBLANE, :]
            act = _silu(y)
            q = act[:, 0:GDN_W]
            k = act[:, GDN_W:2 * GDN_W]
            v = act[:, 2 * GDN_W:]
            yield
            ssq, ssk = _block_sums([q * q, k * k], m_dd_b)
            q = q * lax.rsqrt(ssq + EPS) * (GDN_DK ** -0.5)
            k = k * lax.rsqrt(ssk + EPS)
            stage["q"][1, n] = q.astype(BF16)
            stage["k"][1, n] = k.astype(BF16)
            yield
            ba = ba_ref[n]
            beta_c = jnp.where(lane < GDN_HEADS, _sigmoid(ba), 0.0)
            g_c = -jnp.exp(alog_ref[...]) * _softplus(ba + dtb_ref[...])
            gc_c = jnp.where((lane >= GDN_HEADS) & (lane < 2 * GDN_HEADS), _cumsum_rows(g_c, tri_b), 0.0)
            yield
            pieces = jnp.concatenate(list(_split_bf16(beta_c)) + list(_split3_bf16(gc_c)), axis=0)
            ex = _mm(pieces, expand_b)
            beta_x = ex[0:L] + ex[L:2 * L]
            gc_x = ex[2 * L:3 * L] + (ex[3 * L:4 * L] + ex[4 * L:])
            beta_p, beta_w = beta_x[:, 0:GDN_HEADS * L], beta_x[:, GDN_HEADS * L:]
            gc_p, gc_w = gc_x[:, 0:GDN_HEADS * L], gc_x[:, GDN_HEADS * L:]
            eye8 = jnp.concatenate([eye] * n_grp, axis=1)
            incl8 = jnp.concatenate([incl] * n_grp, axis=1)
            yield
            rw = _mm(ones_b, jnp.concatenate(_split3_bf16(gc_p * eye8), axis=1))
            pw8 = GDN_HEADS * L
            gc_row = rw[:, 0:pw8] + (rw[:, pw8:2 * pw8] + rw[:, 2 * pw8:])
            stage["beta_p"][1, n] = beta_p
            stage["dm"][1, n] = jnp.where(incl8, jnp.exp(jnp.where(incl8, gc_p - gc_row, 0.0)), 0.0)
            gl_w = gc_w[L - 1:L, :]
            eg_w = jnp.exp(gc_w)
            stage["qd"][1, n] = (q * eg_w).astype(BF16)
            stage["kd"][1, n] = (k * jnp.exp(gl_w - gc_w)).astype(BF16)
            stage["bv"][1, n] = (beta_w * v).astype(BF16)
            stage["bek"][1, n] = (beta_w * eg_w * k).astype(BF16)
            stage["egl"][1, n] = jnp.broadcast_to(jnp.exp(gl_w), (SUBLANE, GDN_W))
            yield

    def chunk_algebra():
        chains = [(n, g) for g in range(n_grp) for n in range(nb)]
        wsl = lambda g: slice(g * hw, (g + 1) * hw)
        psl = lambda g: slice(g * pw_w, (g + 1) * pw_w)
        ppg = hw // GROUP
        a_mat, qkd = {}, {}
        for c in chains:
            n, sw, sp = c[0], wsl(c[1]), psl(c[1])
            k_g = stage["k"][0, n, :, sw]
            dm_g = stage["dm"][0, n, :, sp]
            prod = _mm_nt(jnp.concatenate([k_g, stage["q"][0, n, :, sw]], axis=0), bdiag(k_g))
            a_mat[c] = jnp.where(strict, stage["beta_p"][0, n, :, sp] * prod[0:L] * dm_g, 0.0)
            qkd[c] = prod[L:] * dm_g
        yield
        t_inv = dict(zip(chains, (yield from _unit_lower_inverses([a_mat[c] for c in chains], eye, lm_p))))
        sol = {}
        for c in chains:
            n, sw = c[0], wsl(c[1])
            sol[c] = _mm(t_inv[c].astype(BF16),
                         jnp.concatenate([bdiag(stage["bv"][0, n, :, sw]), bdiag(stage["bek"][0, n, :, sw])], axis=1))
        yield
        pairs = [(c, j) for c in chains for j in range(ppg)]
        s_p, r2, u_p = {}, {}, {}
        for c, j in pairs:
            pair = c[1] * ppg + j
            sg = slice(pair * GROUP, (pair + 1) * GROUP)
            s_p[c, j] = sb[c[0], pair]
            wk = sol[c][:, hw + j * GROUP:hw + (j + 1) * GROUP].astype(BF16)
            r2[c, j] = _mm(jnp.concatenate([wk, stage["qd"][0, c[0], :, sg]], axis=0), s_p[c, j].astype(BF16))
        yield
        for c, j in pairs:
            pair = c[1] * ppg + j
            sg = slice(pair * GROUP, (pair + 1) * GROUP)
            u_p[c, j] = sol[c][:, j * GROUP:(j + 1) * GROUP] - r2[c, j][0:L]
            upd = _mm_tn(stage["kd"][0, c[0], :, sg], u_p[c, j].astype(BF16))
            egl = stage["egl"][0, c[0], 0:1, sg]
            for h in range(hp):
                d = dsl(h)
                sb[c[0], pair, d, d] = s_p[c, j][d, d] * egl[:, d] + upd[d, d]
        yield
        o_g = {}
        for c in chains:
            u = jnp.concatenate([u_p[c, j] for j in range(ppg)], axis=1)
            qs = jnp.concatenate([r2[c, j][L:] for j in range(ppg)], axis=1)
            o_g[c] = qs + _mm(qkd[c].astype(BF16), bdiag(u.astype(BF16)))
        yield
        for n in range(nb):
            o = jnp.concatenate([o_g[(n, g)] for g in range(n_grp)], axis=1)
            sso, = _block_sums([o * o], m_dd_b)
            on = o * lax.rsqrt(sso * (1.0 / GDN_DV) + EPS) * on_ref[...]
            o_ref[n] = (on * _silu(z_ref[n])).astype(o_ref.dtype)
            yield

    if pipelined:
        yield from _round_robin([chunk_algebra(), prepare()])
        _commit_stage(stage_refs)
    else:
        yield from prepare()
        _commit_stage(stage_refs)
        yield from chunk_algebra()
    yield MAIN_DONE

    @pl.when(i == nsteps - 1 + int(pipelined))
    def _():
        for n in range(nb):
            for h in range(GDN_HEADS):
                sfin_ref[n, h] = sb[n, h // hp, dsl(h), dsl(h)]


def _rwkv_body(r_ref, k_ref, v_ref, xw_ref, xa_ref, z_ref, pr_ref, pk_ref, pv_ref, pxw_ref, pxa_ref, s0_ref,
               mu_ref, muw_ref, mua_ref, w0_ref, wl_ref, a0_ref, al_ref, kk_ref, ka_ref, rk_ref,
               lnw_ref, lnb_ref,
               o_ref, sfin_ref,
               sbuf, lbuf, sbt, *stage_refs, nb, chunk, nsteps, pipelined):
    i = pl.program_id(1)
    L = chunk
    stage = dict(zip(RWKV_STAGE, stage_refs))
    pw_w = PACK * L
    hd_bits = _bits(RWKV_HD)

    lm_p = _lane_masks(L, L)
    lm_k = _lane_masks(L, RWKV_HD)
    m_kk = _head_mask(GROUP, hd_bits, GROUP, hd_bits)
    m_kk_b = m_kk.astype(BF16)
    t_row = _iota((L, pw_w), 0)
    j_lane = _iota((L, pw_w), 1) & (L - 1)
    strict = j_lane < t_row
    incl = j_lane <= t_row
    eye = (j_lane == t_row).astype(F32)
    tri_b = (_iota((L, L), 0) >= _iota((L, L), 1)).astype(BF16)
    hsl = lambda h: slice((h % PACK) * RWKV_HD, (h % PACK + 1) * RWKV_HD)
    inv_hd = 1.0 / RWKV_HD

    def bdiag(x):
        return _block_diag(x, PACK, lm_k)

    @pl.when(i == 0)
    def _():
        sbt[...] = jnp.zeros(sbt.shape, F32)
        _zero_slot0(stage_refs)
        for n in range(nb):
            sbuf[n, SUBLANE - 1:SUBLANE, 0:SEG] = pr_ref[n]
            sbuf[n, SUBLANE - 1:SUBLANE, SEG:2 * SEG] = pk_ref[n]
            sbuf[n, SUBLANE - 1:SUBLANE, 2 * SEG:3 * SEG] = pv_ref[n]
            lbuf[n, SUBLANE - 1:SUBLANE, 0:LANE] = pxw_ref[n]
            lbuf[n, SUBLANE - 1:SUBLANE, LANE:2 * LANE] = pxa_ref[n]

    @pl.when(i == int(pipelined))
    def _():
        sbt[...] = jnp.zeros(sbt.shape, F32)
        for n in range(nb):
            for h in range(RWKV_HEADS):
                sbt[n, h // PACK, hsl(h), hsl(h)] = s0_ref[n, h]

    yield

    def prepare():
        for n in range(nb):
            sbuf[n, SUBLANE:SUBLANE + L, 0:SEG] = r_ref[n]
            sbuf[n, SUBLANE:SUBLANE + L, SEG:2 * SEG] = k_ref[n]
            sbuf[n, SUBLANE:SUBLANE + L, 2 * SEG:3 * SEG] = v_ref[n]
            lbuf[n, SUBLANE:SUBLANE + L, 0:LANE] = xw_ref[n]
            lbuf[n, SUBLANE:SUBLANE + L, LANE:2 * LANE] = xa_ref[n]
            cur = sbuf[n, SUBLANE:SUBLANE + L, :]
            prv = sbuf[n, SUBLANE - 1:SUBLANE - 1 + L, :]
            rkv = cur + (prv - cur) * mu_ref[...]
            curl = lbuf[n, SUBLANE:SUBLANE + L, :]
            prvl = lbuf[n, SUBLANE - 1:SUBLANE - 1 + L, :]
            xw = curl[:, 0:LANE] + (prvl[:, 0:LANE] - curl[:, 0:LANE]) * muw_ref[...]
            xa = curl[:, LANE:] + (prvl[:, LANE:] - curl[:, LANE:]) * mua_ref[...]
            sbuf[n, 0:SUBLANE, :] = sbuf[n, L:L + SUBLANE, :]
            lbuf[n, 0:SUBLANE, :] = lbuf[n, L:L + SUBLANE, :]
            r = rkv[:, 0:SEG]
            kr = rkv[:, SEG:2 * SEG]
            vr = rkv[:, 2 * SEG:3 * SEG]
            stage["vr"][1, n] = vr
            yield
            wl = w0_ref[...] + _mm(jnp.tanh(xw).astype(BF16), wl_ref[...])
            logw = -math.exp(-W_DECAY_OFFSET) * _sigmoid(wl)
            a = _sigmoid(a0_ref[...] + _mm(xa.astype(BF16), al_ref[...]))
            kkr = kr * kk_ref[...]
            kt = kr * (1.0 + (a - 1.0) * ka_ref[...])
            yield
            ss, bonus = _block_sums([kkr * kkr, r * kt * rk_ref[...]], m_kk_b)
            stage["bonus"][1, n] = bonus
            kk = kkr * lax.rsqrt(ss + EPS)
            ah = a * kk
            yield
            cum = _cumsum_rows(logw, tri_b)
            c_last = cum[L - 1:L, :]
            e_neg = jnp.exp(-cum)
            e_last = jnp.exp(c_last - cum)
            stage["kx"][1, n] = (kk * jnp.exp(cum - logw)).astype(BF16)
            stage["rx"][1, n] = (r * jnp.exp(cum)).astype(BF16)
            stage["kb"][1, n] = (kt * e_neg).astype(BF16)
            stage["ab"][1, n] = (ah * e_neg).astype(BF16)
            stage["kh"][1, n] = (kt * e_last).astype(BF16)
            stage["ahh"][1, n] = (ah * e_last).astype(BF16)
            stage["g_last"][1, n] = jnp.broadcast_to(jnp.exp(c_last), (SUBLANE, RWKV_W))
            yield

    def chunk_algebra():
        chains = [(n, g) for g in range(RWKV_GROUPS) for n in range(nb)]
        gsl = lambda g: slice(g * GROUP, (g + 1) * GROUP)
        get = lambda name, c: stage[name][0, c[0], :, gsl(c[1])]
        lhs, c_mat, e_mat, rk_mat, ra_mat = {}, {}, {}, {}, {}
        for c in chains:
            lhs[c] = jnp.concatenate([get("kx", c), get("rx", c)], axis=0)
            rhs_nt = jnp.concatenate([bdiag(get("kb", c)), bdiag(get("ab", c))], axis=0)
            prod = _mm_nt(lhs[c], rhs_nt)
            c_mat[c] = jnp.where(strict, prod[0:L, 0:pw_w], 0.0)
            e_mat[c] = jnp.where(strict, prod[0:L, pw_w:], 0.0)
            rk_mat[c] = jnp.where(incl, prod[L:, 0:pw_w], 0.0)
            ra_mat[c] = jnp.where(incl, prod[L:, pw_w:], 0.0)
        yield
        t_inv = dict(zip(chains, (yield from _unit_lower_inverses([e_mat[c] for c in chains], eye, lm_p))))
        s_g, ks_rs, cv, u, vr_b = {}, {}, {}, {}, {}
        for c in chains:
            s_g[c] = sbt[c[0], c[1]]
            vr_b[c] = get("vr", c).astype(BF16)
            ks_rs[c] = _mm_nt(lhs[c], s_g[c].astype(BF16))
            cv[c] = _mm(jnp.concatenate([c_mat[c], rk_mat[c]], axis=0).astype(BF16), bdiag(vr_b[c]))
        yield
        for c in chains:
            u[c] = _mm(t_inv[c].astype(BF16), bdiag((ks_rs[c][0:L] + cv[c][0:L]).astype(BF16)))
        yield
        y_g = {}
        for c in chains:
            sl = gsl(c[1])
            u_b = u[c].astype(BF16)
            y_g[c] = ks_rs[c][L:] + cv[c][L:] - _mm(ra_mat[c].astype(BF16), bdiag(u_b))
            upd = _mm_tn(jnp.concatenate([vr_b[c], -u_b], axis=0),
                         jnp.concatenate([get("kh", c), get("ahh", c)], axis=0))
            gl = stage["g_last"][0, c[0], 0:1, sl]
            for v in range(GROUP // LANE):
                d = slice(v * LANE, (v + 1) * LANE)
                sbt[c[0], c[1], d, d] = s_g[c][d, d] * gl[:, d] + jnp.where(m_kk[d, d], upd[d, d], 0.0)
        yield
        for n in range(nb):
            y = jnp.concatenate([y_g[(n, g)] for g in range(RWKV_GROUPS)], axis=1)
            dlt = y - _block_sums([y], m_kk_b)[0] * inv_hd
            var = _block_sums([dlt * dlt], m_kk_b)[0] * inv_hd
            yn = dlt * lax.rsqrt(var + GN_EPS) * lnw_ref[...] + lnb_ref[...]
            o_ref[n] = ((yn + stage["bonus"][0, n] * stage["vr"][0, n]) * _silu(z_ref[n])).astype(o_ref.dtype)
            yield

    if pipelined:
        yield from _round_robin([chunk_algebra(), prepare()])
        _commit_stage(stage_refs)
    else:
        yield from prepare()
        _commit_stage(stage_refs)
        yield from chunk_algebra()
    yield MAIN_DONE

    @pl.when(i == nsteps - 1 + int(pipelined))
    def _():
        for n in range(nb):
            for h in range(RWKV_HEADS):
                sfin_ref[n, h] = sbt[n, h // PACK, hsl(h), hsl(h)]


N_GDN_IN, N_RWKV_IN = 11, 24


def _scan_kernel(*refs, nb, chunk, nsteps, pipelined):
    g_in = refs[0:N_GDN_IN]
    r_in = refs[N_GDN_IN:N_GDN_IN + N_RWKV_IN]
    og_ref, sg_ref, or_ref, sr_ref, ubuf, sb, sbuf, lbuf, sbt = refs[N_GDN_IN + N_RWKV_IN:N_GDN_IN + N_RWKV_IN + 9]
    stage = refs[N_GDN_IN + N_RWKV_IN + 9:]
    g_stage, r_stage = stage[:len(GDN_STAGE)], stage[len(GDN_STAGE):]
    kw = dict(nb=nb, chunk=chunk, nsteps=nsteps, pipelined=pipelined)
    bodies = [_gdn_body(*g_in, og_ref, sg_ref, ubuf, sb, *g_stage, **kw),
              _rwkv_body(*r_in, or_ref, sr_ref, sbuf, lbuf, sbt, *r_stage, **kw)]
    for body in bodies:
        next(body)
    active = list(bodies)
    while active:
        for body in list(active):
            if next(body) is MAIN_DONE:
                active.remove(body)
    for body in bodies:
        for _ in body:
            pass


def _scans(proj3, conv_prev, s_gdn, shift_prev, s_rwkv, p, nb):
    b, t, _ = proj3.shape
    L = min(GDN_CHUNK, t)
    assert b % nb == 0 and t % L == 0 and L % SUBLANE == 0
    nsteps = t // L
    pipelined = nsteps > 1
    lagged = int(pipelined)
    blk = lambda c, w: pl.BlockSpec((nb, L, w), lambda g, i, c=c: (g, jnp.minimum(i, nsteps - 1), c))
    lag = lambda c, w: pl.BlockSpec((nb, L, w), lambda g, i, c=c: (g, jnp.maximum(i - lagged, 0), c))
    const = lambda shape: pl.BlockSpec(shape, lambda g, i: (0,) * len(shape))
    per_seq = lambda *shape: pl.BlockSpec((nb,) + shape, lambda g, i: (g,) + (0,) * len(shape))
    prev_pad = jnp.pad(conv_prev, ((0, 0), (SUBLANE - (CONV_W - 1), 0), (0, 0)))
    lane_pad = (GDN_HEADS, LANE - 2 * GDN_HEADS)
    alog_row = jnp.pad(p["a_log"], lane_pad).reshape(1, LANE)
    dtb_row = jnp.pad(p["dt_bias"], lane_pad).reshape(1, LANE)
    on_row = jnp.tile(p["onorm_g"], GDN_HEADS).reshape(1, GDN_W)
    gdn_state = per_seq(GDN_HEADS, GDN_DK, GDN_DV)
    gdn_specs = [
        blk(COL_Q, SEG), blk(COL_K, SEG), blk(COL_V, SEG), lag(COL_ZG, SEG), blk(COL_BA // LANE, LANE),
        const((CONV_W, 3 * GDN_W)), per_seq(SUBLANE, 3 * GDN_W), gdn_state,
        const((1, LANE)), const((1, LANE)), const((1, GDN_W)),
    ]
    gdn_args = [proj3] * 5 + [p["conv_w"], prev_pad, s_gdn, alog_row, dtb_row, on_row]
    w3 = 3 * RWKV_W
    padl = lambda x, n: jnp.pad(x, [(0, 0)] * (x.ndim - 1) + [(0, LANE - n)])
    p_r, p_k, p_v = shift_prev[..., 0:RWKV_W], shift_prev[..., RWKV_W:2 * RWKV_W], shift_prev[..., 2 * RWKV_W:w3]
    p_xw = padl(shift_prev[..., w3:w3 + W_LORA], W_LORA)
    p_xa = padl(shift_prev[..., w3 + W_LORA:], A_LORA)
    mu = p["mu"]
    mu_rkv = mu[0:w3].reshape(1, w3)
    mu_w = padl(mu[w3:w3 + W_LORA], W_LORA).reshape(1, LANE)
    mu_a = padl(mu[w3 + W_LORA:], A_LORA).reshape(1, LANE)
    wl_pad = jnp.pad(p["w_lora"], ((0, LANE - W_LORA), (0, 0))).astype(BF16)
    al_pad = jnp.pad(p["a_lora"], ((0, LANE - A_LORA), (0, 0))).astype(BF16)
    s0_t = s_rwkv.transpose(0, 1, 3, 2)
    row1 = lambda x: x.reshape(1, RWKV_W)
    rwkv_state = per_seq(RWKV_HEADS, RWKV_HD, RWKV_HD)
    rwkv_specs = [
        blk(COL_R, SEG), blk(COL_RK, SEG), blk(COL_RV, SEG),
        blk(COL_XW // LANE, LANE), blk(COL_XA // LANE, LANE), lag(COL_ZR, SEG),
        per_seq(1, SEG), per_seq(1, SEG), per_seq(1, SEG), per_seq(1, LANE), per_seq(1, LANE),
        rwkv_state,
        const((1, w3)), const((1, LANE)), const((1, LANE)),
        const((1, RWKV_W)), const((LANE, RWKV_W)), const((1, RWKV_W)), const((LANE, RWKV_W)),
        const((1, RWKV_W)), const((1, RWKV_W)), const((1, RWKV_W)), const((1, RWKV_W)), const((1, RWKV_W)),
    ]
    rwkv_args = [proj3] * 6 + [p_r, p_k, p_v, p_xw, p_xa, s0_t, mu_rkv, mu_w, mu_a, row1(p["w0"]), wl_pad,
                               row1(p["a0"]), al_pad, row1(p["k_k"]), row1(p["k_a"]), row1(p["r_k"]),
                               row1(p["ln_w"]), row1(p["ln_b"])]
    assert len(gdn_specs) == N_GDN_IN and len(rwkv_specs) == N_RWKV_IN
    gdn_stage, rwkv_stage = _stage_shapes(nb, L)
    o_g, sg, o_r, sr = pl.pallas_call(
        functools.partial(_scan_kernel, nb=nb, chunk=L, nsteps=nsteps, pipelined=pipelined),
        grid=(b // nb, nsteps + lagged),
        in_specs=gdn_specs + rwkv_specs,
        out_specs=[
            lag(0, GDN_W), gdn_state,
            lag(0, RWKV_W), rwkv_state,
        ],
        out_shape=[
            jax.ShapeDtypeStruct((b, t, GDN_W), BF16),
            jax.ShapeDtypeStruct((b, GDN_HEADS, GDN_DK, GDN_DV), F32),
            jax.ShapeDtypeStruct((b, t, RWKV_W), BF16),
            jax.ShapeDtypeStruct((b, RWKV_HEADS, RWKV_HD, RWKV_HD), F32),
        ],
        scratch_shapes=[
            pltpu.VMEM((nb, L + SUBLANE, 3 * GDN_W), F32),
            pltpu.VMEM((nb, GDN_PAIRS, GROUP, GROUP), F32),
            pltpu.VMEM((nb, L + SUBLANE, w3), F32),
            pltpu.VMEM((nb, L + SUBLANE, 2 * LANE), F32),
            pltpu.VMEM((nb, RWKV_GROUPS, GROUP, GROUP), F32),
        ] + gdn_stage + rwkv_stage,
        compiler_params=pltpu.CompilerParams(
            dimension_semantics=("arbitrary", "arbitrary"), vmem_limit_bytes=VMEM_LIMIT),
        name="scan",
    )(*gdn_args, *rwkv_args)
    return o_g.reshape(b * t, GDN_W), sg, o_r.reshape(b * t, RWKV_W), sr.transpose(0, 1, 3, 2)


def _merge_kernel(og_ref, or_ref, gg_ref, gr_ref, x_ref, gate_ref, wog_ref, wor_ref, wout_ref, fg_ref, o_ref,
                  *, final_norm):
    m = _sigmoid(gg_ref[...]) * _mm(og_ref[...], wog_ref[...]) \
        + _sigmoid(gr_ref[...]) * _mm(or_ref[...], wor_ref[...])
    out = _mm(m.astype(BF16), wout_ref[...])
    xn = x_ref[...] + gate_ref[...] * out
    if final_norm:
        xn = xn * lax.rsqrt(jnp.mean(xn * xn, axis=-1, keepdims=True) + EPS) * fg_ref[...]
    o_ref[...] = xn


def _merge(o_g, o_r, proj, x2d, gate, w_og, w_or, w_out, final_g, rows_per_mod, final_norm):
    m, d = x2d.shape
    tm = min(m, 256) if rows_per_mod == 1 else min(rows_per_mod, 256)
    if rows_per_mod == 1:
        gate_spec = pl.BlockSpec((tm, d), lambda i: (i, 0))
    else:
        assert rows_per_mod % tm == 0
        per = rows_per_mod // tm
        gate = gate.reshape(-1, 1, d)
        gate_spec = pl.BlockSpec((None, 1, d), lambda i: (i // per, 0, 0))
    whole = lambda shape: pl.BlockSpec(shape, lambda i: (0, 0))
    gcol = COL_GATES // d
    return pl.pallas_call(
        functools.partial(_merge_kernel, final_norm=final_norm),
        grid=(m // tm,),
        in_specs=[
            pl.BlockSpec((tm, GDN_W), lambda i: (i, 0)),
            pl.BlockSpec((tm, RWKV_W), lambda i: (i, 0)),
            pl.BlockSpec((tm, d), lambda i: (i, gcol)),
            pl.BlockSpec((tm, d), lambda i: (i, gcol + 1)),
            pl.BlockSpec((tm, d), lambda i: (i, 0)),
            gate_spec,
            whole((GDN_W, d)), whole((RWKV_W, d)), whole((d, d)), whole((1, d)),
        ],
        out_specs=pl.BlockSpec((tm, d), lambda i: (i, 0)),
        out_shape=jax.ShapeDtypeStruct((m, d), F32),
        compiler_params=pltpu.CompilerParams(
            dimension_semantics=("arbitrary",), vmem_limit_bytes=VMEM_LIMIT),
        name="merge_out",
    )(o_g, o_r, proj, proj, x2d, gate, w_og, w_or, w_out, final_g.reshape(1, d))


def _pad_in_weight(w_in):
    d = w_in.shape[0]
    o_zg = 3 * GDN_W
    o_b = o_zg + GDN_W
    o_rw = o_b + 2 * GDN_HEADS
    o_xw = o_rw + 3 * RWKV_W
    o_xa = o_xw + W_LORA
    o_zr = o_xa + A_LORA
    o_br = o_zr + RWKV_W
    z = lambda n: jnp.zeros((d, n), w_in.dtype)
    cols = [
        w_in[:, 0:o_zg], w_in[:, o_zg:o_b],
        w_in[:, o_rw:o_xw], w_in[:, o_zr:o_br],
        w_in[:, o_br:o_br + 2 * D_MODEL],
        w_in[:, o_b:o_rw], z(LANE - 2 * GDN_HEADS),
        w_in[:, o_xw:o_xa], z(LANE - W_LORA),
        w_in[:, o_xa:o_zr], z(LANE - A_LORA),
    ]
    used = COL_XA + LANE
    cols.append(z(N_PROJ - used))
    return jnp.concatenate(cols, axis=1).astype(BF16)


def _layer(x2d, batch, seq, scale, shift, gate, rows_per_mod, conv_prev, s_gdn, shift_prev, s_rwkv, p, final_g,
           final_norm, scan_nb):
    proj = _in_proj(x2d, scale, shift, p["norm_g"], p["w_in_pad"], rows_per_mod)
    proj3 = proj.reshape(batch, seq, N_PROJ)
    o_g, s_gdn_new, o_r, s_rwkv_new = _scans(proj3, conv_prev, s_gdn, shift_prev, s_rwkv, p, scan_nb)
    x_new = _merge(o_g, o_r, proj, x2d, gate, p["w_o_gdn"], p["w_o_rwkv"], p["w_out"], final_g, rows_per_mod,
                   final_norm)
    conv_new = proj3[:, seq - (CONV_W - 1):, 0:3 * GDN_W]
    last = proj3[:, seq - 1:, :]
    shift_new = jnp.concatenate(
        [last[..., COL_R * SEG:COL_R * SEG + 3 * RWKV_W], last[..., COL_XW:COL_XW + W_LORA],
         last[..., COL_XA:COL_XA + A_LORA]], axis=-1)
    return x_new, conv_new, s_gdn_new, shift_new, s_rwkv_new


def _forward(x_prompt, x_sample, c_prompt, c_sample, cache_gdn_conv, state_gdn, cache_rwkv_shift, state_rwkv,
             ada_w, ada_b, norm_g, w_in, gdn_conv_w, gdn_a_log, gdn_dt_bias, gdn_out_norm_g,
             rwkv_mu, rwkv_w0, rwkv_w_lora, rwkv_a0, rwkv_a_lora, rwkv_k_k, rwkv_k_a, rwkv_r_k,
             rwkv_ln_w, rwkv_ln_b, w_o_gdn, w_o_rwkv, w_out, final_norm_g):
    depth = ada_w.shape[0]
    bp, tp, d = x_prompt.shape
    bs, ts, _ = x_sample.shape
    assert CONV_W - 1 <= min(tp, ts)
    c_all = jnp.concatenate([c_prompt, c_sample], axis=0)
    rows = -(-(bp + bs) // SUBLANE) * SUBLANE
    c_all = jnp.pad(c_all, ((0, rows - (bp + bs)), (0, 0)))
    mod = _ada_mod(c_all, ada_w, ada_b)
    xp = x_prompt.reshape(bp * tp, d)
    xs = x_sample.reshape(bs * ts, d)
    outs = [[] for _ in range(8)]
    for l in range(depth):
        p = dict(norm_g=norm_g[l], w_in_pad=_pad_in_weight(w_in[l]), conv_w=gdn_conv_w[l], a_log=gdn_a_log[l],
                 dt_bias=gdn_dt_bias[l], onorm_g=gdn_out_norm_g[l], mu=rwkv_mu[l], w0=rwkv_w0[l],
                 w_lora=rwkv_w_lora[l], a0=rwkv_a0[l], a_lora=rwkv_a_lora[l], k_k=rwkv_k_k[l], k_a=rwkv_k_a[l],
                 r_k=rwkv_r_k[l].reshape(-1), ln_w=rwkv_ln_w[l], ln_b=rwkv_ln_b[l],
                 w_o_gdn=w_o_gdn[l].astype(BF16), w_o_rwkv=w_o_rwkv[l].astype(BF16), w_out=w_out[l].astype(BF16))
        last = l == depth - 1
        m_p = mod[l, 0:bp]
        m_s = jnp.repeat(mod[l, bp:bp + bs], ts, axis=0)
        sh_p, sc_p, gt_p = m_p[:, 0:d], m_p[:, d:2 * d], m_p[:, 2 * d:]
        sh_s, sc_s, gt_s = m_s[:, 0:d], m_s[:, d:2 * d], m_s[:, 2 * d:]
        zeros = lambda *s: jnp.zeros(s, F32)
        xp, c1, g1, h1, r1 = _layer(
            xp, bp, tp, sc_p, sh_p, gt_p, tp,
            zeros(bp, CONV_W - 1, 3 * GDN_W), zeros(bp, GDN_HEADS, GDN_DK, GDN_DV),
            zeros(bp, 1, 3 * RWKV_W + W_LORA + A_LORA), zeros(bp, RWKV_HEADS, RWKV_HD, RWKV_HD),
            p, final_norm_g, last, scan_nb=bp)
        xs, c2, g2, h2, r2 = _layer(
            xs, bs, ts, sc_s, sh_s, gt_s, 1,
            cache_gdn_conv[l], state_gdn[l], cache_rwkv_shift[l], state_rwkv[l],
            p, final_norm_g, last, scan_nb=min(bs, 2))
        for lst, val in zip(outs, (c1, g1, h1, r1, c2, g2, h2, r2)):
            lst.append(val)
    stk = [jnp.stack(v) for v in outs]
    return (xp.reshape(bp, tp, d), xs.reshape(bs, ts, d), *stk)


def kernel(x_prompt, x_sample, c_prompt, c_sample, cache_gdn_conv, state_gdn, cache_rwkv_shift, state_rwkv, ada_w, ada_b, norm_g, w_in, gdn_conv_w, gdn_a_log, gdn_dt_bias, gdn_out_norm_g, rwkv_mu, rwkv_w0, rwkv_w_lora, rwkv_a0, rwkv_a_lora, rwkv_k_k, rwkv_k_a, rwkv_r_k, rwkv_ln_w, rwkv_ln_b, w_o_gdn, w_o_rwkv, w_out, final_norm_g):
    return _forward(x_prompt, x_sample, c_prompt, c_sample, cache_gdn_conv, state_gdn, cache_rwkv_shift,
                    state_rwkv, ada_w, ada_b, norm_g, w_in, gdn_conv_w, gdn_a_log, gdn_dt_bias, gdn_out_norm_g,
                    rwkv_mu, rwkv_w0, rwkv_w_lora, rwkv_a0, rwkv_a_lora, rwkv_k_k, rwkv_k_a, rwkv_r_k,
                    rwkv_ln_w, rwkv_ln_b, w_o_gdn, w_o_rwkv, w_out, final_norm_g)
```

```python
import functools
import math

import jax
import jax.numpy as jnp
from jax import lax
from jax.experimental import pallas as pl
from jax.experimental.pallas import tpu as pltpu

F32 = jnp.float32
BF16 = jnp.bfloat16

D_MODEL = 2048
GDN_HEADS = 8
GDN_DK = 128
GDN_DV = 128
GDN_W = GDN_HEADS * GDN_DK
GDN_CHUNK = 64
CONV_W = 4
RWKV_HEADS = 16
RWKV_HD = 64
RWKV_W = RWKV_HEADS * RWKV_HD
W_LORA = 96
A_LORA = 96
EPS = 1e-6
GN_EPS = 64e-5
W_DECAY_OFFSET = 0.5

LANE = 128
SUBLANE = 8
SEG = 1024
COL_Q, COL_K, COL_V, COL_ZG, COL_R, COL_RK, COL_RV, COL_ZR = 0, 1, 2, 3, 4, 5, 6, 7
COL_GATES = 8 * SEG
COL_BA = 12 * SEG
COL_XW = COL_BA + LANE
COL_XA = COL_XW + LANE
GROUP = 256
PROJ_TILE = 5 * GROUP
N_PROJ = 10 * PROJ_TILE
PACK = 4
RWKV_GROUPS = RWKV_W // GROUP
GDN_PAIRS = GDN_W // GROUP
VMEM_LIMIT = 56 * 1024 * 1024


def _mm(a, b):
    return jnp.dot(a, b, preferred_element_type=F32)


def _mm_nt(a, b):
    return lax.dot_general(a, b, (((1,), (1,)), ((), ())), preferred_element_type=F32)


def _mm_tn(a, b):
    return lax.dot_general(a, b, (((0,), (0,)), ((), ())), preferred_element_type=F32)


def _split_bf16(x):
    hi = x.astype(BF16)
    return hi, (x - hi.astype(F32)).astype(BF16)


def _split3_bf16(x):
    hi = x.astype(BF16)
    mid, lo = _split_bf16(x - hi.astype(F32))
    return hi, mid, lo


def _sigmoid(x):
    return 1.0 / (1.0 + jnp.exp(-x))


def _silu(x):
    return x * _sigmoid(x)


def _softplus(x):
    return jnp.maximum(x, 0.0) + jnp.log1p(jnp.exp(-jnp.abs(x)))


def _iota(shape, dim):
    return lax.broadcasted_iota(jnp.int32, shape, dim)


def _bits(n):
    assert n & (n - 1) == 0
    return n.bit_length() - 1


def _mod_kernel(c_ref, w_ref, b_ref, o_ref):
    o_ref[...] = _mm(c_ref[...], w_ref[...]) + b_ref[...]


def _ada_mod(c_all, ada_w, ada_b):
    depth, d, n3 = ada_w.shape
    rows = c_all.shape[0]
    tn = 512
    return pl.pallas_call(
        _mod_kernel,
        grid=(depth, n3 // tn),
        in_specs=[
            pl.BlockSpec((rows, d), lambda l, j: (0, 0)),
            pl.BlockSpec((None, d, tn), lambda l, j: (l, 0, j)),
            pl.BlockSpec((None, 1, tn), lambda l, j: (l, 0, j)),
        ],
        out_specs=pl.BlockSpec((None, rows, tn), lambda l, j: (l, 0, j)),
        out_shape=jax.ShapeDtypeStruct((depth, rows, n3), F32),
        compiler_params=pltpu.CompilerParams(
            dimension_semantics=("arbitrary", "arbitrary"), vmem_limit_bytes=VMEM_LIMIT),
        name="ada_mod",
    )(c_all, ada_w, ada_b.reshape(depth, 1, n3))


def _in_proj_kernel(x_ref, sc_ref, sh_ref, g_ref, w_ref, o_ref, h_ref):
    @pl.when(pl.program_id(1) == 0)
    def _():
        x = x_ref[...]
        ms = jnp.mean(x * x, axis=-1, keepdims=True)
        h = x * lax.rsqrt(ms + EPS) * g_ref[...] * (1.0 + sc_ref[...]) + sh_ref[...]
        h_ref[...] = h.astype(BF16)

    o_ref[...] = _mm(h_ref[...], w_ref[...])


def _in_proj(x2d, scale, shift, norm_g, w_pad, rows_per_mod):
    m, d = x2d.shape
    tm = min(m, 512) if rows_per_mod == 1 else min(rows_per_mod, 1024)
    tn = PROJ_TILE
    if rows_per_mod == 1:
        mod_spec = pl.BlockSpec((tm, d), lambda i, j: (i, 0))
    else:
        assert rows_per_mod % tm == 0
        per = rows_per_mod // tm
        scale = scale.reshape(-1, 1, d)
        shift = shift.reshape(-1, 1, d)
        mod_spec = pl.BlockSpec((None, 1, d), lambda i, j: (i // per, 0, 0))
    return pl.pallas_call(
        _in_proj_kernel,
        grid=(m // tm, N_PROJ // tn),
        in_specs=[
            pl.BlockSpec((tm, d), lambda i, j: (i, 0)),
            mod_spec, mod_spec,
            pl.BlockSpec((1, d), lambda i, j: (0, 0)),
            pl.BlockSpec((d, tn), lambda i, j: (0, j)),
        ],
        out_specs=pl.BlockSpec((tm, tn), lambda i, j: (i, j)),
        out_shape=jax.ShapeDtypeStruct((m, N_PROJ), F32),
        scratch_shapes=[pltpu.VMEM((tm, d), BF16)],
        compiler_params=pltpu.CompilerParams(
            dimension_semantics=("arbitrary", "arbitrary"), vmem_limit_bytes=VMEM_LIMIT),
        name="in_proj",
    )(x2d, scale, shift, norm_g.reshape(1, d), w_pad)


def _lane_masks(rows, head_lanes):
    if head_lanes >= LANE:
        return None
    lane = _iota((rows, LANE), 1)
    return [((lane >= o) & (lane < o + head_lanes)).astype(BF16) for o in range(0, LANE, head_lanes)]


def _block_diag(x, n_heads, lane_masks):
    L, w = x.shape
    hl = w // n_heads
    span = max(hl, LANE)
    rows = []
    for h in range(n_heads):
        c0 = (h * hl) // span * span
        piece = x[:, c0:c0 + span]
        if hl < LANE:
            piece = piece * lane_masks[(h * hl - c0) // hl]
        parts = [piece]
        if c0:
            parts.insert(0, jnp.zeros((L, c0), x.dtype))
        if w - c0 - span:
            parts.append(jnp.zeros((L, w - c0 - span), x.dtype))
        rows.append(jnp.concatenate(parts, axis=1) if len(parts) > 1 else piece)
    return jnp.concatenate(rows, axis=0)


def _head_mask(rows, row_bits, cols, col_bits):
    return (_iota((rows, cols), 0) >> row_bits) == (_iota((rows, cols), 1) >> col_bits)


def _block_sums(xs, gmat_b):
    t, width = xs[0].shape
    ng = width // GROUP
    parts = []
    for x in xs:
        xb = x.astype(BF16)
        parts += [xb[:, g * GROUP:(g + 1) * GROUP] for g in range(ng)]
    res = _mm(jnp.concatenate(parts, axis=0), gmat_b)
    return [jnp.concatenate([res[(i * ng + g) * t:(i * ng + g + 1) * t] for g in range(ng)], axis=1)
            for i in range(len(xs))]


def _cumsum_rows(x, tri_b):
    w = x.shape[1]
    res = _mm(tri_b, jnp.concatenate(_split3_bf16(x), axis=1))
    return res[:, 0:w] + (res[:, w:2 * w] + res[:, 2 * w:])


def _unit_lower_inverses(e_mats, eye, lane_masks):
    L, pw_w = e_mats[0].shape
    steps = _bits(L)
    pws = [-e for e in e_mats]
    t_invs = [eye + pw for pw in pws]
    for s in range(steps):
        first, last = s == 0, s == steps - 1
        if first and last:
            break
        for c in range(len(e_mats)):
            p_hi, p_lo = _split_bf16(pws[c])
            bd_hi = _block_diag(p_hi, PACK, lane_masks)
            bd_lo = _block_diag(p_lo, PACK, lane_masks)
            rows = ([] if first else [_split_bf16(t_invs[c])]) + ([] if last else [(p_hi, p_lo)])
            x_hi = jnp.concatenate([hi for hi, _ in rows], axis=0)
            x_lo = jnp.concatenate([lo for _, lo in rows], axis=0)
            nr = x_hi.shape[0]
            hh = _mm(jnp.concatenate([x_hi, x_lo], axis=0), bd_hi)
            res = hh[0:nr] + (hh[nr:] + _mm(x_hi, bd_lo))
            if not first:
                t_invs[c] = t_invs[c] + res[0:L]
            if not last:
                pws[c] = res[nr - L:]
        yield
    return t_invs


def _gdn_body(q_ref, k_ref, v_ref, z_ref, ba_ref, cw_ref, prev_ref, s0_ref, alog_ref, dtb_ref, on_ref,
              o_ref, sfin_ref, ubuf, sb, *, nb, chunk, nsteps):
    L = chunk
    i = pl.program_id(1)
    l_bits = _bits(L)
    dk_bits = _bits(GDN_DK)
    pw_w = PACK * L
    hw = PACK * GDN_DK
    n_grp = GDN_HEADS // PACK
    hp = GROUP // GDN_DK
    dsl = lambda h: slice((h % hp) * GDN_DK, (h % hp + 1) * GDN_DK)

    lm_p = _lane_masks(L, L)
    m_dd = _head_mask(GROUP, dk_bits, GROUP, dk_bits)
    m_dd_b = m_dd.astype(BF16)
    t_row = _iota((L, pw_w), 0)
    j_lane = _iota((L, pw_w), 1) & (L - 1)
    strict = j_lane < t_row
    incl = j_lane <= t_row
    eye = (j_lane == t_row).astype(F32)
    tri_b = (_iota((L, L), 0) >= _iota((L, L), 1)).astype(BF16)
    ones_b = jnp.ones((L, L), BF16)
    lane = _iota((L, LANE), 1)
    x_rows = _iota((LANE, GDN_HEADS * L + GDN_W), 0) & (GDN_HEADS - 1)
    x_cols = _iota((LANE, GDN_HEADS * L + GDN_W), 1)
    x_head = jnp.where(x_cols < GDN_HEADS * L, x_cols >> l_bits, (x_cols - GDN_HEADS * L) >> dk_bits)
    expand_b = ((x_rows == x_head) & (_iota((LANE, GDN_HEADS * L + GDN_W), 0) < 2 * GDN_HEADS)).astype(BF16)

    def bdiag(x):
        return _block_diag(x.astype(BF16), PACK, None)

    @pl.when(i == 0)
    def _():
        sb[...] = jnp.zeros(sb.shape, F32)
        for n in range(nb):
            ubuf[n, 0:SUBLANE, :] = prev_ref[n]
            for h in range(GDN_HEADS):
                sb[n, h // hp, dsl(h), dsl(h)] = s0_ref[n, h]

    yield
    base = SUBLANE - (CONV_W - 1)
    pre = []
    for n in range(nb):
        ubuf[n, SUBLANE:SUBLANE + L, 0:GDN_W] = q_ref[n]
        ubuf[n, SUBLANE:SUBLANE + L, GDN_W:2 * GDN_W] = k_ref[n]
        ubuf[n, SUBLANE:SUBLANE + L, 2 * GDN_W:3 * GDN_W] = v_ref[n]
        y = ubuf[n, base:base + L, :] * cw_ref[0:1, :]
        for j in range(1, CONV_W):
            y = y + ubuf[n, base + j:base + j + L, :] * cw_ref[j:j + 1, :]
        ubuf[n, 0:SUBLANE, :] = ubuf[n, L:L + SUBLANE, :]
        act = _silu(y)
        q = act[:, 0:GDN_W]
        k = act[:, GDN_W:2 * GDN_W]
        v = act[:, 2 * GDN_W:]
        ssq, ssk = _block_sums([q * q, k * k], m_dd_b)
        q = q * lax.rsqrt(ssq + EPS) * (GDN_DK ** -0.5)
        k = k * lax.rsqrt(ssk + EPS)

        ba = ba_ref[n]
        beta_c = jnp.where(lane < GDN_HEADS, _sigmoid(ba), 0.0)
        g_c = -jnp.exp(alog_ref[...]) * _softplus(ba + dtb_ref[...])
        gc_c = jnp.where((lane >= GDN_HEADS) & (lane < 2 * GDN_HEADS), _cumsum_rows(g_c, tri_b), 0.0)
        pieces = jnp.concatenate(list(_split_bf16(beta_c)) + list(_split3_bf16(gc_c)), axis=0)
        ex = _mm(pieces, expand_b)
        beta_x = ex[0:L] + ex[L:2 * L]
        gc_x = ex[2 * L:3 * L] + (ex[3 * L:4 * L] + ex[4 * L:])
        beta_p, beta_w = beta_x[:, 0:GDN_HEADS * L], beta_x[:, GDN_HEADS * L:]
        gc_p, gc_w = gc_x[:, 0:GDN_HEADS * L], gc_x[:, GDN_HEADS * L:]
        eye8 = jnp.concatenate([eye] * n_grp, axis=1)
        incl8 = jnp.concatenate([incl] * n_grp, axis=1)
        rw = _mm(ones_b, jnp.concatenate(_split3_bf16(gc_p * eye8), axis=1))
        pw8 = GDN_HEADS * L
        gc_row = rw[:, 0:pw8] + (rw[:, pw8:2 * pw8] + rw[:, 2 * pw8:])
        dm = jnp.where(incl8, jnp.exp(jnp.where(incl8, gc_p - gc_row, 0.0)), 0.0)
        gl_w = gc_w[L - 1:L, :]
        eg_w = jnp.exp(gc_w)
        qd = q * eg_w
        kd = k * jnp.exp(gl_w - gc_w)
        bv = beta_w * v
        bek = beta_w * eg_w * k
        pre.append(dict(q=q, k=k, beta_p=beta_p, dm=dm, qd=qd, kd=kd, bv=bv, bek=bek, egl_w=jnp.exp(gl_w)))
        yield

    chains = [(n, g) for g in range(n_grp) for n in range(nb)]
    wsl = lambda g: slice(g * hw, (g + 1) * hw)
    psl = lambda g: slice(g * pw_w, (g + 1) * pw_w)
    ppg = hw // GROUP
    a_mat, qkd = {}, {}
    for c in chains:
        p, sw, sp = pre[c[0]], wsl(c[1]), psl(c[1])
        prod = _mm_nt(jnp.concatenate([p["k"][:, sw], p["q"][:, sw]], axis=0).astype(BF16), bdiag(p["k"][:, sw]))
        a_mat[c] = jnp.where(strict, p["beta_p"][:, sp] * prod[0:L] * p["dm"][:, sp], 0.0)
        qkd[c] = prod[L:] * p["dm"][:, sp]
    yield
    t_inv = dict(zip(chains, (yield from _unit_lower_inverses([a_mat[c] for c in chains], eye, lm_p))))
    sol = {}
    for c in chains:
        p, sw = pre[c[0]], wsl(c[1])
        sol[c] = _mm(t_inv[c].astype(BF16), jnp.concatenate([bdiag(p["bv"][:, sw]), bdiag(p["bek"][:, sw])], axis=1))
    yield
    pairs = [(c, j) for c in chains for j in range(ppg)]
    s_p, r2, u_p = {}, {}, {}
    for c, j in pairs:
        pair = c[1] * ppg + j
        sg = slice(pair * GROUP, (pair + 1) * GROUP)
        s_p[c, j] = sb[c[0], pair]
        r2[c, j] = _mm(jnp.concatenate([sol[c][:, hw + j * GROUP:hw + (j + 1) * GROUP], pre[c[0]]["qd"][:, sg]],
                                       axis=0).astype(BF16), s_p[c, j].astype(BF16))
    yield
    for c, j in pairs:
        pair = c[1] * ppg + j
        sg = slice(pair * GROUP, (pair + 1) * GROUP)
        u_p[c, j] = sol[c][:, j * GROUP:(j + 1) * GROUP] - r2[c, j][0:L]
        upd = _mm_tn(pre[c[0]]["kd"][:, sg].astype(BF16), u_p[c, j].astype(BF16))
        sb[c[0], pair] = s_p[c, j] * pre[c[0]]["egl_w"][:, sg] + jnp.where(m_dd, upd, 0.0)
    yield
    o_g = {}
    for c in chains:
        u = jnp.concatenate([u_p[c, j] for j in range(ppg)], axis=1)
        qs = jnp.concatenate([r2[c, j][L:] for j in range(ppg)], axis=1)
        o_g[c] = qs + _mm(qkd[c].astype(BF16), bdiag(u))
    yield

    for n in range(nb):
        o = jnp.concatenate([o_g[(n, g)] for g in range(n_grp)], axis=1)
        sso, = _block_sums([o * o], m_dd_b)
        on = o * lax.rsqrt(sso * (1.0 / GDN_DV) + EPS) * on_ref[...]
        o_ref[n] = (on * _silu(z_ref[n])).astype(o_ref.dtype)
    yield MAIN_DONE

    @pl.when(i == nsteps - 1)
    def _():
        for n in range(nb):
            for h in range(GDN_HEADS):
                sfin_ref[n, h] = sb[n, h // hp, dsl(h), dsl(h)]


def _rwkv_body(r_ref, k_ref, v_ref, xw_ref, xa_ref, z_ref, pr_ref, pk_ref, pv_ref, pxw_ref, pxa_ref, s0_ref,
               mu_ref, muw_ref, mua_ref, w0_ref, wl_ref, a0_ref, al_ref, kk_ref, ka_ref, rk_ref,
               lnw_ref, lnb_ref,
               o_ref, sfin_ref,
               sbuf, lbuf, sbt, *, nb, chunk, nsteps):
    i = pl.program_id(1)
    L = chunk
    pw_w = PACK * L
    hd_bits = _bits(RWKV_HD)

    lm_p = _lane_masks(L, L)
    lm_k = _lane_masks(L, RWKV_HD)
    m_kk = _head_mask(GROUP, hd_bits, GROUP, hd_bits)
    m_kk_b = m_kk.astype(BF16)
    t_row = _iota((L, pw_w), 0)
    j_lane = _iota((L, pw_w), 1) & (L - 1)
    strict = j_lane < t_row
    incl = j_lane <= t_row
    eye = (j_lane == t_row).astype(F32)
    tri_b = (_iota((L, L), 0) >= _iota((L, L), 1)).astype(BF16)
    hsl = lambda h: slice((h % PACK) * RWKV_HD, (h % PACK + 1) * RWKV_HD)

    def bdiag(x):
        return _block_diag(x.astype(BF16), PACK, lm_k)

    @pl.when(i == 0)
    def _():
        sbt[...] = jnp.zeros(sbt.shape, F32)
        for n in range(nb):
            for h in range(RWKV_HEADS):
                sbt[n, h // PACK, hsl(h), hsl(h)] = s0_ref[n, h]
            sbuf[n, SUBLANE - 1:SUBLANE, 0:SEG] = pr_ref[n]
            sbuf[n, SUBLANE - 1:SUBLANE, SEG:2 * SEG] = pk_ref[n]
            sbuf[n, SUBLANE - 1:SUBLANE, 2 * SEG:3 * SEG] = pv_ref[n]
            lbuf[n, SUBLANE - 1:SUBLANE, 0:LANE] = pxw_ref[n]
            lbuf[n, SUBLANE - 1:SUBLANE, LANE:2 * LANE] = pxa_ref[n]

    yield
    inv_hd = 1.0 / RWKV_HD
    pre = []
    for n in range(nb):
        sbuf[n, SUBLANE:SUBLANE + L, 0:SEG] = r_ref[n]
        sbuf[n, SUBLANE:SUBLANE + L, SEG:2 * SEG] = k_ref[n]
        sbuf[n, SUBLANE:SUBLANE + L, 2 * SEG:3 * SEG] = v_ref[n]
        lbuf[n, SUBLANE:SUBLANE + L, 0:LANE] = xw_ref[n]
        lbuf[n, SUBLANE:SUBLANE + L, LANE:2 * LANE] = xa_ref[n]
        cur = sbuf[n, SUBLANE:SUBLANE + L, :]
        prv = sbuf[n, SUBLANE - 1:SUBLANE - 1 + L, :]
        rkv = cur + (prv - cur) * mu_ref[...]
        curl = lbuf[n, SUBLANE:SUBLANE + L, :]
        prvl = lbuf[n, SUBLANE - 1:SUBLANE - 1 + L, :]
        xw = curl[:, 0:LANE] + (prvl[:, 0:LANE] - curl[:, 0:LANE]) * muw_ref[...]
        xa = curl[:, LANE:] + (prvl[:, LANE:] - curl[:, LANE:]) * mua_ref[...]
        sbuf[n, 0:SUBLANE, :] = sbuf[n, L:L + SUBLANE, :]
        lbuf[n, 0:SUBLANE, :] = lbuf[n, L:L + SUBLANE, :]
        r = rkv[:, 0:SEG]
        kr = rkv[:, SEG:2 * SEG]
        vr = rkv[:, 2 * SEG:3 * SEG]
        wl = w0_ref[...] + _mm(jnp.tanh(xw).astype(BF16), wl_ref[...])
        logw = -math.exp(-W_DECAY_OFFSET) * _sigmoid(wl)
        a = _sigmoid(a0_ref[...] + _mm(xa.astype(BF16), al_ref[...]))
        kkr = kr * kk_ref[...]
        kt = kr * (1.0 + (a - 1.0) * ka_ref[...])
        ss, bonus = _block_sums([kkr * kkr, r * kt * rk_ref[...]], m_kk_b)
        kk = kkr * lax.rsqrt(ss + EPS)
        ah = a * kk
        cum = _cumsum_rows(logw, tri_b)
        c_last = cum[L - 1:L, :]
        e_neg = jnp.exp(-cum)
        e_last = jnp.exp(c_last - cum)
        kx = kk * jnp.exp(cum - logw)
        rx = r * jnp.exp(cum)
        kb = kt * e_neg
        ab = ah * e_neg
        kh = kt * e_last
        ahh = ah * e_last
        pre.append(dict(vr=vr, bonus=bonus, kx=kx, rx=rx, kb=kb, ab=ab, kh=kh, ahh=ahh, g_last=jnp.exp(c_last)))
        yield

    chains = [(n, g) for g in range(RWKV_GROUPS) for n in range(nb)]
    gsl = lambda g: slice(g * GROUP, (g + 1) * GROUP)
    lhs, c_mat, e_mat, rk_mat, ra_mat = {}, {}, {}, {}, {}
    for c in chains:
        p, sl = pre[c[0]], gsl(c[1])
        lhs[c] = jnp.concatenate([p["kx"][:, sl], p["rx"][:, sl]], axis=0).astype(BF16)
        rhs_nt = jnp.concatenate([bdiag(p["kb"][:, sl]), bdiag(p["ab"][:, sl])], axis=0)
        prod = _mm_nt(lhs[c], rhs_nt)
        c_mat[c] = jnp.where(strict, prod[0:L, 0:pw_w], 0.0)
        e_mat[c] = jnp.where(strict, prod[0:L, pw_w:], 0.0)
        rk_mat[c] = jnp.where(incl, prod[L:, 0:pw_w], 0.0)
        ra_mat[c] = jnp.where(incl, prod[L:, pw_w:], 0.0)
    yield
    t_inv = dict(zip(chains, (yield from _unit_lower_inverses([e_mat[c] for c in chains], eye, lm_p))))
    s_g, ks_rs, cv, u = {}, {}, {}, {}
    for c in chains:
        s_g[c] = sbt[c[0], c[1]]
        ks_rs[c] = _mm_nt(lhs[c], s_g[c].astype(BF16))
        cv[c] = _mm(jnp.concatenate([c_mat[c], rk_mat[c]], axis=0).astype(BF16),
                    bdiag(pre[c[0]]["vr"][:, gsl(c[1])]))
    yield
    for c in chains:
        u[c] = _mm(t_inv[c].astype(BF16), bdiag(ks_rs[c][0:L] + cv[c][0:L]))
    yield
    y_g = {}
    for c in chains:
        p, sl = pre[c[0]], gsl(c[1])
        y_g[c] = ks_rs[c][L:] + cv[c][L:] - _mm(ra_mat[c].astype(BF16), bdiag(u[c]))
        upd = _mm_tn(jnp.concatenate([p["vr"][:, sl], -u[c]], axis=0).astype(BF16),
                     jnp.concatenate([p["kh"][:, sl], p["ahh"][:, sl]], axis=0).astype(BF16))
        sbt[c[0], c[1]] = s_g[c] * p["g_last"][:, sl] + jnp.where(m_kk, upd, 0.0)
    yield

    for n in range(nb):
        y = jnp.concatenate([y_g[(n, g)] for g in range(RWKV_GROUPS)], axis=1)
        dlt = y - _block_sums([y], m_kk_b)[0] * inv_hd
        var = _block_sums([dlt * dlt], m_kk_b)[0] * inv_hd
        yn = dlt * lax.rsqrt(var + GN_EPS) * lnw_ref[...] + lnb_ref[...]
        o_ref[n] = ((yn + pre[n]["bonus"] * pre[n]["vr"]) * _silu(z_ref[n])).astype(o_ref.dtype)
    yield MAIN_DONE

    @pl.when(i == nsteps - 1)
    def _():
        for n in range(nb):
            for h in range(RWKV_HEADS):
                sfin_ref[n, h] = sbt[n, h // PACK, hsl(h), hsl(h)]


N_GDN_IN, N_RWKV_IN = 11, 24
MAIN_DONE = "main stages done"


def _scan_kernel(*refs, nb, chunk, nsteps):
    g_in = refs[0:N_GDN_IN]
    r_in = refs[N_GDN_IN:N_GDN_IN + N_RWKV_IN]
    og_ref, sg_ref, or_ref, sr_ref, ubuf, sb, sbuf, lbuf, sbt = refs[N_GDN_IN + N_RWKV_IN:]
    kw = dict(nb=nb, chunk=chunk, nsteps=nsteps)
    bodies = [_gdn_body(*g_in, og_ref, sg_ref, ubuf, sb, **kw), _rwkv_body(*r_in, or_ref, sr_ref, sbuf, lbuf, sbt, **kw)]
    for body in bodies:
        next(body)
    active = list(bodies)
    while active:
        for body in list(active):
            if next(body) is MAIN_DONE:
                active.remove(body)
    for body in bodies:
        for _ in body:
            pass


def _scans(proj3, conv_prev, s_gdn, shift_prev, s_rwkv, p, nb):
    b, t, _ = proj3.shape
    L = min(GDN_CHUNK, t)
    assert b % nb == 0 and t % L == 0 and L % SUBLANE == 0
    nsteps = t // L
    blk = lambda c, w: pl.BlockSpec((nb, L, w), lambda g, i, c=c: (g, i, c))
    const = lambda shape: pl.BlockSpec(shape, lambda g, i: (0,) * len(shape))
    per_seq = lambda *shape: pl.BlockSpec((nb,) + shape, lambda g, i: (g,) + (0,) * len(shape))
    prev_pad = jnp.pad(conv_prev, ((0, 0), (SUBLANE - (CONV_W - 1), 0), (0, 0)))
    lane_pad = (GDN_HEADS, LANE - 2 * GDN_HEADS)
    alog_row = jnp.pad(p["a_log"], lane_pad).reshape(1, LANE)
    dtb_row = jnp.pad(p["dt_bias"], lane_pad).reshape(1, LANE)
    on_row = jnp.tile(p["onorm_g"], GDN_HEADS).reshape(1, GDN_W)
    gdn_state = per_seq(GDN_HEADS, GDN_DK, GDN_DV)
    gdn_specs = [
        blk(COL_Q, SEG), blk(COL_K, SEG), blk(COL_V, SEG), blk(COL_ZG, SEG), blk(COL_BA // LANE, LANE),
        const((CONV_W, 3 * GDN_W)), per_seq(SUBLANE, 3 * GDN_W), gdn_state,
        const((1, LANE)), const((1, LANE)), const((1, GDN_W)),
    ]
    gdn_args = [proj3] * 5 + [p["conv_w"], prev_pad, s_gdn, alog_row, dtb_row, on_row]
    w3 = 3 * RWKV_W
    padl = lambda x, n: jnp.pad(x, [(0, 0)] * (x.ndim - 1) + [(0, LANE - n)])
    p_r, p_k, p_v = shift_prev[..., 0:RWKV_W], shift_prev[..., RWKV_W:2 * RWKV_W], shift_prev[..., 2 * RWKV_W:w3]
    p_xw = padl(shift_prev[..., w3:w3 + W_LORA], W_LORA)
    p_xa = padl(shift_prev[..., w3 + W_LORA:], A_LORA)
    mu = p["mu"]
    mu_rkv = mu[0:w3].reshape(1, w3)
    mu_w = padl(mu[w3:w3 + W_LORA], W_LORA).reshape(1, LANE)
    mu_a = padl(mu[w3 + W_LORA:], A_LORA).reshape(1, LANE)
    wl_pad = jnp.pad(p["w_lora"], ((0, LANE - W_LORA), (0, 0))).astype(BF16)
    al_pad = jnp.pad(p["a_lora"], ((0, LANE - A_LORA), (0, 0))).astype(BF16)
    s0_t = s_rwkv.transpose(0, 1, 3, 2)
    row1 = lambda x: x.reshape(1, RWKV_W)
    rwkv_state = per_seq(RWKV_HEADS, RWKV_HD, RWKV_HD)
    rwkv_specs = [
        blk(COL_R, SEG), blk(COL_RK, SEG), blk(COL_RV, SEG),
        blk(COL_XW // LANE, LANE), blk(COL_XA // LANE, LANE), blk(COL_ZR, SEG),
        per_seq(1, SEG), per_seq(1, SEG), per_seq(1, SEG), per_seq(1, LANE), per_seq(1, LANE),
        rwkv_state,
        const((1, w3)), const((1, LANE)), const((1, LANE)),
        const((1, RWKV_W)), const((LANE, RWKV_W)), const((1, RWKV_W)), const((LANE, RWKV_W)),
        const((1, RWKV_W)), const((1, RWKV_W)), const((1, RWKV_W)), const((1, RWKV_W)), const((1, RWKV_W)),
    ]
    rwkv_args = [proj3] * 6 + [p_r, p_k, p_v, p_xw, p_xa, s0_t, mu_rkv, mu_w, mu_a, row1(p["w0"]), wl_pad,
                               row1(p["a0"]), al_pad, row1(p["k_k"]), row1(p["k_a"]), row1(p["r_k"]),
                               row1(p["ln_w"]), row1(p["ln_b"])]
    assert len(gdn_specs) == N_GDN_IN and len(rwkv_specs) == N_RWKV_IN
    o_g, sg, o_r, sr = pl.pallas_call(
        functools.partial(_scan_kernel, nb=nb, chunk=L, nsteps=nsteps),
        grid=(b // nb, nsteps),
        in_specs=gdn_specs + rwkv_specs,
        out_specs=[
            pl.BlockSpec((nb, L, GDN_W), lambda g, i: (g, i, 0)), gdn_state,
            pl.BlockSpec((nb, L, RWKV_W), lambda g, i: (g, i, 0)), rwkv_state,
        ],
        out_shape=[
            jax.ShapeDtypeStruct((b, t, GDN_W), BF16),
            jax.ShapeDtypeStruct((b, GDN_HEADS, GDN_DK, GDN_DV), F32),
            jax.ShapeDtypeStruct((b, t, RWKV_W), BF16),
            jax.ShapeDtypeStruct((b, RWKV_HEADS, RWKV_HD, RWKV_HD), F32),
        ],
        scratch_shapes=[
            pltpu.VMEM((nb, L + SUBLANE, 3 * GDN_W), F32),
            pltpu.VMEM((nb, GDN_PAIRS, GROUP, GROUP), F32),
            pltpu.VMEM((nb, L + SUBLANE, w3), F32),
            pltpu.VMEM((nb, L + SUBLANE, 2 * LANE), F32),
            pltpu.VMEM((nb, RWKV_GROUPS, GROUP, GROUP), F32),
        ],
        compiler_params=pltpu.CompilerParams(
            dimension_semantics=("arbitrary", "arbitrary"), vmem_limit_bytes=VMEM_LIMIT),
        name="scan",
    )(*gdn_args, *rwkv_args)
    return o_g.reshape(b * t, GDN_W), sg, o_r.reshape(b * t, RWKV_W), sr.transpose(0, 1, 3, 2)


def _merge_kernel(og_ref, or_ref, gg_ref, gr_ref, x_ref, gate_ref, wog_ref, wor_ref, wout_ref, fg_ref, o_ref,
                  *, final_norm):
    m = _sigmoid(gg_ref[...]) * _mm(og_ref[...], wog_ref[...]) \
        + _sigmoid(gr_ref[...]) * _mm(or_ref[...], wor_ref[...])
    out = _mm(m.astype(BF16), wout_ref[...])
    xn = x_ref[...] + gate_ref[...] * out
    if final_norm:
        xn = xn * lax.rsqrt(jnp.mean(xn * xn, axis=-1, keepdims=True) + EPS) * fg_ref[...]
    o_ref[...] = xn


def _merge(o_g, o_r, proj, x2d, gate, w_og, w_or, w_out, final_g, rows_per_mod, final_norm):
    m, d = x2d.shape
    tm = min(m, 256) if rows_per_mod == 1 else min(rows_per_mod, 256)
    if rows_per_mod == 1:
        gate_spec = pl.BlockSpec((tm, d), lambda i: (i, 0))
    else:
        assert rows_per_mod % tm == 0
        per = rows_per_mod // tm
        gate = gate.reshape(-1, 1, d)
        gate_spec = pl.BlockSpec((None, 1, d), lambda i: (i // per, 0, 0))
    whole = lambda shape: pl.BlockSpec(shape, lambda i: (0, 0))
    gcol = COL_GATES // d
    return pl.pallas_call(
        functools.partial(_merge_kernel, final_norm=final_norm),
        grid=(m // tm,),
        in_specs=[
            pl.BlockSpec((tm, GDN_W), lambda i: (i, 0)),
            pl.BlockSpec((tm, RWKV_W), lambda i: (i, 0)),
            pl.BlockSpec((tm, d), lambda i: (i, gcol)),
            pl.BlockSpec((tm, d), lambda i: (i, gcol + 1)),
            pl.BlockSpec((tm, d), lambda i: (i, 0)),
            gate_spec,
            whole((GDN_W, d)), whole((RWKV_W, d)), whole((d, d)), whole((1, d)),
        ],
        out_specs=pl.BlockSpec((tm, d), lambda i: (i, 0)),
        out_shape=jax.ShapeDtypeStruct((m, d), F32),
        compiler_params=pltpu.CompilerParams(
            dimension_semantics=("arbitrary",), vmem_limit_bytes=VMEM_LIMIT),
        name="merge_out",
    )(o_g, o_r, proj, proj, x2d, gate, w_og, w_or, w_out, final_g.reshape(1, d))


def _pad_in_weight(w_in):
    d = w_in.shape[0]
    o_zg = 3 * GDN_W
    o_b = o_zg + GDN_W
    o_rw = o_b + 2 * GDN_HEADS
    o_xw = o_rw + 3 * RWKV_W
    o_xa = o_xw + W_LORA
    o_zr = o_xa + A_LORA
    o_br = o_zr + RWKV_W
    z = lambda n: jnp.zeros((d, n), w_in.dtype)
    cols = [
        w_in[:, 0:o_zg], w_in[:, o_zg:o_b],
        w_in[:, o_rw:o_xw], w_in[:, o_zr:o_br],
        w_in[:, o_br:o_br + 2 * D_MODEL],
        w_in[:, o_b:o_rw], z(LANE - 2 * GDN_HEADS),
        w_in[:, o_xw:o_xa], z(LANE - W_LORA),
        w_in[:, o_xa:o_zr], z(LANE - A_LORA),
    ]
    used = COL_XA + LANE
    cols.append(z(N_PROJ - used))
    return jnp.concatenate(cols, axis=1).astype(BF16)


def _layer(x2d, batch, seq, scale, shift, gate, rows_per_mod, conv_prev, s_gdn, shift_prev, s_rwkv, p, final_g,
           final_norm, scan_nb):
    proj = _in_proj(x2d, scale, shift, p["norm_g"], p["w_in_pad"], rows_per_mod)
    proj3 = proj.reshape(batch, seq, N_PROJ)
    o_g, s_gdn_new, o_r, s_rwkv_new = _scans(proj3, conv_prev, s_gdn, shift_prev, s_rwkv, p, scan_nb)
    x_new = _merge(o_g, o_r, proj, x2d, gate, p["w_o_gdn"], p["w_o_rwkv"], p["w_out"], final_g, rows_per_mod,
                   final_norm)
    conv_new = proj3[:, seq - (CONV_W - 1):, 0:3 * GDN_W]
    last = proj3[:, seq - 1:, :]
    shift_new = jnp.concatenate(
        [last[..., COL_R * SEG:COL_R * SEG + 3 * RWKV_W], last[..., COL_XW:COL_XW + W_LORA],
         last[..., COL_XA:COL_XA + A_LORA]], axis=-1)
    return x_new, conv_new, s_gdn_new, shift_new, s_rwkv_new


def _forward(x_prompt, x_sample, c_prompt, c_sample, cache_gdn_conv, state_gdn, cache_rwkv_shift, state_rwkv,
             ada_w, ada_b, norm_g, w_in, gdn_conv_w, gdn_a_log, gdn_dt_bias, gdn_out_norm_g,
             rwkv_mu, rwkv_w0, rwkv_w_lora, rwkv_a0, rwkv_a_lora, rwkv_k_k, rwkv_k_a, rwkv_r_k,
             rwkv_ln_w, rwkv_ln_b, w_o_gdn, w_o_rwkv, w_out, final_norm_g):
    depth = ada_w.shape[0]
    bp, tp, d = x_prompt.shape
    bs, ts, _ = x_sample.shape
    assert CONV_W - 1 <= min(tp, ts)
    c_all = jnp.concatenate([c_prompt, c_sample], axis=0)
    rows = -(-(bp + bs) // SUBLANE) * SUBLANE
    c_all = jnp.pad(c_all, ((0, rows - (bp + bs)), (0, 0)))
    mod = _ada_mod(c_all, ada_w, ada_b)
    xp = x_prompt.reshape(bp * tp, d)
    xs = x_sample.reshape(bs * ts, d)
    outs = [[] for _ in range(8)]
    for l in range(depth):
        p = dict(norm_g=norm_g[l], w_in_pad=_pad_in_weight(w_in[l]), conv_w=gdn_conv_w[l], a_log=gdn_a_log[l],
                 dt_bias=gdn_dt_bias[l], onorm_g=gdn_out_norm_g[l], mu=rwkv_mu[l], w0=rwkv_w0[l],
                 w_lora=rwkv_w_lora[l], a0=rwkv_a0[l], a_lora=rwkv_a_lora[l], k_k=rwkv_k_k[l], k_a=rwkv_k_a[l],
                 r_k=rwkv_r_k[l].reshape(-1), ln_w=rwkv_ln_w[l], ln_b=rwkv_ln_b[l],
                 w_o_gdn=w_o_gdn[l].astype(BF16), w_o_rwkv=w_o_rwkv[l].astype(BF16), w_out=w_out[l].astype(BF16))
        last = l == depth - 1
        m_p = mod[l, 0:bp]
        m_s = jnp.repeat(mod[l, bp:bp + bs], ts, axis=0)
        sh_p, sc_p, gt_p = m_p[:, 0:d], m_p[:, d:2 * d], m_p[:, 2 * d:]
        sh_s, sc_s, gt_s = m_s[:, 0:d], m_s[:, d:2 * d], m_s[:, 2 * d:]
        zeros = lambda *s: jnp.zeros(s, F32)
        xp, c1, g1, h1, r1 = _layer(
            xp, bp, tp, sc_p, sh_p, gt_p, tp,
            zeros(bp, CONV_W - 1, 3 * GDN_W), zeros(bp, GDN_HEADS, GDN_DK, GDN_DV),
            zeros(bp, 1, 3 * RWKV_W + W_LORA + A_LORA), zeros(bp, RWKV_HEADS, RWKV_HD, RWKV_HD),
            p, final_norm_g, last, scan_nb=bp)
        xs, c2, g2, h2, r2 = _layer(
            xs, bs, ts, sc_s, sh_s, gt_s, 1,
            cache_gdn_conv[l], state_gdn[l], cache_rwkv_shift[l], state_rwkv[l],
            p, final_norm_g, last, scan_nb=min(bs, 2))
        for lst, val in zip(outs, (c1, g1, h1, r1, c2, g2, h2, r2)):
            lst.append(val)
    stk = [jnp.stack(v) for v in outs]
    return (xp.reshape(bp, tp, d), xs.reshape(bs, ts, d), *stk)


def kernel(x_prompt, x_sample, c_prompt, c_sample, cache_gdn_conv, state_gdn, cache_rwkv_shift, state_rwkv, ada_w, ada_b, norm_g, w_in, gdn_conv_w, gdn_a_log, gdn_dt_bias, gdn_out_norm_g, rwkv_mu, rwkv_w0, rwkv_w_lora, rwkv_a0, rwkv_a_lora, rwkv_k_k, rwkv_k_a, rwkv_r_k, rwkv_ln_w, rwkv_ln_b, w_o_gdn, w_o_rwkv, w_out, final_norm_g):
    return _forward(x_prompt, x_sample, c_prompt, c_sample, cache_gdn_conv, state_gdn, cache_rwkv_shift,
                    state_rwkv, ada_w, ada_b, norm_g, w_in, gdn_conv_w, gdn_a_log, gdn_dt_bias, gdn_out_norm_g,
                    rwkv_mu, rwkv_w0, rwkv_w_lora, rwkv_a0, rwkv_a_lora, rwkv_k_k, rwkv_k_a, rwkv_r_k,
                    rwkv_ln_w, rwkv_ln_b, w_o_gdn, w_o_rwkv, w_out, final_norm_g)
```

```python
import functools
import math

import jax
import jax.numpy as jnp
from jax import lax
from jax.experimental import pallas as pl
from jax.experimental.pallas import tpu as pltpu

F32 = jnp.float32
BF16 = jnp.bfloat16

D_MODEL = 2048
GDN_HEADS = 8
GDN_DK = 128
GDN_DV = 128
GDN_W = GDN_HEADS * GDN_DK
GDN_CHUNK = 64
CONV_W = 4
RWKV_HEADS = 16
RWKV_HD = 64
RWKV_W = RWKV_HEADS * RWKV_HD
W_LORA = 96
A_LORA = 96
EPS = 1e-6
GN_EPS = 64e-5
W_DECAY_OFFSET = 0.5

LANE = 128
SUBLANE = 8
SEG = 1024
COL_Q, COL_K, COL_V, COL_ZG, COL_R, COL_RK, COL_RV, COL_ZR = 0, 1, 2, 3, 4, 5, 6, 7
COL_GATES = 8 * SEG
COL_BA = 12 * SEG
COL_XW = COL_BA + LANE
COL_XA = COL_XW + LANE
GROUP = 256
PROJ_TILE = 5 * GROUP
N_PROJ = 10 * PROJ_TILE
PACK = 4
RWKV_GROUPS = RWKV_W // GROUP
GDN_PAIRS = GDN_W // GROUP
VMEM_LIMIT = 56 * 1024 * 1024


def _mm(a, b):
    return jnp.dot(a, b, preferred_element_type=F32)


def _mm_nt(a, b):
    return lax.dot_general(a, b, (((1,), (1,)), ((), ())), preferred_element_type=F32)


def _mm_tn(a, b):
    return lax.dot_general(a, b, (((0,), (0,)), ((), ())), preferred_element_type=F32)


def _split_bf16(x):
    hi = x.astype(BF16)
    return hi, (x - hi.astype(F32)).astype(BF16)


def _split3_bf16(x):
    hi = x.astype(BF16)
    mid, lo = _split_bf16(x - hi.astype(F32))
    return hi, mid, lo


def _sigmoid(x):
    return 1.0 / (1.0 + jnp.exp(-x))


def _silu(x):
    return x * _sigmoid(x)


def _softplus(x):
    return jnp.maximum(x, 0.0) + jnp.log1p(jnp.exp(-jnp.abs(x)))


def _iota(shape, dim):
    return lax.broadcasted_iota(jnp.int32, shape, dim)


def _bits(n):
    assert n & (n - 1) == 0
    return n.bit_length() - 1


def _mod_kernel(c_ref, w_ref, b_ref, o_ref):
    o_ref[...] = _mm(c_ref[...], w_ref[...]) + b_ref[...]


def _ada_mod(c_all, ada_w, ada_b):
    depth, d, n3 = ada_w.shape
    rows = c_all.shape[0]
    tn = 512
    return pl.pallas_call(
        _mod_kernel,
        grid=(depth, n3 // tn),
        in_specs=[
            pl.BlockSpec((rows, d), lambda l, j: (0, 0)),
            pl.BlockSpec((None, d, tn), lambda l, j: (l, 0, j)),
            pl.BlockSpec((None, 1, tn), lambda l, j: (l, 0, j)),
        ],
        out_specs=pl.BlockSpec((None, rows, tn), lambda l, j: (l, 0, j)),
        out_shape=jax.ShapeDtypeStruct((depth, rows, n3), F32),
        compiler_params=pltpu.CompilerParams(
            dimension_semantics=("arbitrary", "arbitrary"), vmem_limit_bytes=VMEM_LIMIT),
        name="ada_mod",
    )(c_all, ada_w, ada_b.reshape(depth, 1, n3))


def _in_proj_kernel(x_ref, sc_ref, sh_ref, g_ref, w_ref, o_ref, h_ref):
    @pl.when(pl.program_id(1) == 0)
    def _():
        x = x_ref[...]
        ms = jnp.mean(x * x, axis=-1, keepdims=True)
        h = x * lax.rsqrt(ms + EPS) * g_ref[...] * (1.0 + sc_ref[...]) + sh_ref[...]
        h_ref[...] = h.astype(BF16)

    o_ref[...] = _mm(h_ref[...], w_ref[...])


def _in_proj(x2d, scale, shift, norm_g, w_pad, rows_per_mod):
    m, d = x2d.shape
    tm = min(m, 512) if rows_per_mod == 1 else min(rows_per_mod, 1024)
    tn = PROJ_TILE
    if rows_per_mod == 1:
        mod_spec = pl.BlockSpec((tm, d), lambda i, j: (i, 0))
    else:
        assert rows_per_mod % tm == 0
        per = rows_per_mod // tm
        scale = scale.reshape(-1, 1, d)
        shift = shift.reshape(-1, 1, d)
        mod_spec = pl.BlockSpec((None, 1, d), lambda i, j: (i // per, 0, 0))
    return pl.pallas_call(
        _in_proj_kernel,
        grid=(m // tm, N_PROJ // tn),
        in_specs=[
            pl.BlockSpec((tm, d), lambda i, j: (i, 0)),
            mod_spec, mod_spec,
            pl.BlockSpec((1, d), lambda i, j: (0, 0)),
            pl.BlockSpec((d, tn), lambda i, j: (0, j)),
        ],
        out_specs=pl.BlockSpec((tm, tn), lambda i, j: (i, j)),
        out_shape=jax.ShapeDtypeStruct((m, N_PROJ), F32),
        scratch_shapes=[pltpu.VMEM((tm, d), BF16)],
        compiler_params=pltpu.CompilerParams(
            dimension_semantics=("arbitrary", "arbitrary"), vmem_limit_bytes=VMEM_LIMIT),
        name="in_proj",
    )(x2d, scale, shift, norm_g.reshape(1, d), w_pad)


def _round_robin(gens):
    gens = list(gens)
    while gens:
        for gen in list(gens):
            try:
                next(gen)
            except StopIteration:
                gens.remove(gen)
        yield


def _stagger(prepare, algebra, n_seq):
    yield from prepare(0)
    for n in range(1, n_seq):
        yield from _round_robin([algebra([n - 1]), prepare(n)])
    yield from algebra([n_seq - 1])


def _lane_masks(rows, head_lanes):
    if head_lanes >= LANE:
        return None
    lane = _iota((rows, LANE), 1)
    return [((lane >= o) & (lane < o + head_lanes)).astype(BF16) for o in range(0, LANE, head_lanes)]


def _block_diag(x, n_heads, lane_masks):
    L, w = x.shape
    hl = w // n_heads
    span = max(hl, LANE)
    rows = []
    for h in range(n_heads):
        c0 = (h * hl) // span * span
        piece = x[:, c0:c0 + span]
        if hl < LANE:
            piece = piece * lane_masks[(h * hl - c0) // hl]
        parts = [piece]
        if c0:
            parts.insert(0, jnp.zeros((L, c0), x.dtype))
        if w - c0 - span:
            parts.append(jnp.zeros((L, w - c0 - span), x.dtype))
        rows.append(jnp.concatenate(parts, axis=1) if len(parts) > 1 else piece)
    return jnp.concatenate(rows, axis=0)


def _head_mask(rows, row_bits, cols, col_bits):
    return (_iota((rows, cols), 0) >> row_bits) == (_iota((rows, cols), 1) >> col_bits)


def _block_sums(xs, gmat_b):
    t, width = xs[0].shape
    ng = width // GROUP
    parts = []
    for x in xs:
        xb = x.astype(BF16)
        parts += [xb[:, g * GROUP:(g + 1) * GROUP] for g in range(ng)]
    res = _mm(jnp.concatenate(parts, axis=0), gmat_b)
    return [jnp.concatenate([res[(i * ng + g) * t:(i * ng + g + 1) * t] for g in range(ng)], axis=1)
            for i in range(len(xs))]


def _cumsum_rows(x, tri_b):
    w = x.shape[1]
    res = _mm(tri_b, jnp.concatenate(_split3_bf16(x), axis=1))
    return res[:, 0:w] + (res[:, w:2 * w] + res[:, 2 * w:])


def _unit_lower_inverses(e_mats, eye, lane_masks):
    L, pw_w = e_mats[0].shape
    steps = _bits(L)
    pws = [-e for e in e_mats]
    t_invs = [eye + pw for pw in pws]
    for s in range(steps):
        first, last = s == 0, s == steps - 1
        if first and last:
            break
        for c in range(len(e_mats)):
            p_hi, p_lo = _split_bf16(pws[c])
            bd_hi = _block_diag(p_hi, PACK, lane_masks)
            bd_lo = _block_diag(p_lo, PACK, lane_masks)
            rows = ([] if first else [_split_bf16(t_invs[c])]) + ([] if last else [(p_hi, p_lo)])
            x_hi = jnp.concatenate([hi for hi, _ in rows], axis=0)
            x_lo = jnp.concatenate([lo for _, lo in rows], axis=0)
            nr = x_hi.shape[0]
            hh = _mm(jnp.concatenate([x_hi, x_lo], axis=0), bd_hi)
            res = hh[0:nr] + (hh[nr:] + _mm(x_hi, bd_lo))
            if not first:
                t_invs[c] = t_invs[c] + res[0:L]
            if not last:
                pws[c] = res[nr - L:]
        yield
    return t_invs


def _gdn_body(q_ref, k_ref, v_ref, z_ref, ba_ref, cw_ref, prev_ref, s0_ref, alog_ref, dtb_ref, on_ref,
              o_ref, sfin_ref, ubuf, sb, *, nb, chunk, nsteps):
    L = chunk
    i = pl.program_id(1)
    l_bits = _bits(L)
    dk_bits = _bits(GDN_DK)
    pw_w = PACK * L
    hw = PACK * GDN_DK
    n_grp = GDN_HEADS // PACK
    hp = GROUP // GDN_DK
    dsl = lambda h: slice((h % hp) * GDN_DK, (h % hp + 1) * GDN_DK)

    lm_p = _lane_masks(L, L)
    m_dd = _head_mask(GROUP, dk_bits, GROUP, dk_bits)
    m_dd_b = m_dd.astype(BF16)
    t_row = _iota((L, pw_w), 0)
    j_lane = _iota((L, pw_w), 1) & (L - 1)
    strict = j_lane < t_row
    incl = j_lane <= t_row
    eye = (j_lane == t_row).astype(F32)
    tri_b = (_iota((L, L), 0) >= _iota((L, L), 1)).astype(BF16)
    ones_b = jnp.ones((L, L), BF16)
    lane = _iota((L, LANE), 1)
    x_rows = _iota((LANE, GDN_HEADS * L + GDN_W), 0) & (GDN_HEADS - 1)
    x_cols = _iota((LANE, GDN_HEADS * L + GDN_W), 1)
    x_head = jnp.where(x_cols < GDN_HEADS * L, x_cols >> l_bits, (x_cols - GDN_HEADS * L) >> dk_bits)
    expand_b = ((x_rows == x_head) & (_iota((LANE, GDN_HEADS * L + GDN_W), 0) < 2 * GDN_HEADS)).astype(BF16)

    def bdiag(x):
        return _block_diag(x.astype(BF16), PACK, None)

    @pl.when(i == 0)
    def _():
        sb[...] = jnp.zeros(sb.shape, F32)
        for n in range(nb):
            ubuf[n, 0:SUBLANE, :] = prev_ref[n]
            for h in range(GDN_HEADS):
                sb[n, h // hp, dsl(h), dsl(h)] = s0_ref[n, h]

    yield
    base = SUBLANE - (CONV_W - 1)
    pre = {}

    def prepare(n):
        ubuf[n, SUBLANE:SUBLANE + L, 0:GDN_W] = q_ref[n]
        ubuf[n, SUBLANE:SUBLANE + L, GDN_W:2 * GDN_W] = k_ref[n]
        ubuf[n, SUBLANE:SUBLANE + L, 2 * GDN_W:3 * GDN_W] = v_ref[n]
        y = ubuf[n, base:base + L, :] * cw_ref[0:1, :]
        for j in range(1, CONV_W):
            y = y + ubuf[n, base + j:base + j + L, :] * cw_ref[j:j + 1, :]
        ubuf[n, 0:SUBLANE, :] = ubuf[n, L:L + SUBLANE, :]
        act = _silu(y)
        q = act[:, 0:GDN_W]
        k = act[:, GDN_W:2 * GDN_W]
        v = act[:, 2 * GDN_W:]
        ssq, ssk = _block_sums([q * q, k * k], m_dd_b)
        q = q * lax.rsqrt(ssq + EPS) * (GDN_DK ** -0.5)
        k = k * lax.rsqrt(ssk + EPS)

        ba = ba_ref[n]
        beta_c = jnp.where(lane < GDN_HEADS, _sigmoid(ba), 0.0)
        g_c = -jnp.exp(alog_ref[...]) * _softplus(ba + dtb_ref[...])
        gc_c = jnp.where((lane >= GDN_HEADS) & (lane < 2 * GDN_HEADS), _cumsum_rows(g_c, tri_b), 0.0)
        pieces = jnp.concatenate(list(_split_bf16(beta_c)) + list(_split3_bf16(gc_c)), axis=0)
        ex = _mm(pieces, expand_b)
        beta_x = ex[0:L] + ex[L:2 * L]
        gc_x = ex[2 * L:3 * L] + (ex[3 * L:4 * L] + ex[4 * L:])
        beta_p, beta_w = beta_x[:, 0:GDN_HEADS * L], beta_x[:, GDN_HEADS * L:]
        gc_p, gc_w = gc_x[:, 0:GDN_HEADS * L], gc_x[:, GDN_HEADS * L:]
        eye8 = jnp.concatenate([eye] * n_grp, axis=1)
        incl8 = jnp.concatenate([incl] * n_grp, axis=1)
        rw = _mm(ones_b, jnp.concatenate(_split3_bf16(gc_p * eye8), axis=1))
        pw8 = GDN_HEADS * L
        gc_row = rw[:, 0:pw8] + (rw[:, pw8:2 * pw8] + rw[:, 2 * pw8:])
        dm = jnp.where(incl8, jnp.exp(jnp.where(incl8, gc_p - gc_row, 0.0)), 0.0)
        gl_w = gc_w[L - 1:L, :]
        eg_w = jnp.exp(gc_w)
        qd = q * eg_w
        kd = k * jnp.exp(gl_w - gc_w)
        bv = beta_w * v
        bek = beta_w * eg_w * k
        pre[n] = dict(q=q, k=k, beta_p=beta_p, dm=dm, qd=qd, kd=kd, bv=bv, bek=bek, egl_w=jnp.exp(gl_w))
        yield

    def algebra(seqs):
        chains = [(n, g) for g in range(n_grp) for n in seqs]
        wsl = lambda g: slice(g * hw, (g + 1) * hw)
        psl = lambda g: slice(g * pw_w, (g + 1) * pw_w)
        ppg = hw // GROUP
        a_mat, qkd = {}, {}
        for c in chains:
            p, sw, sp = pre[c[0]], wsl(c[1]), psl(c[1])
            prod = _mm_nt(jnp.concatenate([p["k"][:, sw], p["q"][:, sw]], axis=0).astype(BF16), bdiag(p["k"][:, sw]))
            a_mat[c] = jnp.where(strict, p["beta_p"][:, sp] * prod[0:L] * p["dm"][:, sp], 0.0)
            qkd[c] = prod[L:] * p["dm"][:, sp]
        yield
        t_inv = dict(zip(chains, (yield from _unit_lower_inverses([a_mat[c] for c in chains], eye, lm_p))))
        sol = {}
        for c in chains:
            p, sw = pre[c[0]], wsl(c[1])
            sol[c] = _mm(t_inv[c].astype(BF16), jnp.concatenate([bdiag(p["bv"][:, sw]), bdiag(p["bek"][:, sw])], axis=1))
        yield
        pairs = [(c, j) for c in chains for j in range(ppg)]
        s_p, r2, u_p = {}, {}, {}
        for c, j in pairs:
            pair = c[1] * ppg + j
            sg = slice(pair * GROUP, (pair + 1) * GROUP)
            s_p[c, j] = sb[c[0], pair]
            r2[c, j] = _mm(jnp.concatenate([sol[c][:, hw + j * GROUP:hw + (j + 1) * GROUP], pre[c[0]]["qd"][:, sg]],
                                           axis=0).astype(BF16), s_p[c, j].astype(BF16))
        yield
        for c, j in pairs:
            pair = c[1] * ppg + j
            sg = slice(pair * GROUP, (pair + 1) * GROUP)
            u_p[c, j] = sol[c][:, j * GROUP:(j + 1) * GROUP] - r2[c, j][0:L]
            upd = _mm_tn(pre[c[0]]["kd"][:, sg].astype(BF16), u_p[c, j].astype(BF16))
            sb[c[0], pair] = s_p[c, j] * pre[c[0]]["egl_w"][:, sg] + jnp.where(m_dd, upd, 0.0)
        yield
        o_g = {}
        for c in chains:
            u = jnp.concatenate([u_p[c, j] for j in range(ppg)], axis=1)
            qs = jnp.concatenate([r2[c, j][L:] for j in range(ppg)], axis=1)
            o_g[c] = qs + _mm(qkd[c].astype(BF16), bdiag(u))
        yield

        for n in seqs:
            o = jnp.concatenate([o_g[(n, g)] for g in range(n_grp)], axis=1)
            sso, = _block_sums([o * o], m_dd_b)
            on = o * lax.rsqrt(sso * (1.0 / GDN_DV) + EPS) * on_ref[...]
            o_ref[n] = (on * _silu(z_ref[n])).astype(o_ref.dtype)

    yield from _stagger(prepare, algebra, nb)
    yield MAIN_DONE

    @pl.when(i == nsteps - 1)
    def _():
        for n in range(nb):
            for h in range(GDN_HEADS):
                sfin_ref[n, h] = sb[n, h // hp, dsl(h), dsl(h)]


def _rwkv_body(r_ref, k_ref, v_ref, xw_ref, xa_ref, z_ref, pr_ref, pk_ref, pv_ref, pxw_ref, pxa_ref, s0_ref,
               mu_ref, muw_ref, mua_ref, w0_ref, wl_ref, a0_ref, al_ref, kk_ref, ka_ref, rk_ref,
               lnw_ref, lnb_ref,
               o_ref, sfin_ref,
               sbuf, lbuf, sbt, *, nb, chunk, nsteps):
    i = pl.program_id(1)
    L = chunk
    pw_w = PACK * L
    hd_bits = _bits(RWKV_HD)

    lm_p = _lane_masks(L, L)
    lm_k = _lane_masks(L, RWKV_HD)
    m_kk = _head_mask(GROUP, hd_bits, GROUP, hd_bits)
    m_kk_b = m_kk.astype(BF16)
    t_row = _iota((L, pw_w), 0)
    j_lane = _iota((L, pw_w), 1) & (L - 1)
    strict = j_lane < t_row
    incl = j_lane <= t_row
    eye = (j_lane == t_row).astype(F32)
    tri_b = (_iota((L, L), 0) >= _iota((L, L), 1)).astype(BF16)
    hsl = lambda h: slice((h % PACK) * RWKV_HD, (h % PACK + 1) * RWKV_HD)

    def bdiag(x):
        return _block_diag(x.astype(BF16), PACK, lm_k)

    @pl.when(i == 0)
    def _():
        sbt[...] = jnp.zeros(sbt.shape, F32)
        for n in range(nb):
            for h in range(RWKV_HEADS):
                sbt[n, h // PACK, hsl(h), hsl(h)] = s0_ref[n, h]
            sbuf[n, SUBLANE - 1:SUBLANE, 0:SEG] = pr_ref[n]
            sbuf[n, SUBLANE - 1:SUBLANE, SEG:2 * SEG] = pk_ref[n]
            sbuf[n, SUBLANE - 1:SUBLANE, 2 * SEG:3 * SEG] = pv_ref[n]
            lbuf[n, SUBLANE - 1:SUBLANE, 0:LANE] = pxw_ref[n]
            lbuf[n, SUBLANE - 1:SUBLANE, LANE:2 * LANE] = pxa_ref[n]

    yield
    inv_hd = 1.0 / RWKV_HD
    pre = {}

    def prepare(n):
        sbuf[n, SUBLANE:SUBLANE + L, 0:SEG] = r_ref[n]
        sbuf[n, SUBLANE:SUBLANE + L, SEG:2 * SEG] = k_ref[n]
        sbuf[n, SUBLANE:SUBLANE + L, 2 * SEG:3 * SEG] = v_ref[n]
        lbuf[n, SUBLANE:SUBLANE + L, 0:LANE] = xw_ref[n]
        lbuf[n, SUBLANE:SUBLANE + L, LANE:2 * LANE] = xa_ref[n]
        cur = sbuf[n, SUBLANE:SUBLANE + L, :]
        prv = sbuf[n, SUBLANE - 1:SUBLANE - 1 + L, :]
        rkv = cur + (prv - cur) * mu_ref[...]
        curl = lbuf[n, SUBLANE:SUBLANE + L, :]
        prvl = lbuf[n, SUBLANE - 1:SUBLANE - 1 + L, :]
        xw = curl[:, 0:LANE] + (prvl[:, 0:LANE] - curl[:, 0:LANE]) * muw_ref[...]
        xa = curl[:, LANE:] + (prvl[:, LANE:] - curl[:, LANE:]) * mua_ref[...]
        sbuf[n, 0:SUBLANE, :] = sbuf[n, L:L + SUBLANE, :]
        lbuf[n, 0:SUBLANE, :] = lbuf[n, L:L + SUBLANE, :]
        r = rkv[:, 0:SEG]
        kr = rkv[:, SEG:2 * SEG]
        vr = rkv[:, 2 * SEG:3 * SEG]
        wl = w0_ref[...] + _mm(jnp.tanh(xw).astype(BF16), wl_ref[...])
        logw = -math.exp(-W_DECAY_OFFSET) * _sigmoid(wl)
        a = _sigmoid(a0_ref[...] + _mm(xa.astype(BF16), al_ref[...]))
        kkr = kr * kk_ref[...]
        kt = kr * (1.0 + (a - 1.0) * ka_ref[...])
        ss, bonus = _block_sums([kkr * kkr, r * kt * rk_ref[...]], m_kk_b)
        kk = kkr * lax.rsqrt(ss + EPS)
        ah = a * kk
        cum = _cumsum_rows(logw, tri_b)
        c_last = cum[L - 1:L, :]
        e_neg = jnp.exp(-cum)
        e_last = jnp.exp(c_last - cum)
        kx = kk * jnp.exp(cum - logw)
        rx = r * jnp.exp(cum)
        kb = kt * e_neg
        ab = ah * e_neg
        kh = kt * e_last
        ahh = ah * e_last
        pre[n] = dict(vr=vr, bonus=bonus, kx=kx, rx=rx, kb=kb, ab=ab, kh=kh, ahh=ahh, g_last=jnp.exp(c_last))
        yield

    def algebra(seqs):
        chains = [(n, g) for g in range(RWKV_GROUPS) for n in seqs]
        gsl = lambda g: slice(g * GROUP, (g + 1) * GROUP)
        lhs, c_mat, e_mat, rk_mat, ra_mat = {}, {}, {}, {}, {}
        for c in chains:
            p, sl = pre[c[0]], gsl(c[1])
            lhs[c] = jnp.concatenate([p["kx"][:, sl], p["rx"][:, sl]], axis=0).astype(BF16)
            rhs_nt = jnp.concatenate([bdiag(p["kb"][:, sl]), bdiag(p["ab"][:, sl])], axis=0)
            prod = _mm_nt(lhs[c], rhs_nt)
            c_mat[c] = jnp.where(strict, prod[0:L, 0:pw_w], 0.0)
            e_mat[c] = jnp.where(strict, prod[0:L, pw_w:], 0.0)
            rk_mat[c] = jnp.where(incl, prod[L:, 0:pw_w], 0.0)
            ra_mat[c] = jnp.where(incl, prod[L:, pw_w:], 0.0)
        yield
        t_inv = dict(zip(chains, (yield from _unit_lower_inverses([e_mat[c] for c in chains], eye, lm_p))))
        s_g, ks_rs, cv, u = {}, {}, {}, {}
        for c in chains:
            s_g[c] = sbt[c[0], c[1]]
            ks_rs[c] = _mm_nt(lhs[c], s_g[c].astype(BF16))
            cv[c] = _mm(jnp.concatenate([c_mat[c], rk_mat[c]], axis=0).astype(BF16),
                        bdiag(pre[c[0]]["vr"][:, gsl(c[1])]))
        yield
        for c in chains:
            u[c] = _mm(t_inv[c].astype(BF16), bdiag(ks_rs[c][0:L] + cv[c][0:L]))
        yield
        y_g = {}
        for c in chains:
            p, sl = pre[c[0]], gsl(c[1])
            y_g[c] = ks_rs[c][L:] + cv[c][L:] - _mm(ra_mat[c].astype(BF16), bdiag(u[c]))
            upd = _mm_tn(jnp.concatenate([p["vr"][:, sl], -u[c]], axis=0).astype(BF16),
                         jnp.concatenate([p["kh"][:, sl], p["ahh"][:, sl]], axis=0).astype(BF16))
            sbt[c[0], c[1]] = s_g[c] * p["g_last"][:, sl] + jnp.where(m_kk, upd, 0.0)
        yield

        for n in seqs:
            y = jnp.concatenate([y_g[(n, g)] for g in range(RWKV_GROUPS)], axis=1)
            dlt = y - _block_sums([y], m_kk_b)[0] * inv_hd
            var = _block_sums([dlt * dlt], m_kk_b)[0] * inv_hd
            yn = dlt * lax.rsqrt(var + GN_EPS) * lnw_ref[...] + lnb_ref[...]
            o_ref[n] = ((yn + pre[n]["bonus"] * pre[n]["vr"]) * _silu(z_ref[n])).astype(o_ref.dtype)

    yield from _stagger(prepare, algebra, nb)
    yield MAIN_DONE

    @pl.when(i == nsteps - 1)
    def _():
        for n in range(nb):
            for h in range(RWKV_HEADS):
                sfin_ref[n, h] = sbt[n, h // PACK, hsl(h), hsl(h)]


N_GDN_IN, N_RWKV_IN = 11, 24
MAIN_DONE = "main stages done"


def _scan_kernel(*refs, nb, chunk, nsteps):
    g_in = refs[0:N_GDN_IN]
    r_in = refs[N_GDN_IN:N_GDN_IN + N_RWKV_IN]
    og_ref, sg_ref, or_ref, sr_ref, ubuf, sb, sbuf, lbuf, sbt = refs[N_GDN_IN + N_RWKV_IN:]
    kw = dict(nb=nb, chunk=chunk, nsteps=nsteps)
    bodies = [_gdn_body(*g_in, og_ref, sg_ref, ubuf, sb, **kw), _rwkv_body(*r_in, or_ref, sr_ref, sbuf, lbuf, sbt, **kw)]
    for body in bodies:
        next(body)
    active = list(bodies)
    while active:
        for body in list(active):
            if next(body) is MAIN_DONE:
                active.remove(body)
    for body in bodies:
        for _ in body:
            pass


def _scans(proj3, conv_prev, s_gdn, shift_prev, s_rwkv, p, nb):
    b, t, _ = proj3.shape
    L = min(GDN_CHUNK, t)
    assert b % nb == 0 and t % L == 0 and L % SUBLANE == 0
    nsteps = t // L
    blk = lambda c, w: pl.BlockSpec((nb, L, w), lambda g, i, c=c: (g, i, c))
    const = lambda shape: pl.BlockSpec(shape, lambda g, i: (0,) * len(shape))
    per_seq = lambda *shape: pl.BlockSpec((nb,) + shape, lambda g, i: (g,) + (0,) * len(shape))
    prev_pad = jnp.pad(conv_prev, ((0, 0), (SUBLANE - (CONV_W - 1), 0), (0, 0)))
    lane_pad = (GDN_HEADS, LANE - 2 * GDN_HEADS)
    alog_row = jnp.pad(p["a_log"], lane_pad).reshape(1, LANE)
    dtb_row = jnp.pad(p["dt_bias"], lane_pad).reshape(1, LANE)
    on_row = jnp.tile(p["onorm_g"], GDN_HEADS).reshape(1, GDN_W)
    gdn_state = per_seq(GDN_HEADS, GDN_DK, GDN_DV)
    gdn_specs = [
        blk(COL_Q, SEG), blk(COL_K, SEG), blk(COL_V, SEG), blk(COL_ZG, SEG), blk(COL_BA // LANE, LANE),
        const((CONV_W, 3 * GDN_W)), per_seq(SUBLANE, 3 * GDN_W), gdn_state,
        const((1, LANE)), const((1, LANE)), const((1, GDN_W)),
    ]
    gdn_args = [proj3] * 5 + [p["conv_w"], prev_pad, s_gdn, alog_row, dtb_row, on_row]
    w3 = 3 * RWKV_W
    padl = lambda x, n: jnp.pad(x, [(0, 0)] * (x.ndim - 1) + [(0, LANE - n)])
    p_r, p_k, p_v = shift_prev[..., 0:RWKV_W], shift_prev[..., RWKV_W:2 * RWKV_W], shift_prev[..., 2 * RWKV_W:w3]
    p_xw = padl(shift_prev[..., w3:w3 + W_LORA], W_LORA)
    p_xa = padl(shift_prev[..., w3 + W_LORA:], A_LORA)
    mu = p["mu"]
    mu_rkv = mu[0:w3].reshape(1, w3)
    mu_w = padl(mu[w3:w3 + W_LORA], W_LORA).reshape(1, LANE)
    mu_a = padl(mu[w3 + W_LORA:], A_LORA).reshape(1, LANE)
    wl_pad = jnp.pad(p["w_lora"], ((0, LANE - W_LORA), (0, 0))).astype(BF16)
    al_pad = jnp.pad(p["a_lora"], ((0, LANE - A_LORA), (0, 0))).astype(BF16)
    s0_t = s_rwkv.transpose(0, 1, 3, 2)
    row1 = lambda x: x.reshape(1, RWKV_W)
    rwkv_state = per_seq(RWKV_HEADS, RWKV_HD, RWKV_HD)
    rwkv_specs = [
        blk(COL_R, SEG), blk(COL_RK, SEG), blk(COL_RV, SEG),
        blk(COL_XW // LANE, LANE), blk(COL_XA // LANE, LANE), blk(COL_ZR, SEG),
        per_seq(1, SEG), per_seq(1, SEG), per_seq(1, SEG), per_seq(1, LANE), per_seq(1, LANE),
        rwkv_state,
        const((1, w3)), const((1, LANE)), const((1, LANE)),
        const((1, RWKV_W)), const((LANE, RWKV_W)), const((1, RWKV_W)), const((LANE, RWKV_W)),
        const((1, RWKV_W)), const((1, RWKV_W)), const((1, RWKV_W)), const((1, RWKV_W)), const((1, RWKV_W)),
    ]
    rwkv_args = [proj3] * 6 + [p_r, p_k, p_v, p_xw, p_xa, s0_t, mu_rkv, mu_w, mu_a, row1(p["w0"]), wl_pad,
                               row1(p["a0"]), al_pad, row1(p["k_k"]), row1(p["k_a"]), row1(p["r_k"]),
                               row1(p["ln_w"]), row1(p["ln_b"])]
    assert len(gdn_specs) == N_GDN_IN and len(rwkv_specs) == N_RWKV_IN
    o_g, sg, o_r, sr = pl.pallas_call(
        functools.partial(_scan_kernel, nb=nb, chunk=L, nsteps=nsteps),
        grid=(b // nb, nsteps),
        in_specs=gdn_specs + rwkv_specs,
        out_specs=[
            pl.BlockSpec((nb, L, GDN_W), lambda g, i: (g, i, 0)), gdn_state,
            pl.BlockSpec((nb, L, RWKV_W), lambda g, i: (g, i, 0)), rwkv_state,
        ],
        out_shape=[
            jax.ShapeDtypeStruct((b, t, GDN_W), BF16),
            jax.ShapeDtypeStruct((b, GDN_HEADS, GDN_DK, GDN_DV), F32),
            jax.ShapeDtypeStruct((b, t, RWKV_W), BF16),
            jax.ShapeDtypeStruct((b, RWKV_HEADS, RWKV_HD, RWKV_HD), F32),
        ],
        scratch_shapes=[
            pltpu.VMEM((nb, L + SUBLANE, 3 * GDN_W), F32),
            pltpu.VMEM((nb, GDN_PAIRS, GROUP, GROUP), F32),
            pltpu.VMEM((nb, L + SUBLANE, w3), F32),
            pltpu.VMEM((nb, L + SUBLANE, 2 * LANE), F32),
            pltpu.VMEM((nb, RWKV_GROUPS, GROUP, GROUP), F32),
        ],
        compiler_params=pltpu.CompilerParams(
            dimension_semantics=("arbitrary", "arbitrary"), vmem_limit_bytes=VMEM_LIMIT),
        name="scan",
    )(*gdn_args, *rwkv_args)
    return o_g.reshape(b * t, GDN_W), sg, o_r.reshape(b * t, RWKV_W), sr.transpose(0, 1, 3, 2)


def _merge_kernel(og_ref, or_ref, gg_ref, gr_ref, x_ref, gate_ref, wog_ref, wor_ref, wout_ref, fg_ref, o_ref,
                  *, final_norm):
    m = _sigmoid(gg_ref[...]) * _mm(og_ref[...], wog_ref[...]) \
        + _sigmoid(gr_ref[...]) * _mm(or_ref[...], wor_ref[...])
    out = _mm(m.astype(BF16), wout_ref[...])
    xn = x_ref[...] + gate_ref[...] * out
    if final_norm:
        xn = xn * lax.rsqrt(jnp.mean(xn * xn, axis=-1, keepdims=True) + EPS) * fg_ref[...]
    o_ref[...] = xn


def _merge(o_g, o_r, proj, x2d, gate, w_og, w_or, w_out, final_g, rows_per_mod, final_norm):
    m, d = x2d.shape
    tm = min(m, 256) if rows_per_mod == 1 else min(rows_per_mod, 256)
    if rows_per_mod == 1:
        gate_spec = pl.BlockSpec((tm, d), lambda i: (i, 0))
    else:
        assert rows_per_mod % tm == 0
        per = rows_per_mod // tm
        gate = gate.reshape(-1, 1, d)
        gate_spec = pl.BlockSpec((None, 1, d), lambda i: (i // per, 0, 0))
    whole = lambda shape: pl.BlockSpec(shape, lambda i: (0, 0))
    gcol = COL_GATES // d
    return pl.pallas_call(
        functools.partial(_merge_kernel, final_norm=final_norm),
        grid=(m // tm,),
        in_specs=[
            pl.BlockSpec((tm, GDN_W), lambda i: (i, 0)),
            pl.BlockSpec((tm, RWKV_W), lambda i: (i, 0)),
            pl.BlockSpec((tm, d), lambda i: (i, gcol)),
            pl.BlockSpec((tm, d), lambda i: (i, gcol + 1)),
            pl.BlockSpec((tm, d), lambda i: (i, 0)),
            gate_spec,
            whole((GDN_W, d)), whole((RWKV_W, d)), whole((d, d)), whole((1, d)),
        ],
        out_specs=pl.BlockSpec((tm, d), lambda i: (i, 0)),
        out_shape=jax.ShapeDtypeStruct((m, d), F32),
        compiler_params=pltpu.CompilerParams(
            dimension_semantics=("arbitrary",), vmem_limit_bytes=VMEM_LIMIT),
        name="merge_out",
    )(o_g, o_r, proj, proj, x2d, gate, w_og, w_or, w_out, final_g.reshape(1, d))


def _pad_in_weight(w_in):
    d = w_in.shape[0]
    o_zg = 3 * GDN_W
    o_b = o_zg + GDN_W
    o_rw = o_b + 2 * GDN_HEADS
    o_xw = o_rw + 3 * RWKV_W
    o_xa = o_xw + W_LORA
    o_zr = o_xa + A_LORA
    o_br = o_zr + RWKV_W
    z = lambda n: jnp.zeros((d, n), w_in.dtype)
    cols = [
        w_in[:, 0:o_zg], w_in[:, o_zg:o_b],
        w_in[:, o_rw:o_xw], w_in[:, o_zr:o_br],
        w_in[:, o_br:o_br + 2 * D_MODEL],
        w_in[:, o_b:o_rw], z(LANE - 2 * GDN_HEADS),
        w_in[:, o_xw:o_xa], z(LANE - W_LORA),
        w_in[:, o_xa:o_zr], z(LANE - A_LORA),
    ]
    used = COL_XA + LANE
    cols.append(z(N_PROJ - used))
    return jnp.concatenate(cols, axis=1).astype(BF16)


def _layer(x2d, batch, seq, scale, shift, gate, rows_per_mod, conv_prev, s_gdn, shift_prev, s_rwkv, p, final_g,
           final_norm, scan_nb):
    proj = _in_proj(x2d, scale, shift, p["norm_g"], p["w_in_pad"], rows_per_mod)
    proj3 = proj.reshape(batch, seq, N_PROJ)
    o_g, s_gdn_new, o_r, s_rwkv_new = _scans(proj3, conv_prev, s_gdn, shift_prev, s_rwkv, p, scan_nb)
    x_new = _merge(o_g, o_r, proj, x2d, gate, p["w_o_gdn"], p["w_o_rwkv"], p["w_out"], final_g, rows_per_mod,
                   final_norm)
    conv_new = proj3[:, seq - (CONV_W - 1):, 0:3 * GDN_W]
    last = proj3[:, seq - 1:, :]
    shift_new = jnp.concatenate(
        [last[..., COL_R * SEG:COL_R * SEG + 3 * RWKV_W], last[..., COL_XW:COL_XW + W_LORA],
         last[..., COL_XA:COL_XA + A_LORA]], axis=-1)
    return x_new, conv_new, s_gdn_new, shift_new, s_rwkv_new


def _forward(x_prompt, x_sample, c_prompt, c_sample, cache_gdn_conv, state_gdn, cache_rwkv_shift, state_rwkv,
             ada_w, ada_b, norm_g, w_in, gdn_conv_w, gdn_a_log, gdn_dt_bias, gdn_out_norm_g,
             rwkv_mu, rwkv_w0, rwkv_w_lora, rwkv_a0, rwkv_a_lora, rwkv_k_k, rwkv_k_a, rwkv_r_k,
             rwkv_ln_w, rwkv_ln_b, w_o_gdn, w_o_rwkv, w_out, final_norm_g):
    depth = ada_w.shape[0]
    bp, tp, d = x_prompt.shape
    bs, ts, _ = x_sample.shape
    assert CONV_W - 1 <= min(tp, ts)
    c_all = jnp.concatenate([c_prompt, c_sample], axis=0)
    rows = -(-(bp + bs) // SUBLANE) * SUBLANE
    c_all = jnp.pad(c_all, ((0, rows - (bp + bs)), (0, 0)))
    mod = _ada_mod(c_all, ada_w, ada_b)
    xp = x_prompt.reshape(bp * tp, d)
    xs = x_sample.reshape(bs * ts, d)
    outs = [[] for _ in range(8)]
    for l in range(depth):
        p = dict(norm_g=norm_g[l], w_in_pad=_pad_in_weight(w_in[l]), conv_w=gdn_conv_w[l], a_log=gdn_a_log[l],
                 dt_bias=gdn_dt_bias[l], onorm_g=gdn_out_norm_g[l], mu=rwkv_mu[l], w0=rwkv_w0[l],
                 w_lora=rwkv_w_lora[l], a0=rwkv_a0[l], a_lora=rwkv_a_lora[l], k_k=rwkv_k_k[l], k_a=rwkv_k_a[l],
                 r_k=rwkv_r_k[l].reshape(-1), ln_w=rwkv_ln_w[l], ln_b=rwkv_ln_b[l],
                 w_o_gdn=w_o_gdn[l].astype(BF16), w_o_rwkv=w_o_rwkv[l].astype(BF16), w_out=w_out[l].astype(BF16))
        last = l == depth - 1
        m_p = mod[l, 0:bp]
        m_s = jnp.repeat(mod[l, bp:bp + bs], ts, axis=0)
        sh_p, sc_p, gt_p = m_p[:, 0:d], m_p[:, d:2 * d], m_p[:, 2 * d:]
        sh_s, sc_s, gt_s = m_s[:, 0:d], m_s[:, d:2 * d], m_s[:, 2 * d:]
        zeros = lambda *s: jnp.zeros(s, F32)
        xp, c1, g1, h1, r1 = _layer(
            xp, bp, tp, sc_p, sh_p, gt_p, tp,
            zeros(bp, CONV_W - 1, 3 * GDN_W), zeros(bp, GDN_HEADS, GDN_DK, GDN_DV),
            zeros(bp, 1, 3 * RWKV_W + W_LORA + A_LORA), zeros(bp, RWKV_HEADS, RWKV_HD, RWKV_HD),
            p, final_norm_g, last, scan_nb=bp)
        xs, c2, g2, h2, r2 = _layer(
            xs, bs, ts, sc_s, sh_s, gt_s, 1,
            cache_gdn_conv[l], state_gdn[l], cache_rwkv_shift[l], state_rwkv[l],
            p, final_norm_g, last, scan_nb=min(bs, 2))
        for lst, val in zip(outs, (c1, g1, h1, r1, c2, g2, h2, r2)):
            lst.append(val)
    stk = [jnp.stack(v) for v in outs]
    return (xp.reshape(bp, tp, d), xs.reshape(bs, ts, d), *stk)


def kernel(x_prompt, x_sample, c_prompt, c_sample, cache_gdn_conv, state_gdn, cache_rwkv_shift, state_rwkv, ada_w, ada_b, norm_g, w_in, gdn_conv_w, gdn_a_log, gdn_dt_bias, gdn_out_norm_g, rwkv_mu, rwkv_w0, rwkv_w_lora, rwkv_a0, rwkv_a_lora, rwkv_k_k, rwkv_k_a, rwkv_r_k, rwkv_ln_w, rwkv_ln_b, w_o_gdn, w_o_rwkv, w_out, final_norm_g):
    return _forward(x_prompt, x_sample, c_prompt, c_sample, cache_gdn_conv, state_gdn, cache_rwkv_shift,
                    state_rwkv, ada_w, ada_b, norm_g, w_in, gdn_conv_w, gdn_a_log, gdn_dt_bias, gdn_out_norm_g,
                    rwkv_mu, rwkv_w0, rwkv_w_lora, rwkv_a0, rwkv_a_lora, rwkv_k_k, rwkv_k_a, rwkv_r_k,
                    rwkv_ln_w, rwkv_ln_b, w_o_gdn, w_o_rwkv, w_out, final_norm_g)
```

```python
import functools
import math

import jax
import jax.numpy as jnp
from jax import lax
from jax.experimental import pallas as pl
from jax.experimental.pallas import tpu as pltpu

F32 = jnp.float32
BF16 = jnp.bfloat16

D_MODEL = 2048
GDN_HEADS = 8
GDN_DK = 128
GDN_DV = 128
GDN_W = GDN_HEADS * GDN_DK
GDN_CHUNK = 64
CONV_W = 4
RWKV_HEADS = 16
RWKV_HD = 64
RWKV_W = RWKV_HEADS * RWKV_HD
W_LORA = 96
A_LORA = 96
EPS = 1e-6
GN_EPS = 64e-5
W_DECAY_OFFSET = 0.5

LANE = 128
SUBLANE = 8
SEG = 1024
COL_Q, COL_K, COL_V, COL_ZG, COL_R, COL_RK, COL_RV, COL_ZR = 0, 1, 2, 3, 4, 5, 6, 7
COL_GATES = 8 * SEG
COL_BA = 12 * SEG
COL_XW = COL_BA + LANE
COL_XA = COL_XW + LANE
GROUP = 256
PROJ_TILE = 5 * GROUP
N_PROJ = 10 * PROJ_TILE
PACK = 4
RWKV_GROUPS = RWKV_W // GROUP
GDN_PAIRS = GDN_W // GROUP
VMEM_LIMIT = 56 * 1024 * 1024


def _mm(a, b):
    return jnp.dot(a, b, preferred_element_type=F32)


def _mm_nt(a, b):
    return lax.dot_general(a, b, (((1,), (1,)), ((), ())), preferred_element_type=F32)


def _mm_tn(a, b):
    return lax.dot_general(a, b, (((0,), (0,)), ((), ())), preferred_element_type=F32)


def _split_bf16(x):
    hi = x.astype(BF16)
    return hi, (x - hi.astype(F32)).astype(BF16)


def _split3_bf16(x):
    hi = x.astype(BF16)
    mid, lo = _split_bf16(x - hi.astype(F32))
    return hi, mid, lo


def _sigmoid(x):
    return 1.0 / (1.0 + jnp.exp(-x))


def _silu(x):
    return x * _sigmoid(x)


def _softplus(x):
    return jnp.maximum(x, 0.0) + jnp.log1p(jnp.exp(-jnp.abs(x)))


def _iota(shape, dim):
    return lax.broadcasted_iota(jnp.int32, shape, dim)


def _bits(n):
    assert n & (n - 1) == 0
    return n.bit_length() - 1


def _mod_kernel(c_ref, w_ref, b_ref, o_ref):
    o_ref[...] = _mm(c_ref[...], w_ref[...]) + b_ref[...]


def _ada_mod(c_all, ada_w, ada_b):
    depth, d, n3 = ada_w.shape
    rows = c_all.shape[0]
    tn = 512
    return pl.pallas_call(
        _mod_kernel,
        grid=(depth, n3 // tn),
        in_specs=[
            pl.BlockSpec((rows, d), lambda l, j: (0, 0)),
            pl.BlockSpec((None, d, tn), lambda l, j: (l, 0, j)),
            pl.BlockSpec((None, 1, tn), lambda l, j: (l, 0, j)),
        ],
        out_specs=pl.BlockSpec((None, rows, tn), lambda l, j: (l, 0, j)),
        out_shape=jax.ShapeDtypeStruct((depth, rows, n3), F32),
        compiler_params=pltpu.CompilerParams(
            dimension_semantics=("arbitrary", "arbitrary"), vmem_limit_bytes=VMEM_LIMIT),
        name="ada_mod",
    )(c_all, ada_w, ada_b.reshape(depth, 1, n3))


def _in_proj_kernel(x_ref, sc_ref, sh_ref, g_ref, w_ref, o_ref, h_ref):
    @pl.when(pl.program_id(1) == 0)
    def _():
        x = x_ref[...]
        ms = jnp.mean(x * x, axis=-1, keepdims=True)
        h = x * lax.rsqrt(ms + EPS) * g_ref[...] * (1.0 + sc_ref[...]) + sh_ref[...]
        h_ref[...] = h.astype(BF16)

    o_ref[...] = _mm(h_ref[...], w_ref[...])


def _in_proj(x2d, scale, shift, norm_g, w_pad, rows_per_mod):
    m, d = x2d.shape
    tm = min(m, 512) if rows_per_mod == 1 else min(rows_per_mod, 1024)
    tn = PROJ_TILE
    if rows_per_mod == 1:
        mod_spec = pl.BlockSpec((tm, d), lambda i, j: (i, 0))
    else:
        assert rows_per_mod % tm == 0
        per = rows_per_mod // tm
        scale = scale.reshape(-1, 1, d)
        shift = shift.reshape(-1, 1, d)
        mod_spec = pl.BlockSpec((None, 1, d), lambda i, j: (i // per, 0, 0))
    return pl.pallas_call(
        _in_proj_kernel,
        grid=(m // tm, N_PROJ // tn),
        in_specs=[
            pl.BlockSpec((tm, d), lambda i, j: (i, 0)),
            mod_spec, mod_spec,
            pl.BlockSpec((1, d), lambda i, j: (0, 0)),
            pl.BlockSpec((d, tn), lambda i, j: (0, j)),
        ],
        out_specs=pl.BlockSpec((tm, tn), lambda i, j: (i, j)),
        out_shape=jax.ShapeDtypeStruct((m, N_PROJ), F32),
        scratch_shapes=[pltpu.VMEM((tm, d), BF16)],
        compiler_params=pltpu.CompilerParams(
            dimension_semantics=("arbitrary", "arbitrary"), vmem_limit_bytes=VMEM_LIMIT),
        name="in_proj",
    )(x2d, scale, shift, norm_g.reshape(1, d), w_pad)


def _round_robin(gens):
    gens = list(gens)
    while gens:
        for gen in list(gens):
            try:
                next(gen)
            except StopIteration:
                gens.remove(gen)
        yield


def _stagger(prepare, algebra, n_seq):
    yield from prepare(0)
    for n in range(1, n_seq):
        yield from _round_robin([algebra([n - 1]), prepare(n)])
    yield from algebra([n_seq - 1])


def _lane_masks(rows, head_lanes):
    if head_lanes >= LANE:
        return None
    lane = _iota((rows, LANE), 1)
    return [((lane >= o) & (lane < o + head_lanes)).astype(BF16) for o in range(0, LANE, head_lanes)]


def _block_diag(x, n_heads, lane_masks):
    L, w = x.shape
    hl = w // n_heads
    span = max(hl, LANE)
    rows = []
    for h in range(n_heads):
        c0 = (h * hl) // span * span
        piece = x[:, c0:c0 + span]
        if hl < LANE:
            piece = piece * lane_masks[(h * hl - c0) // hl]
        parts = [piece]
        if c0:
            parts.insert(0, jnp.zeros((L, c0), x.dtype))
        if w - c0 - span:
            parts.append(jnp.zeros((L, w - c0 - span), x.dtype))
        rows.append(jnp.concatenate(parts, axis=1) if len(parts) > 1 else piece)
    return jnp.concatenate(rows, axis=0)


def _head_mask(rows, row_bits, cols, col_bits):
    return (_iota((rows, cols), 0) >> row_bits) == (_iota((rows, cols), 1) >> col_bits)


def _block_sums(xs, gmat_b):
    t, width = xs[0].shape
    ng = width // GROUP
    parts = []
    for x in xs:
        xb = x.astype(BF16)
        parts += [xb[:, g * GROUP:(g + 1) * GROUP] for g in range(ng)]
    res = _mm(jnp.concatenate(parts, axis=0), gmat_b)
    return [jnp.concatenate([res[(i * ng + g) * t:(i * ng + g + 1) * t] for g in range(ng)], axis=1)
            for i in range(len(xs))]


def _cumsum_rows(x, tri_b):
    w = x.shape[1]
    res = _mm(tri_b, jnp.concatenate(_split3_bf16(x), axis=1))
    return res[:, 0:w] + (res[:, w:2 * w] + res[:, 2 * w:])


def _unit_lower_inverses(e_mats, eye, lane_masks):
    L, pw_w = e_mats[0].shape
    steps = _bits(L)
    pws = [-e for e in e_mats]
    t_invs = [eye + pw for pw in pws]
    for s in range(steps):
        first, last = s == 0, s == steps - 1
        if first and last:
            break
        for c in range(len(e_mats)):
            p_hi, p_lo = _split_bf16(pws[c])
            bd_hi = _block_diag(p_hi, PACK, lane_masks)
            bd_lo = _block_diag(p_lo, PACK, lane_masks)
            rows = ([] if first else [_split_bf16(t_invs[c])]) + ([] if last else [(p_hi, p_lo)])
            x_hi = jnp.concatenate([hi for hi, _ in rows], axis=0)
            x_lo = jnp.concatenate([lo for _, lo in rows], axis=0)
            nr = x_hi.shape[0]
            hh = _mm(jnp.concatenate([x_hi, x_lo], axis=0), bd_hi)
            res = hh[0:nr] + (hh[nr:] + _mm(x_hi, bd_lo))
            if not first:
                t_invs[c] = t_invs[c] + res[0:L]
            if not last:
                pws[c] = res[nr - L:]
        yield
    return t_invs


def _gdn_body(q_ref, k_ref, v_ref, z_ref, ba_ref, cw_ref, prev_ref, s0_ref, alog_ref, dtb_ref, on_ref,
              o_ref, sfin_ref, ubuf, sb, *, nb, chunk, nsteps):
    L = chunk
    i = pl.program_id(1)
    l_bits = _bits(L)
    dk_bits = _bits(GDN_DK)
    pw_w = PACK * L
    hw = PACK * GDN_DK
    n_grp = GDN_HEADS // PACK
    hp = GROUP // GDN_DK
    dsl = lambda h: slice((h % hp) * GDN_DK, (h % hp + 1) * GDN_DK)

    lm_p = _lane_masks(L, L)
    m_dd = _head_mask(GROUP, dk_bits, GROUP, dk_bits)
    m_dd_b = m_dd.astype(BF16)
    t_row = _iota((L, pw_w), 0)
    j_lane = _iota((L, pw_w), 1) & (L - 1)
    strict = j_lane < t_row
    incl = j_lane <= t_row
    eye = (j_lane == t_row).astype(F32)
    tri_b = (_iota((L, L), 0) >= _iota((L, L), 1)).astype(BF16)
    ones_b = jnp.ones((L, L), BF16)
    lane = _iota((L, LANE), 1)
    x_rows = _iota((LANE, GDN_HEADS * L + GDN_W), 0) & (GDN_HEADS - 1)
    x_cols = _iota((LANE, GDN_HEADS * L + GDN_W), 1)
    x_head = jnp.where(x_cols < GDN_HEADS * L, x_cols >> l_bits, (x_cols - GDN_HEADS * L) >> dk_bits)
    expand_b = ((x_rows == x_head) & (_iota((LANE, GDN_HEADS * L + GDN_W), 0) < 2 * GDN_HEADS)).astype(BF16)

    def bdiag(x):
        return _block_diag(x.astype(BF16), PACK, None)

    @pl.when(i == 0)
    def _():
        sb[...] = jnp.zeros(sb.shape, F32)
        for n in range(nb):
            ubuf[n, 0:SUBLANE, :] = prev_ref[n]
            for h in range(GDN_HEADS):
                sb[n, h // hp, dsl(h), dsl(h)] = s0_ref[n, h]

    yield
    base = SUBLANE - (CONV_W - 1)
    pre = {}

    def prepare(n):
        ubuf[n, SUBLANE:SUBLANE + L, 0:GDN_W] = q_ref[n]
        ubuf[n, SUBLANE:SUBLANE + L, GDN_W:2 * GDN_W] = k_ref[n]
        ubuf[n, SUBLANE:SUBLANE + L, 2 * GDN_W:3 * GDN_W] = v_ref[n]
        full = ubuf[n, :, :]
        tap = lambda j: (full if j == CONV_W - 1 else pltpu.roll(full, CONV_W - 1 - j, axis=0))[SUBLANE:, :]
        y = tap(0) * cw_ref[0:1, :]
        for j in range(1, CONV_W):
            y = y + tap(j) * cw_ref[j:j + 1, :]
        ubuf[n, 0:SUBLANE, :] = ubuf[n, L:L + SUBLANE, :]
        act = _silu(y)
        q = act[:, 0:GDN_W]
        k = act[:, GDN_W:2 * GDN_W]
        v = act[:, 2 * GDN_W:]
        ssq, ssk = _block_sums([q * q, k * k], m_dd_b)
        q = q * lax.rsqrt(ssq + EPS) * (GDN_DK ** -0.5)
        k = k * lax.rsqrt(ssk + EPS)

        ba = ba_ref[n]
        beta_c = jnp.where(lane < GDN_HEADS, _sigmoid(ba), 0.0)
        g_c = -jnp.exp(alog_ref[...]) * _softplus(ba + dtb_ref[...])
        gc_c = jnp.where((lane >= GDN_HEADS) & (lane < 2 * GDN_HEADS), _cumsum_rows(g_c, tri_b), 0.0)
        pieces = jnp.concatenate(list(_split_bf16(beta_c)) + list(_split3_bf16(gc_c)), axis=0)
        ex = _mm(pieces, expand_b)
        beta_x = ex[0:L] + ex[L:2 * L]
        gc_x = ex[2 * L:3 * L] + (ex[3 * L:4 * L] + ex[4 * L:])
        beta_p, beta_w = beta_x[:, 0:GDN_HEADS * L], beta_x[:, GDN_HEADS * L:]
        gc_p, gc_w = gc_x[:, 0:GDN_HEADS * L], gc_x[:, GDN_HEADS * L:]
        eye8 = jnp.concatenate([eye] * n_grp, axis=1)
        incl8 = jnp.concatenate([incl] * n_grp, axis=1)
        rw = _mm(ones_b, jnp.concatenate(_split3_bf16(gc_p * eye8), axis=1))
        pw8 = GDN_HEADS * L
        gc_row = rw[:, 0:pw8] + (rw[:, pw8:2 * pw8] + rw[:, 2 * pw8:])
        dm = jnp.where(incl8, jnp.exp(jnp.where(incl8, gc_p - gc_row, 0.0)), 0.0)
        gl_w = gc_w[L - 1:L, :]
        eg_w = jnp.exp(gc_w)
        qd = q * eg_w
        kd = k * jnp.exp(gl_w - gc_w)
        bv = beta_w * v
        bek = beta_w * eg_w * k
        pre[n] = dict(q=q, k=k, beta_p=beta_p, dm=dm, qd=qd, kd=kd, bv=bv, bek=bek, egl_w=jnp.exp(gl_w))
        yield

    def algebra(seqs):
        chains = [(n, g) for g in range(n_grp) for n in seqs]
        wsl = lambda g: slice(g * hw, (g + 1) * hw)
        psl = lambda g: slice(g * pw_w, (g + 1) * pw_w)
        ppg = hw // GROUP
        a_mat, qkd = {}, {}
        for c in chains:
            p, sw, sp = pre[c[0]], wsl(c[1]), psl(c[1])
            prod = _mm_nt(jnp.concatenate([p["k"][:, sw], p["q"][:, sw]], axis=0).astype(BF16), bdiag(p["k"][:, sw]))
            a_mat[c] = jnp.where(strict, p["beta_p"][:, sp] * prod[0:L] * p["dm"][:, sp], 0.0)
            qkd[c] = prod[L:] * p["dm"][:, sp]
        yield
        t_inv = dict(zip(chains, (yield from _unit_lower_inverses([a_mat[c] for c in chains], eye, lm_p))))
        sol = {}
        for c in chains:
            p, sw = pre[c[0]], wsl(c[1])
            sol[c] = _mm(t_inv[c].astype(BF16), jnp.concatenate([bdiag(p["bv"][:, sw]), bdiag(p["bek"][:, sw])], axis=1))
        yield
        pairs = [(c, j) for c in chains for j in range(ppg)]
        s_p, r2, u_p = {}, {}, {}
        for c, j in pairs:
            pair = c[1] * ppg + j
            sg = slice(pair * GROUP, (pair + 1) * GROUP)
            s_p[c, j] = sb[c[0], pair]
            r2[c, j] = _mm(jnp.concatenate([sol[c][:, hw + j * GROUP:hw + (j + 1) * GROUP], pre[c[0]]["qd"][:, sg]],
                                           axis=0).astype(BF16), s_p[c, j].astype(BF16))
        yield
        for c, j in pairs:
            pair = c[1] * ppg + j
            sg = slice(pair * GROUP, (pair + 1) * GROUP)
            u_p[c, j] = sol[c][:, j * GROUP:(j + 1) * GROUP] - r2[c, j][0:L]
            upd = _mm_tn(pre[c[0]]["kd"][:, sg].astype(BF16), u_p[c, j].astype(BF16))
            sb[c[0], pair] = s_p[c, j] * pre[c[0]]["egl_w"][:, sg] + jnp.where(m_dd, upd, 0.0)
        yield
        o_g = {}
        for c in chains:
            u = jnp.concatenate([u_p[c, j] for j in range(ppg)], axis=1)
            qs = jnp.concatenate([r2[c, j][L:] for j in range(ppg)], axis=1)
            o_g[c] = qs + _mm(qkd[c].astype(BF16), bdiag(u))
        yield

        for n in seqs:
            o = jnp.concatenate([o_g[(n, g)] for g in range(n_grp)], axis=1)
            sso, = _block_sums([o * o], m_dd_b)
            on = o * lax.rsqrt(sso * (1.0 / GDN_DV) + EPS) * on_ref[...]
            o_ref[n] = (on * _silu(z_ref[n])).astype(o_ref.dtype)

    yield from _stagger(prepare, algebra, nb)
    yield MAIN_DONE

    @pl.when(i == nsteps - 1)
    def _():
        for n in range(nb):
            for h in range(GDN_HEADS):
                sfin_ref[n, h] = sb[n, h // hp, dsl(h), dsl(h)]


def _rwkv_body(r_ref, k_ref, v_ref, xw_ref, xa_ref, z_ref, pr_ref, pk_ref, pv_ref, pxw_ref, pxa_ref, s0_ref,
               mu_ref, muw_ref, mua_ref, w0_ref, wl_ref, a0_ref, al_ref, kk_ref, ka_ref, rk_ref,
               lnw_ref, lnb_ref,
               o_ref, sfin_ref,
               sbuf, lbuf, sbt, *, nb, chunk, nsteps):
    i = pl.program_id(1)
    L = chunk
    pw_w = PACK * L
    hd_bits = _bits(RWKV_HD)

    lm_p = _lane_masks(L, L)
    lm_k = _lane_masks(L, RWKV_HD)
    m_kk = _head_mask(GROUP, hd_bits, GROUP, hd_bits)
    m_kk_b = m_kk.astype(BF16)
    t_row = _iota((L, pw_w), 0)
    j_lane = _iota((L, pw_w), 1) & (L - 1)
    strict = j_lane < t_row
    incl = j_lane <= t_row
    eye = (j_lane == t_row).astype(F32)
    tri_b = (_iota((L, L), 0) >= _iota((L, L), 1)).astype(BF16)
    hsl = lambda h: slice((h % PACK) * RWKV_HD, (h % PACK + 1) * RWKV_HD)

    def bdiag(x):
        return _block_diag(x.astype(BF16), PACK, lm_k)

    @pl.when(i == 0)
    def _():
        sbt[...] = jnp.zeros(sbt.shape, F32)
        for n in range(nb):
            for h in range(RWKV_HEADS):
                sbt[n, h // PACK, hsl(h), hsl(h)] = s0_ref[n, h]
            sbuf[n, 0:SUBLANE, :] = jnp.zeros((SUBLANE, 3 * SEG), F32)
            lbuf[n, 0:SUBLANE, :] = jnp.zeros((SUBLANE, 2 * LANE), F32)
            sbuf[n, SUBLANE - 1:SUBLANE, 0:SEG] = pr_ref[n]
            sbuf[n, SUBLANE - 1:SUBLANE, SEG:2 * SEG] = pk_ref[n]
            sbuf[n, SUBLANE - 1:SUBLANE, 2 * SEG:3 * SEG] = pv_ref[n]
            lbuf[n, SUBLANE - 1:SUBLANE, 0:LANE] = pxw_ref[n]
            lbuf[n, SUBLANE - 1:SUBLANE, LANE:2 * LANE] = pxa_ref[n]

    yield
    inv_hd = 1.0 / RWKV_HD
    pre = {}

    def prepare(n):
        sbuf[n, SUBLANE:SUBLANE + L, 0:SEG] = r_ref[n]
        sbuf[n, SUBLANE:SUBLANE + L, SEG:2 * SEG] = k_ref[n]
        sbuf[n, SUBLANE:SUBLANE + L, 2 * SEG:3 * SEG] = v_ref[n]
        lbuf[n, SUBLANE:SUBLANE + L, 0:LANE] = xw_ref[n]
        lbuf[n, SUBLANE:SUBLANE + L, LANE:2 * LANE] = xa_ref[n]
        cur = sbuf[n, SUBLANE:SUBLANE + L, :]
        prv = pltpu.roll(sbuf[n, :, :], 1, axis=0)[SUBLANE:, :]
        rkv = cur + (prv - cur) * mu_ref[...]
        curl = lbuf[n, SUBLANE:SUBLANE + L, :]
        prvl = pltpu.roll(lbuf[n, :, :], 1, axis=0)[SUBLANE:, :]
        xw = curl[:, 0:LANE] + (prvl[:, 0:LANE] - curl[:, 0:LANE]) * muw_ref[...]
        xa = curl[:, LANE:] + (prvl[:, LANE:] - curl[:, LANE:]) * mua_ref[...]
        sbuf[n, 0:SUBLANE, :] = sbuf[n, L:L + SUBLANE, :]
        lbuf[n, 0:SUBLANE, :] = lbuf[n, L:L + SUBLANE, :]
        r = rkv[:, 0:SEG]
        kr = rkv[:, SEG:2 * SEG]
        vr = rkv[:, 2 * SEG:3 * SEG]
        wl = w0_ref[...] + _mm(jnp.tanh(xw).astype(BF16), wl_ref[...])
        logw = -math.exp(-W_DECAY_OFFSET) * _sigmoid(wl)
        a = _sigmoid(a0_ref[...] + _mm(xa.astype(BF16), al_ref[...]))
        kkr = kr * kk_ref[...]
        kt = kr * (1.0 + (a - 1.0) * ka_ref[...])
        ss, bonus = _block_sums([kkr * kkr, r * kt * rk_ref[...]], m_kk_b)
        kk = kkr * lax.rsqrt(ss + EPS)
        ah = a * kk
        cum = _cumsum_rows(logw, tri_b)
        c_last = cum[L - 1:L, :]
        e_neg = jnp.exp(-cum)
        e_last = jnp.exp(c_last - cum)
        kx = kk * jnp.exp(cum - logw)
        rx = r * jnp.exp(cum)
        kb = kt * e_neg
        ab = ah * e_neg
        kh = kt * e_last
        ahh = ah * e_last
        pre[n] = dict(vr=vr, bonus=bonus, kx=kx, rx=rx, kb=kb, ab=ab, kh=kh, ahh=ahh, g_last=jnp.exp(c_last))
        yield

    def algebra(seqs):
        chains = [(n, g) for g in range(RWKV_GROUPS) for n in seqs]
        gsl = lambda g: slice(g * GROUP, (g + 1) * GROUP)
        lhs, c_mat, e_mat, rk_mat, ra_mat = {}, {}, {}, {}, {}
        for c in chains:
            p, sl = pre[c[0]], gsl(c[1])
            lhs[c] = jnp.concatenate([p["kx"][:, sl], p["rx"][:, sl]], axis=0).astype(BF16)
            rhs_nt = jnp.concatenate([bdiag(p["kb"][:, sl]), bdiag(p["ab"][:, sl])], axis=0)
            prod = _mm_nt(lhs[c], rhs_nt)
            c_mat[c] = jnp.where(strict, prod[0:L, 0:pw_w], 0.0)
            e_mat[c] = jnp.where(strict, prod[0:L, pw_w:], 0.0)
            rk_mat[c] = jnp.where(incl, prod[L:, 0:pw_w], 0.0)
            ra_mat[c] = jnp.where(incl, prod[L:, pw_w:], 0.0)
        yield
        t_inv = dict(zip(chains, (yield from _unit_lower_inverses([e_mat[c] for c in chains], eye, lm_p))))
        s_g, ks_rs, cv, u = {}, {}, {}, {}
        for c in chains:
            s_g[c] = sbt[c[0], c[1]]
            ks_rs[c] = _mm_nt(lhs[c], s_g[c].astype(BF16))
            cv[c] = _mm(jnp.concatenate([c_mat[c], rk_mat[c]], axis=0).astype(BF16),
                        bdiag(pre[c[0]]["vr"][:, gsl(c[1])]))
        yield
        for c in chains:
            u[c] = _mm(t_inv[c].astype(BF16), bdiag(ks_rs[c][0:L] + cv[c][0:L]))
        yield
        y_g = {}
        for c in chains:
            p, sl = pre[c[0]], gsl(c[1])
            y_g[c] = ks_rs[c][L:] + cv[c][L:] - _mm(ra_mat[c].astype(BF16), bdiag(u[c]))
            upd = _mm_tn(jnp.concatenate([p["vr"][:, sl], -u[c]], axis=0).astype(BF16),
                         jnp.concatenate([p["kh"][:, sl], p["ahh"][:, sl]], axis=0).astype(BF16))
            sbt[c[0], c[1]] = s_g[c] * p["g_last"][:, sl] + jnp.where(m_kk, upd, 0.0)
        yield

        for n in seqs:
            y = jnp.concatenate([y_g[(n, g)] for g in range(RWKV_GROUPS)], axis=1)
            dlt = y - _block_sums([y], m_kk_b)[0] * inv_hd
            var = _block_sums([dlt * dlt], m_kk_b)[0] * inv_hd
            yn = dlt * lax.rsqrt(var + GN_EPS) * lnw_ref[...] + lnb_ref[...]
            o_ref[n] = ((yn + pre[n]["bonus"] * pre[n]["vr"]) * _silu(z_ref[n])).astype(o_ref.dtype)

    yield from _stagger(prepare, algebra, nb)
    yield MAIN_DONE

    @pl.when(i == nsteps - 1)
    def _():
        for n in range(nb):
            for h in range(RWKV_HEADS):
                sfin_ref[n, h] = sbt[n, h // PACK, hsl(h), hsl(h)]


N_GDN_IN, N_RWKV_IN = 11, 24
MAIN_DONE = "main stages done"


def _scan_kernel(*refs, nb, chunk, nsteps):
    g_in = refs[0:N_GDN_IN]
    r_in = refs[N_GDN_IN:N_GDN_IN + N_RWKV_IN]
    og_ref, sg_ref, or_ref, sr_ref, ubuf, sb, sbuf, lbuf, sbt = refs[N_GDN_IN + N_RWKV_IN:]
    kw = dict(nb=nb, chunk=chunk, nsteps=nsteps)
    bodies = [_gdn_body(*g_in, og_ref, sg_ref, ubuf, sb, **kw), _rwkv_body(*r_in, or_ref, sr_ref, sbuf, lbuf, sbt, **kw)]
    for body in bodies:
        next(body)
    active = list(bodies)
    while active:
        for body in list(active):
            if next(body) is MAIN_DONE:
                active.remove(body)
    for body in bodies:
        for _ in body:
            pass


def _scans(proj3, conv_prev, s_gdn, shift_prev, s_rwkv, p, nb):
    b, t, _ = proj3.shape
    L = min(GDN_CHUNK, t)
    assert b % nb == 0 and t % L == 0 and L % SUBLANE == 0
    nsteps = t // L
    blk = lambda c, w: pl.BlockSpec((nb, L, w), lambda g, i, c=c: (g, i, c))
    const = lambda shape: pl.BlockSpec(shape, lambda g, i: (0,) * len(shape))
    per_seq = lambda *shape: pl.BlockSpec((nb,) + shape, lambda g, i: (g,) + (0,) * len(shape))
    prev_pad = jnp.pad(conv_prev, ((0, 0), (SUBLANE - (CONV_W - 1), 0), (0, 0)))
    lane_pad = (GDN_HEADS, LANE - 2 * GDN_HEADS)
    alog_row = jnp.pad(p["a_log"], lane_pad).reshape(1, LANE)
    dtb_row = jnp.pad(p["dt_bias"], lane_pad).reshape(1, LANE)
    on_row = jnp.tile(p["onorm_g"], GDN_HEADS).reshape(1, GDN_W)
    gdn_state = per_seq(GDN_HEADS, GDN_DK, GDN_DV)
    gdn_specs = [
        blk(COL_Q, SEG), blk(COL_K, SEG), blk(COL_V, SEG), blk(COL_ZG, SEG), blk(COL_BA // LANE, LANE),
        const((CONV_W, 3 * GDN_W)), per_seq(SUBLANE, 3 * GDN_W), gdn_state,
        const((1, LANE)), const((1, LANE)), const((1, GDN_W)),
    ]
    gdn_args = [proj3] * 5 + [p["conv_w"], prev_pad, s_gdn, alog_row, dtb_row, on_row]
    w3 = 3 * RWKV_W
    padl = lambda x, n: jnp.pad(x, [(0, 0)] * (x.ndim - 1) + [(0, LANE - n)])
    p_r, p_k, p_v = shift_prev[..., 0:RWKV_W], shift_prev[..., RWKV_W:2 * RWKV_W], shift_prev[..., 2 * RWKV_W:w3]
    p_xw = padl(shift_prev[..., w3:w3 + W_LORA], W_LORA)
    p_xa = padl(shift_prev[..., w3 + W_LORA:], A_LORA)
    mu = p["mu"]
    mu_rkv = mu[0:w3].reshape(1, w3)
    mu_w = padl(mu[w3:w3 + W_LORA], W_LORA).reshape(1, LANE)
    mu_a = padl(mu[w3 + W_LORA:], A_LORA).reshape(1, LANE)
    wl_pad = jnp.pad(p["w_lora"], ((0, LANE - W_LORA), (0, 0))).astype(BF16)
    al_pad = jnp.pad(p["a_lora"], ((0, LANE - A_LORA), (0, 0))).astype(BF16)
    s0_t = s_rwkv.transpose(0, 1, 3, 2)
    row1 = lambda x: x.reshape(1, RWKV_W)
    rwkv_state = per_seq(RWKV_HEADS, RWKV_HD, RWKV_HD)
    rwkv_specs = [
        blk(COL_R, SEG), blk(COL_RK, SEG), blk(COL_RV, SEG),
        blk(COL_XW // LANE, LANE), blk(COL_XA // LANE, LANE), blk(COL_ZR, SEG),
        per_seq(1, SEG), per_seq(1, SEG), per_seq(1, SEG), per_seq(1, LANE), per_seq(1, LANE),
        rwkv_state,
        const((1, w3)), const((1, LANE)), const((1, LANE)),
        const((1, RWKV_W)), const((LANE, RWKV_W)), const((1, RWKV_W)), const((LANE, RWKV_W)),
        const((1, RWKV_W)), const((1, RWKV_W)), const((1, RWKV_W)), const((1, RWKV_W)), const((1, RWKV_W)),
    ]
    rwkv_args = [proj3] * 6 + [p_r, p_k, p_v, p_xw, p_xa, s0_t, mu_rkv, mu_w, mu_a, row1(p["w0"]), wl_pad,
                               row1(p["a0"]), al_pad, row1(p["k_k"]), row1(p["k_a"]), row1(p["r_k"]),
                               row1(p["ln_w"]), row1(p["ln_b"])]
    assert len(gdn_specs) == N_GDN_IN and len(rwkv_specs) == N_RWKV_IN
    o_g, sg, o_r, sr = pl.pallas_call(
        functools.partial(_scan_kernel, nb=nb, chunk=L, nsteps=nsteps),
        grid=(b // nb, nsteps),
        in_specs=gdn_specs + rwkv_specs,
        out_specs=[
            pl.BlockSpec((nb, L, GDN_W), lambda g, i: (g, i, 0)), gdn_state,
            pl.BlockSpec((nb, L, RWKV_W), lambda g, i: (g, i, 0)), rwkv_state,
        ],
        out_shape=[
            jax.ShapeDtypeStruct((b, t, GDN_W), BF16),
            jax.ShapeDtypeStruct((b, GDN_HEADS, GDN_DK, GDN_DV), F32),
            jax.ShapeDtypeStruct((b, t, RWKV_W), BF16),
            jax.ShapeDtypeStruct((b, RWKV_HEADS, RWKV_HD, RWKV_HD), F32),
        ],
        scratch_shapes=[
            pltpu.VMEM((nb, L + SUBLANE, 3 * GDN_W), F32),
            pltpu.VMEM((nb, GDN_PAIRS, GROUP, GROUP), F32),
            pltpu.VMEM((nb, L + SUBLANE, w3), F32),
            pltpu.VMEM((nb, L + SUBLANE, 2 * LANE), F32),
            pltpu.VMEM((nb, RWKV_GROUPS, GROUP, GROUP), F32),
        ],
        compiler_params=pltpu.CompilerParams(
            dimension_semantics=("arbitrary", "arbitrary"), vmem_limit_bytes=VMEM_LIMIT),
        name="scan",
    )(*gdn_args, *rwkv_args)
    return o_g.reshape(b * t, GDN_W), sg, o_r.reshape(b * t, RWKV_W), sr.transpose(0, 1, 3, 2)


def _merge_kernel(og_ref, or_ref, gg_ref, gr_ref, x_ref, gate_ref, wog_ref, wor_ref, wout_ref, fg_ref, o_ref,
                  *, final_norm):
    m = _sigmoid(gg_ref[...]) * _mm(og_ref[...], wog_ref[...]) \
        + _sigmoid(gr_ref[...]) * _mm(or_ref[...], wor_ref[...])
    out = _mm(m.astype(BF16), wout_ref[...])
    xn = x_ref[...] + gate_ref[...] * out
    if final_norm:
        xn = xn * lax.rsqrt(jnp.mean(xn * xn, axis=-1, keepdims=True) + EPS) * fg_ref[...]
    o_ref[...] = xn


def _merge(o_g, o_r, proj, x2d, gate, w_og, w_or, w_out, final_g, rows_per_mod, final_norm):
    m, d = x2d.shape
    tm = min(m, 256) if rows_per_mod == 1 else min(rows_per_mod, 256)
    if rows_per_mod == 1:
        gate_spec = pl.BlockSpec((tm, d), lambda i: (i, 0))
    else:
        assert rows_per_mod % tm == 0
        per = rows_per_mod // tm
        gate = gate.reshape(-1, 1, d)
        gate_spec = pl.BlockSpec((None, 1, d), lambda i: (i // per, 0, 0))
    whole = lambda shape: pl.BlockSpec(shape, lambda i: (0, 0))
    gcol = COL_GATES // d
    return pl.pallas_call(
        functools.partial(_merge_kernel, final_norm=final_norm),
        grid=(m // tm,),
        in_specs=[
            pl.BlockSpec((tm, GDN_W), lambda i: (i, 0)),
            pl.BlockSpec((tm, RWKV_W), lambda i: (i, 0)),
            pl.BlockSpec((tm, d), lambda i: (i, gcol)),
            pl.BlockSpec((tm, d), lambda i: (i, gcol + 1)),
            pl.BlockSpec((tm, d), lambda i: (i, 0)),
            gate_spec,
            whole((GDN_W, d)), whole((RWKV_W, d)), whole((d, d)), whole((1, d)),
        ],
        out_specs=pl.BlockSpec((tm, d), lambda i: (i, 0)),
        out_shape=jax.ShapeDtypeStruct((m, d), F32),
        compiler_params=pltpu.CompilerParams(
            dimension_semantics=("arbitrary",), vmem_limit_bytes=VMEM_LIMIT),
        name="merge_out",
    )(o_g, o_r, proj, proj, x2d, gate, w_og, w_or, w_out, final_g.reshape(1, d))


def _pad_in_weight(w_in):
    d = w_in.shape[0]
    o_zg = 3 * GDN_W
    o_b = o_zg + GDN_W
    o_rw = o_b + 2 * GDN_HEADS
    o_xw = o_rw + 3 * RWKV_W
    o_xa = o_xw + W_LORA
    o_zr = o_xa + A_LORA
    o_br = o_zr + RWKV_W
    z = lambda n: jnp.zeros((d, n), w_in.dtype)
    cols = [
        w_in[:, 0:o_zg], w_in[:, o_zg:o_b],
        w_in[:, o_rw:o_xw], w_in[:, o_zr:o_br],
        w_in[:, o_br:o_br + 2 * D_MODEL],
        w_in[:, o_b:o_rw], z(LANE - 2 * GDN_HEADS),
        w_in[:, o_xw:o_xa], z(LANE - W_LORA),
        w_in[:, o_xa:o_zr], z(LANE - A_LORA),
    ]
    used = COL_XA + LANE
    cols.append(z(N_PROJ - used))
    return jnp.concatenate(cols, axis=1).astype(BF16)


def _layer(x2d, batch, seq, scale, shift, gate, rows_per_mod, conv_prev, s_gdn, shift_prev, s_rwkv, p, final_g,
           final_norm, scan_nb):
    proj = _in_proj(x2d, scale, shift, p["norm_g"], p["w_in_pad"], rows_per_mod)
    proj3 = proj.reshape(batch, seq, N_PROJ)
    o_g, s_gdn_new, o_r, s_rwkv_new = _scans(proj3, conv_prev, s_gdn, shift_prev, s_rwkv, p, scan_nb)
    x_new = _merge(o_g, o_r, proj, x2d, gate, p["w_o_gdn"], p["w_o_rwkv"], p["w_out"], final_g, rows_per_mod,
                   final_norm)
    conv_new = proj3[:, seq - (CONV_W - 1):, 0:3 * GDN_W]
    last = proj3[:, seq - 1:, :]
    shift_new = jnp.concatenate(
        [last[..., COL_R * SEG:COL_R * SEG + 3 * RWKV_W], last[..., COL_XW:COL_XW + W_LORA],
         last[..., COL_XA:COL_XA + A_LORA]], axis=-1)
    return x_new, conv_new, s_gdn_new, shift_new, s_rwkv_new


def _forward(x_prompt, x_sample, c_prompt, c_sample, cache_gdn_conv, state_gdn, cache_rwkv_shift, state_rwkv,
             ada_w, ada_b, norm_g, w_in, gdn_conv_w, gdn_a_log, gdn_dt_bias, gdn_out_norm_g,
             rwkv_mu, rwkv_w0, rwkv_w_lora, rwkv_a0, rwkv_a_lora, rwkv_k_k, rwkv_k_a, rwkv_r_k,
             rwkv_ln_w, rwkv_ln_b, w_o_gdn, w_o_rwkv, w_out, final_norm_g):
    depth = ada_w.shape[0]
    bp, tp, d = x_prompt.shape
    bs, ts, _ = x_sample.shape
    assert CONV_W - 1 <= min(tp, ts)
    c_all = jnp.concatenate([c_prompt, c_sample], axis=0)
    rows = -(-(bp + bs) // SUBLANE) * SUBLANE
    c_all = jnp.pad(c_all, ((0, rows - (bp + bs)), (0, 0)))
    mod = _ada_mod(c_all, ada_w, ada_b)
    xp = x_prompt.reshape(bp * tp, d)
    xs = x_sample.reshape(bs * ts, d)
    outs = [[] for _ in range(8)]
    for l in range(depth):
        p = dict(norm_g=norm_g[l], w_in_pad=_pad_in_weight(w_in[l]), conv_w=gdn_conv_w[l], a_log=gdn_a_log[l],
                 dt_bias=gdn_dt_bias[l], onorm_g=gdn_out_norm_g[l], mu=rwkv_mu[l], w0=rwkv_w0[l],
                 w_lora=rwkv_w_lora[l], a0=rwkv_a0[l], a_lora=rwkv_a_lora[l], k_k=rwkv_k_k[l], k_a=rwkv_k_a[l],
                 r_k=rwkv_r_k[l].reshape(-1), ln_w=rwkv_ln_w[l], ln_b=rwkv_ln_b[l],
                 w_o_gdn=w_o_gdn[l].astype(BF16), w_o_rwkv=w_o_rwkv[l].astype(BF16), w_out=w_out[l].astype(BF16))
        last = l == depth - 1
        m_p = mod[l, 0:bp]
        m_s = jnp.repeat(mod[l, bp:bp + bs], ts, axis=0)
        sh_p, sc_p, gt_p = m_p[:, 0:d], m_p[:, d:2 * d], m_p[:, 2 * d:]
        sh_s, sc_s, gt_s = m_s[:, 0:d], m_s[:, d:2 * d], m_s[:, 2 * d:]
        zeros = lambda *s: jnp.zeros(s, F32)
        xp, c1, g1, h1, r1 = _layer(
            xp, bp, tp, sc_p, sh_p, gt_p, tp,
            zeros(bp, CONV_W - 1, 3 * GDN_W), zeros(bp, GDN_HEADS, GDN_DK, GDN_DV),
            zeros(bp, 1, 3 * RWKV_W + W_LORA + A_LORA), zeros(bp, RWKV_HEADS, RWKV_HD, RWKV_HD),
            p, final_norm_g, last, scan_nb=bp)
        xs, c2, g2, h2, r2 = _layer(
            xs, bs, ts, sc_s, sh_s, gt_s, 1,
            cache_gdn_conv[l], state_gdn[l], cache_rwkv_shift[l], state_rwkv[l],
            p, final_norm_g, last, scan_nb=min(bs, 2))
        for lst, val in zip(outs, (c1, g1, h1, r1, c2, g2, h2, r2)):
            lst.append(val)
    stk = [jnp.stack(v) for v in outs]
    return (xp.reshape(bp, tp, d), xs.reshape(bs, ts, d), *stk)


def kernel(x_prompt, x_sample, c_prompt, c_sample, cache_gdn_conv, state_gdn, cache_rwkv_shift, state_rwkv, ada_w, ada_b, norm_g, w_in, gdn_conv_w, gdn_a_log, gdn_dt_bias, gdn_out_norm_g, rwkv_mu, rwkv_w0, rwkv_w_lora, rwkv_a0, rwkv_a_lora, rwkv_k_k, rwkv_k_a, rwkv_r_k, rwkv_ln_w, rwkv_ln_b, w_o_gdn, w_o_rwkv, w_out, final_norm_g):
    return _forward(x_prompt, x_sample, c_prompt, c_sample, cache_gdn_conv, state_gdn, cache_rwkv_shift,
                    state_rwkv, ada_w, ada_b, norm_g, w_in, gdn_conv_w, gdn_a_log, gdn_dt_bias, gdn_out_norm_g,
                    rwkv_mu, rwkv_w0, rwkv_w_lora, rwkv_a0, rwkv_a_lora, rwkv_k_k, rwkv_k_a, rwkv_r_k,
                    rwkv_ln_w, rwkv_ln_b, w_o_gdn, w_o_rwkv, w_out, final_norm_g)
```
